```python
import math
import jax, jax.numpy as jnp
from jax import lax
import numpy as np

D_MODEL = 2048
BATCH = 2
SEQ = 16384
DEPTH = 2

F32 = jnp.float32
HG_HEADS = 4
HG_KDIM = 128
HG_VDIM = 128
HG_CHUNK = 64
RET_HEADS = 4
RET_KDIM = 64
RET_VDIM = 128
RET_CHUNK = 128
ROPE_BASE = 10000.0
ATT_HEADS = 8
ATT_HDIM = 128
DILATED_CONFIGS = ((128, 1), (512, 4), (2048, 16))
N_BUCKETS = 32
MAX_DISTANCE = 2048
D_FF = 5504
CONV_WIDTH = 3
EPS = 1e-6

HG_QK = HG_HEADS * HG_KDIM
HG_W = HG_HEADS * HG_VDIM
RET_QK = RET_HEADS * RET_KDIM
RET_W = RET_HEADS * RET_VDIM
ATT_W = ATT_HEADS * ATT_HDIM
MIX_W = HG_W + RET_W + ATT_W
IN_SIZES = (HG_QK, HG_QK, HG_W, HG_W, RET_QK, RET_QK, RET_W, RET_W, ATT_W, ATT_W, ATT_W)
IN_W = sum(IN_SIZES)

kernel_name = 'hybrid_hgrn2_retention_dilated_attn_convffn'


def rms_norm(x, g=None):
    xf = x.astype(F32)
    y = xf * lax.rsqrt(jnp.mean(xf * xf, axis=-1, keepdims=True) + EPS)
    if g is not None:
        y = y * g.astype(F32)
    return y.astype(x.dtype)


def hgrn2_mixer(q, f_logit, i, g, lower_bound):
    dtype = q.dtype
    B, S, _ = q.shape
    N = S // HG_CHUNK
    q = jax.nn.silu(q.astype(F32))
    xf = f_logit.astype(F32)
    lb = lower_bound.astype(F32)
    log_f = jnp.log(lb + (1.0 - lb) * jax.nn.sigmoid(xf))
    k = (1.0 - lb) * jax.nn.sigmoid(-xf)
    v = i.astype(F32)

    def to_chunks(t, d):
        return t.reshape(B, N, HG_CHUNK, HG_HEADS, d).transpose(1, 0, 3, 2, 4)

    causal = jnp.tril(jnp.ones((HG_CHUNK, HG_CHUNK), dtype=bool))

    def step(state, chunk):
        qc, kc, lfc, vc = chunk
        b = jnp.cumsum(lfc, axis=2)
        rel = jnp.where(causal[:, :, None], b[:, :, :, None, :] - b[:, :, None, :, :], -jnp.inf)
        scores = jnp.einsum('bhtc,bhsc,bhtsc->bhts', qc, kc, jnp.exp(rel))
        out = (jnp.einsum('bhts,bhsv->bhtv', scores, vc)
               + jnp.einsum('bhtc,bhcv->bhtv', qc * jnp.exp(b), state))
        b_last = b[:, :, -1, :]
        new_state = (jnp.exp(b_last)[..., None] * state
                     + jnp.einsum('bhsc,bhsv->bhcv', kc * jnp.exp(b_last[:, :, None, :] - b), vc))
        return new_state, out

    state0 = jnp.zeros((B, HG_HEADS, HG_KDIM, HG_VDIM), F32)
    _, o = lax.scan(step, state0, (to_chunks(q, HG_KDIM), to_chunks(k, HG_KDIM),
                                   to_chunks(log_f, HG_KDIM), to_chunks(v, HG_VDIM)))
    o = o.transpose(1, 0, 3, 2, 4).reshape(B, S, HG_HEADS, HG_VDIM)
    o = rms_norm(o).reshape(B, S, HG_W) * jax.nn.silu(g.astype(F32))
    return o.astype(dtype)


def apply_rope(t, cos, sin):
    half = t.shape[-1] // 2
    t1, t2 = t[..., :half], t[..., half:]
    c, s = cos[None, :, None, :], sin[None, :, None, :]
    return jnp.concatenate([t1 * c - t2 * s, t1 * s + t2 * c], axis=-1)


def retention_mixer(q, k, v, g, cos, sin):
    dtype = q.dtype
    B, S, _ = q.shape
    C = RET_CHUNK
    N = S // C
    q = apply_rope(q.astype(F32).reshape(B, S, RET_HEADS, RET_KDIM), cos, sin)
    k = apply_rope(k.astype(F32).reshape(B, S, RET_HEADS, RET_KDIM), cos, sin) * RET_KDIM ** -0.5
    v = v.astype(F32).reshape(B, S, RET_HEADS, RET_VDIM)
    log_gamma = jnp.log(1.0 - 2.0 ** (-5.0 - jnp.arange(RET_HEADS, dtype=F32)))
    qc = q.reshape(B, N, C, RET_HEADS, RET_KDIM)
    kc = k.reshape(B, N, C, RET_HEADS, RET_KDIM)
    vc = v.reshape(B, N, C, RET_HEADS, RET_VDIM)
    pos = jnp.arange(C)
    dist = pos[:, None] - pos[None, :]
    decay = jnp.where(dist >= 0, jnp.exp(log_gamma[:, None, None] * jnp.maximum(dist, 0)), 0.0)
    scores = jnp.einsum('bnthd,bnshd->bnhts', qc, kc) * decay
    intra = jnp.einsum('bnhts,bnshv->bnthv', scores, vc)
    to_end = jnp.exp(log_gamma[:, None] * (C - 1 - pos)[None, :])
    from_start = jnp.exp(log_gamma[:, None] * (pos + 1)[None, :])
    chunk_summary = jnp.einsum('bnshd,hs,bnshv->bnhdv', kc, to_end, vc)
    ci = jnp.arange(N)
    cd = ci[:, None] - ci[None, :] - 1
    chunk_decay = jnp.where(cd >= 0, jnp.exp(log_gamma[:, None, None] * C * jnp.maximum(cd, 0)), 0.0)
    state_in = jnp.einsum('hnm,bmhdv->bnhdv', chunk_decay, chunk_summary)
    inter = jnp.einsum('bnthd,ht,bnhdv->bnthv', qc, from_start, state_in)
    o = rms_norm((intra + inter).reshape(B, S, RET_HEADS, RET_VDIM)).reshape(B, S, RET_W)
    return (jax.nn.silu(g.astype(F32)) * o).astype(dtype)


def t5_bucket(distance):
    max_exact = N_BUCKETS // 2
    n = jnp.maximum(distance, 1).astype(F32)
    large = max_exact + (jnp.log(n / max_exact) / math.log(MAX_DISTANCE / max_exact)
                         * (N_BUCKETS - max_exact)).astype(jnp.int32)
    large = jnp.minimum(large, N_BUCKETS - 1)
    return jnp.where(distance < max_exact, distance, large)


def dilated_branch(q, k, v, window, dilation, rel_bias):
    B, S, H, hd = q.shape
    blk = window // dilation
    L = S // dilation

    def by_residue(t):
        return t.reshape(B, L, dilation, H, hd).transpose(0, 2, 3, 1, 4)

    qs, ks, vs = by_residue(q), by_residue(k), by_residue(v)
    nb = -(-L // blk)
    Lp = nb * blk
    qb = jnp.pad(qs, ((0, 0), (0, 0), (0, 0), (0, Lp - L), (0, 0))).reshape(B, dilation, H, nb, blk, hd)
    kv_pad = ((0, 0), (0, 0), (0, 0), (blk, Lp - L), (0, 0))
    kp = jnp.pad(ks, kv_pad).reshape(B, dilation, H, nb + 1, blk, hd)
    vp = jnp.pad(vs, kv_pad).reshape(B, dilation, H, nb + 1, blk, hd)
    keys = jnp.concatenate([kp[:, :, :, :-1], kp[:, :, :, 1:]], axis=-2)
    vals = jnp.concatenate([vp[:, :, :, :-1], vp[:, :, :, 1:]], axis=-2)
    a = jnp.arange(blk)[:, None]
    kk = jnp.arange(2 * blk)[None, :]
    j = a + blk - kk
    key_idx = jnp.arange(nb)[:, None, None] * blk + kk[None] - blk
    valid = ((j >= 0) & (j <= blk))[None] & (key_idx >= 0)
    bias = rel_bias.astype(F32)[t5_bucket(jnp.clip(j, 0, blk) * dilation)].transpose(2, 0, 1)
    logits = jnp.einsum('bdhnqe,bdhnke->bdhnqk', qb, keys) + bias[:, None]
    logits = jnp.where(valid, logits, -jnp.inf)
    m = jnp.max(logits, axis=-1)
    p = jnp.exp(logits - m[..., None])
    s = jnp.sum(p, axis=-1)
    o = jnp.einsum('bdhnqk,bdhnke->bdhnqe', p, vals) / s[..., None]
    o = o.reshape(B, dilation, H, Lp, hd)[:, :, :, :L].transpose(0, 3, 1, 2, 4).reshape(B, S, H, hd)
    m = m.reshape(B, dilation, H, Lp)[..., :L].transpose(0, 3, 1, 2).reshape(B, S, H)
    s = s.reshape(B, dilation, H, Lp)[..., :L].transpose(0, 3, 1, 2).reshape(B, S, H)
    return o, m, s


def dilated_attention(q, k, v, rel_bias):
    dtype = q.dtype
    B, S, _ = q.shape
    q = q.astype(F32).reshape(B, S, ATT_HEADS, ATT_HDIM) * ATT_HDIM ** -0.5
    k = k.astype(F32).reshape(B, S, ATT_HEADS, ATT_HDIM)
    v = v.astype(F32).reshape(B, S, ATT_HEADS, ATT_HDIM)
    outs, maxes, sums = [], [], []
    for window, dilation in DILATED_CONFIGS:
        o, m, s = dilated_branch(q, k, v, window, dilation, rel_bias)
        outs.append(o)
        maxes.append(m)
        sums.append(s)
    M = jnp.stack(maxes)
    wts = jnp.stack(sums) * jnp.exp(M - jnp.max(M, axis=0))
    o = jnp.einsum('cbsh,cbshe->bshe', wts, jnp.stack(outs)) / jnp.sum(wts, axis=0)[..., None]
    return o.reshape(B, S, ATT_W).astype(dtype)


def causal_dwconv(h, w, b):
    K = w.shape[0]
    out = lax.conv_general_dilated(h, w[:, None, :], window_strides=(1,), padding=[(K - 1, 0)],
                                   dimension_numbers=('NWC', 'WIO', 'NWC'),
                                   feature_group_count=h.shape[-1])
    return out + b


def setup_inputs(seed: int = 0) -> dict:
    key = jax.random.key(seed)
    ks = jax.random.split(key, 13)

    def normal(k, shape, scale):
        return jax.random.normal(k, shape, F32) * scale

    return {
        'x': normal(ks[0], (BATCH, SEQ, D_MODEL), 1.0),
        'norm_mix': 1.0 + normal(ks[1], (DEPTH, D_MODEL), 0.02),
        'w_in': normal(ks[2], (DEPTH, D_MODEL, IN_W), D_MODEL ** -0.5),
        'hg_lower_bound': normal(ks[3], (DEPTH, HG_QK), 0.5),
        'w_out': normal(ks[4], (DEPTH, MIX_W, D_MODEL), MIX_W ** -0.5),
        'norm_ffn': 1.0 + normal(ks[5], (DEPTH, D_MODEL), 0.02),
        'w_gate': normal(ks[6], (DEPTH, D_MODEL, D_FF), D_MODEL ** -0.5),
        'conv_w': normal(ks[7], (DEPTH, CONV_WIDTH, D_FF), CONV_WIDTH ** -0.5),
        'conv_b': normal(ks[8], (DEPTH, D_FF), 0.02),
        'w_up': normal(ks[9], (DEPTH, D_MODEL, D_FF), D_MODEL ** -0.5),
        'w_down': normal(ks[10], (DEPTH, D_FF, D_MODEL), D_FF ** -0.5),
        'rel_bias': normal(ks[11], (N_BUCKETS, ATT_HEADS), 0.2),
        'norm_final': 1.0 + normal(ks[12], (D_MODEL,), 0.02),
    }


def reference(x, norm_mix, w_in, hg_lower_bound, w_out, norm_ffn, w_gate, conv_w, conv_b,
              w_up, w_down, rel_bias, norm_final):
    S = x.shape[1]
    inv_freq = ROPE_BASE ** (-jnp.arange(0, RET_KDIM, 2, dtype=F32) / RET_KDIM)
    ang = jnp.arange(S, dtype=F32)[:, None] * inv_freq[None, :]
    cos, sin = jnp.cos(ang), jnp.sin(ang)
    lb_all = jnp.cumsum(jax.nn.softmax(hg_lower_bound.astype(F32), axis=0), axis=0)
    lb_all = lb_all - lb_all[0]
    split_at = [int(o) for o in np.cumsum(IN_SIZES)[:-1]]
    for l in range(DEPTH):
        h = rms_norm(x, norm_mix[l])
        proj = h @ w_in[l]
        hq, hf, hi, hg, rq, rk, rv, rg, aq, ak, av = jnp.split(proj, split_at, axis=-1)
        mixed = jnp.concatenate([
            hgrn2_mixer(hq, hf, hi, hg, lb_all[l]),
            retention_mixer(rq, rk, rv, rg, cos, sin),
            dilated_attention(aq, ak, av, rel_bias),
        ], axis=-1)
        x = x + mixed @ w_out[l]
        u = rms_norm(x, norm_ffn[l])
        gate = causal_dwconv(u @ w_gate[l], conv_w[l], conv_b[l])
        x = x + (jax.nn.silu(gate) * (u @ w_up[l])) @ w_down[l]
    return rms_norm(x, norm_final)
```

```python
import functools
import math

import numpy as np
import jax
import jax.numpy as jnp
from jax import lax
from jax.experimental import pallas as pl
from jax.experimental.pallas import tpu as pltpu

F32 = jnp.float32
BF16 = jnp.bfloat16

D_MODEL = 2048
DEPTH = 2
HG_HEADS = 4
HG_KDIM = 128
HG_VDIM = 128
RET_HEADS = 4
RET_KDIM = 64
RET_VDIM = 128
ROPE_BASE = 10000.0
ATT_HEADS = 8
ATT_HDIM = 128
DILATED_CONFIGS = ((128, 1), (512, 4), (2048, 16))
N_BUCKETS = 32
MAX_DISTANCE = 2048
D_FF = 5504
EPS = 1e-6

HG_QK = HG_HEADS * HG_KDIM
HG_W = HG_HEADS * HG_VDIM
RET_QK = RET_HEADS * RET_KDIM
RET_W = RET_HEADS * RET_VDIM
ATT_W = ATT_HEADS * ATT_HDIM
HG_COLS = 2 * HG_QK + 2 * HG_W
RET_COLS = 2 * RET_QK + 2 * RET_W
ATT_COLS = 3 * ATT_W

LANES = 128
SUBLANES = 8
D_FF_PAD = 5632
ATT_BLK = 128
NEG = -1e30
MIB = 1024 * 1024


def _dot(a, b):
    return jnp.dot(a, b, preferred_element_type=F32)


def _dot_nt(a, b):
    return lax.dot_general(a, b, (((1,), (1,)), ((), ())), preferred_element_type=F32)


def _dot_tn(a, b):
    return lax.dot_general(a, b, (((0,), (0,)), ((), ())), preferred_element_type=F32)


def _params(semantics, vmem_mib):
    return pltpu.CompilerParams(dimension_semantics=semantics, vmem_limit_bytes=vmem_mib * MIB)


def _rms(x):
    return x * lax.rsqrt(jnp.mean(x * x, axis=-1, keepdims=True) + EPS)


def _silu(x):
    return x * jax.nn.sigmoid(x)


def _norm_matmul_body(x_ref, g_ref, w_ref, o_ref, h_ref):
    @pl.when(pl.program_id(1) == 0)
    def _():
        h_ref[...] = (_rms(x_ref[...]) * g_ref[...]).astype(BF16)

    o_ref[...] = _dot(h_ref[...], w_ref[...]).astype(o_ref.dtype)


def _norm_matmul(x, g, w, out_dtype, tm, tn, name):
    t, d = x.shape
    n = w.shape[1]
    return pl.pallas_call(
        _norm_matmul_body,
        grid=(t // tm, n // tn),
        in_specs=[
            pl.BlockSpec((tm, d), lambda i, j: (i, 0)),
            pl.BlockSpec((1, d), lambda i, j: (0, 0)),
            pl.BlockSpec((d, tn), lambda i, j: (0, j)),
        ],
        out_specs=pl.BlockSpec((tm, tn), lambda i, j: (i, j)),
        out_shape=jax.ShapeDtypeStruct((t, n), out_dtype),
        scratch_shapes=[pltpu.VMEM((tm, d), BF16)],
        compiler_params=_params(("arbitrary", "arbitrary"), 48),
        name=name,
    )(x, g.reshape(1, d), w)


def _hgrn_body(q_ref, f_ref, i_ref, g_ref, lbp_ref, o_ref, st_ref, b_ref, k_ref, *, layer, chunk, n_chunks):
    c = chunk

    @pl.when(pl.program_id(2) == 0)
    def _():
        st_ref[...] = jnp.zeros_like(st_ref)

    p = lbp_ref[...]
    e = jnp.exp(p - jnp.max(p, axis=0, keepdims=True))
    sm = e / jnp.sum(e, axis=0, keepdims=True)
    lb = jnp.zeros((1, LANES), F32)
    for m in range(1, layer + 1):
        lb = lb + sm[m:m + 1, :]
    oml = 1.0 - lb

    row = lax.broadcasted_iota(jnp.int32, (c, c), 0)
    col = lax.broadcasted_iota(jnp.int32, (c, c), 1)
    tri = row >= col
    tri_bf = jnp.where(tri, 1.0, 0.0).astype(BF16)

    def one_chunk(ci, carry):
        rows = pl.ds(pl.multiple_of(ci * c, c), c)
        xf = f_ref[rows, :]
        logf = jnp.log(lb + oml * jax.nn.sigmoid(xf))
        kk = oml * jax.nn.sigmoid(-xf)
        hi = logf.astype(BF16)
        r1 = logf - hi.astype(F32)
        mid = r1.astype(BF16)
        lo = (r1 - mid.astype(F32)).astype(BF16)
        b = _dot(tri_bf, hi) + _dot(tri_bf, mid) + _dot(tri_bf, lo)
        b_ref[...] = b
        k_ref[...] = kk
        qs = _silu(q_ref[rows, :])
        v = i_ref[rows, :].astype(BF16)

        a = jnp.zeros((c, c), F32)
        for s in range(c):
            t0 = (s // SUBLANES) * SUBLANES
            bs = b_ref[pl.ds(s, 1), :]
            ks = k_ref[pl.ds(s, 1), :]
            pr = qs[t0:, :] * jnp.exp(b[t0:, :] - bs) * ks
            rs = jnp.sum(pr, axis=-1, keepdims=True)
            if t0:
                rs = jnp.concatenate([jnp.zeros((t0, 1), F32), rs], axis=0)
            a = jnp.where(col == s, rs, a)
        a = jnp.where(tri, a, 0.0)

        st = st_ref[...]
        b_last = b_ref[pl.ds(c - 1, 1), :]
        o = _dot(a.astype(BF16), v) + _dot_nt((qs * jnp.exp(b)).astype(BF16), st.astype(BF16))
        kt = (kk * jnp.exp(b_last - b)).astype(BF16)
        st_ref[...] = st * jnp.exp(b_last) + _dot_tn(v, kt)
        o_ref[rows, :] = (_rms(o) * _silu(g_ref[rows, :])).astype(o_ref.dtype)
        return carry

    lax.fori_loop(0, n_chunks, one_chunk, 0)


def _hgrn(hg, lb_param, layer, block, chunk):
    bsz, s, _ = hg.shape
    nq = HG_QK // LANES
    spec = lambda seg: pl.BlockSpec((None, block, LANES), lambda b, h, i, seg=seg: (b, i, seg * nq + h))
    return pl.pallas_call(
        functools.partial(_hgrn_body, layer=layer, chunk=chunk, n_chunks=block // chunk),
        grid=(bsz, HG_HEADS, s // block),
        in_specs=[spec(0), spec(1), spec(2), spec(3),
                  pl.BlockSpec((DEPTH, LANES), lambda b, h, i: (0, h))],
        out_specs=pl.BlockSpec((None, block, LANES), lambda b, h, i: (b, i, h)),
        out_shape=jax.ShapeDtypeStruct((bsz, s, HG_W), BF16),
        scratch_shapes=[pltpu.VMEM((HG_VDIM, HG_KDIM), F32),
                        pltpu.VMEM((chunk, LANES), F32),
                        pltpu.VMEM((chunk, LANES), F32)],
        compiler_params=_params(("arbitrary", "arbitrary", "arbitrary"), 32),
        name=f"hgrn2_l{layer}",
    )(hg, hg, hg, hg, lb_param)


def _ret_log_gamma(h):
    return math.log(1.0 - 2.0 ** (-5.0 - h))


def _ret_body(q_ref, k_ref, v_ref, g_ref, cos_ref, sin_ref, o_ref,
              st_ref, dm_ref, fs_ref, te_ref, bd_ref, *, chunk, n_chunks):
    c = chunk
    qk = RET_QK
    lane_head = lax.broadcasted_iota(jnp.int32, (c, qk), 1) // RET_KDIM

    def lane_log_gamma(head_idx):
        lg = jnp.full(head_idx.shape, _ret_log_gamma(0), F32)
        for h in range(1, RET_HEADS):
            lg = jnp.where(head_idx == h, _ret_log_gamma(h), lg)
        return lg

    @pl.when((pl.program_id(0) == 0) & (pl.program_id(1) == 0))
    def _():
        t = lax.broadcasted_iota(jnp.int32, (c, qk), 0).astype(F32)
        lg = lane_log_gamma(lane_head)
        fs_ref[...] = jnp.exp(lg * (t + 1.0))
        te_ref[...] = jnp.exp(lg * (c - 1.0 - t))
        dist = (lax.broadcasted_iota(jnp.int32, (c, c), 0) - lax.broadcasted_iota(jnp.int32, (c, c), 1))
        for h in range(RET_HEADS):
            dm_ref[h] = jnp.where(dist >= 0, jnp.exp(_ret_log_gamma(h) * jnp.maximum(dist, 0).astype(F32)), 0.0)
        rh = lax.broadcasted_iota(jnp.int32, (qk, RET_W), 0) // RET_KDIM
        ch = lax.broadcasted_iota(jnp.int32, (qk, RET_W), 1) // RET_VDIM
        bd_ref[...] = jnp.where(rh == ch, jnp.exp(lane_log_gamma(rh) * float(c)), 0.0)

    @pl.when(pl.program_id(1) == 0)
    def _():
        st_ref[...] = jnp.zeros_like(st_ref)

    in_first_half = (lax.broadcasted_iota(jnp.int32, (c, LANES), 1) % RET_KDIM) < (RET_KDIM // 2)

    def rope(t, cos, sin):
        parts = []
        for p in range(qk // LANES):
            th = t[:, p * LANES:(p + 1) * LANES]
            back = pltpu.roll(th, RET_KDIM // 2, axis=1)
            fwd = pltpu.roll(th, LANES - RET_KDIM // 2, axis=1)
            parts.append(jnp.where(in_first_half, fwd, back))
        return t * cos + jnp.concatenate(parts, axis=1) * sin

    def one_chunk(ci, carry):
        rows = pl.ds(pl.multiple_of(ci * c, c), c)
        cos = cos_ref[rows, :]
        sin = sin_ref[rows, :]
        qr = rope(q_ref[rows, :].astype(F32), cos, sin)
        kr = rope(k_ref[rows, :].astype(F32), cos, sin) * (RET_KDIM ** -0.5)
        kr_bf = kr.astype(BF16)
        v = v_ref[rows, :]
        st = st_ref[...]
        inter = _dot((qr * fs_ref[...]).astype(BF16), st.astype(BF16))
        g = g_ref[rows, :].astype(F32)
        for h in range(RET_HEADS):
            hs = slice(h * RET_VDIM, (h + 1) * RET_VDIM)
            qm = jnp.where(lane_head == h, qr, 0.0).astype(BF16)
            sc = _dot_nt(qm, kr_bf) * dm_ref[h]
            oh = _dot(sc.astype(BF16), v[:, hs]) + inter[:, hs]
            o_ref[rows, hs] = (_rms(oh) * _silu(g[:, hs])).astype(o_ref.dtype)
        upd = _dot_tn((kr * te_ref[...]).astype(BF16), v)
        bd = bd_ref[...]
        st_ref[...] = st * bd + jnp.where(bd > 0.0, upd, 0.0)
        return carry

    lax.fori_loop(0, n_chunks, one_chunk, 0)


def _retention(rt, cos_t, sin_t, block, chunk, name):
    bsz, s, _ = rt.shape
    return pl.pallas_call(
        functools.partial(_ret_body, chunk=chunk, n_chunks=block // chunk),
        grid=(bsz, s // block),
        in_specs=[
            pl.BlockSpec((None, block, RET_QK), lambda b, i: (b, i, 0)),
            pl.BlockSpec((None, block, RET_QK), lambda b, i: (b, i, 1)),
            pl.BlockSpec((None, block, RET_W), lambda b, i: (b, i, 1)),
            pl.BlockSpec((None, block, RET_W), lambda b, i: (b, i, 2)),
            pl.BlockSpec((block, RET_QK), lambda b, i: (i, 0)),
            pl.BlockSpec((block, RET_QK), lambda b, i: (i, 0)),
        ],
        out_specs=pl.BlockSpec((None, block, RET_W), lambda b, i: (b, i, 0)),
        out_shape=jax.ShapeDtypeStruct((bsz, s, RET_W), BF16),
        scratch_shapes=[
            pltpu.VMEM((RET_QK, RET_W), F32),
            pltpu.VMEM((RET_HEADS, chunk, chunk), F32),
            pltpu.VMEM((chunk, RET_QK), F32),
            pltpu.VMEM((chunk, RET_QK), F32),
            pltpu.VMEM((RET_QK, RET_W), F32),
        ],
        compiler_params=_params(("arbitrary", "arbitrary"), 32),
        name=name,
    )(rt, rt, rt, rt, cos_t, sin_t)


def _t5_bucket_np(distance):
    max_exact = N_BUCKETS // 2
    n = np.maximum(distance, 1).astype(np.float32)
    large = max_exact + (np.log(n / max_exact) / math.log(MAX_DISTANCE / max_exact)
                         * (N_BUCKETS - max_exact)).astype(np.int32)
    large = np.minimum(large, N_BUCKETS - 1)
    return np.where(distance < max_exact, distance, large)


def _bucket_table(dilation):
    a = np.arange(ATT_BLK)[:, None]
    kk = np.arange(2 * ATT_BLK)[None, :]
    j = a + ATT_BLK - kk
    valid = (j >= 0) & (j <= ATT_BLK)
    bucket = _t5_bucket_np(np.clip(j, 0, ATT_BLK) * dilation)
    return np.where(valid, bucket, -1).astype(np.int32)


def _attn_body(rb_ref, idx_ref, q_ref, kc_ref, kp_ref, vc_ref, vp_ref, o_ref, st_ref, bias_ref):
    blk = ATT_BLK

    @pl.when((pl.program_id(0) == 0) & (pl.program_id(1) == 0) & (pl.program_id(2) == 0))
    def _():
        idx = idx_ref[...]
        for h in range(ATT_HEADS):
            acc = jnp.full(idx.shape, NEG, F32)
            for n in range(N_BUCKETS):
                acc = jnp.where(idx == n, rb_ref[n, h], acc)
            bias_ref[h] = acc

    first_block = pl.program_id(2) == 0
    scale = ATT_HDIM ** -0.5
    lane = lax.broadcasted_iota(jnp.int32, (blk, LANES), 1)
    stats = jnp.zeros((blk, LANES), F32)
    for h in range(ATT_HEADS):
        hs = slice(h * ATT_HDIM, (h + 1) * ATT_HDIM)
        q = q_ref[:, hs]
        lp = _dot_nt(q, kp_ref[:, hs]) * scale + bias_ref[h, :, 0:blk]
        lp = jnp.where(first_block, NEG, lp)
        lc = _dot_nt(q, kc_ref[:, hs]) * scale + bias_ref[h, :, blk:2 * blk]
        m = jnp.maximum(jnp.max(lp, axis=-1, keepdims=True), jnp.max(lc, axis=-1, keepdims=True))
        pp = jnp.exp(lp - m)
        pc = jnp.exp(lc - m)
        den = jnp.sum(pp, axis=-1, keepdims=True) + jnp.sum(pc, axis=-1, keepdims=True)
        o = _dot(pp.astype(BF16), vp_ref[:, hs]) + _dot(pc.astype(BF16), vc_ref[:, hs])
        o_ref[:, hs] = (o / den).astype(o_ref.dtype)
        stats = jnp.where(lane == h, m, stats)
        stats = jnp.where(lane == ATT_HEADS + h, den, stats)
    st_ref[...] = stats


def _attn_branch(at, rel_bias, dilation, name):
    bsz, s, _ = at.shape
    l = s // dilation
    blk = ATT_BLK
    nb = l // blk
    view = at.reshape(bsz, l, dilation * ATT_COLS)
    idx = jnp.asarray(_bucket_table(dilation))
    nseg = ATT_COLS // ATT_W
    cur = lambda seg: pl.BlockSpec((None, blk, ATT_W), lambda b, r, i, seg=seg: (b, i, nseg * r + seg))
    prev = lambda seg: pl.BlockSpec((None, blk, ATT_W),
                                    lambda b, r, i, seg=seg: (b, jnp.maximum(i - 1, 0), nseg * r + seg))
    o, st = pl.pallas_call(
        _attn_body,
        grid=(bsz, dilation, nb),
        in_specs=[
            pl.BlockSpec(memory_space=pltpu.SMEM),
            pl.BlockSpec((blk, 2 * blk), lambda b, r, i: (0, 0)),
            cur(0), cur(1), prev(1), cur(2), prev(2),
        ],
        out_specs=[
            pl.BlockSpec((None, blk, ATT_W), lambda b, r, i: (b, i, r)),
            pl.BlockSpec((None, blk, LANES), lambda b, r, i: (b, i, r)),
        ],
        out_shape=[
            jax.ShapeDtypeStruct((bsz, l, dilation * ATT_W), BF16),
            jax.ShapeDtypeStruct((bsz, l, dilation * LANES), F32),
        ],
        scratch_shapes=[pltpu.VMEM((ATT_HEADS, blk, 2 * blk), F32)],
        compiler_params=_params(("arbitrary", "arbitrary", "arbitrary"), 32),
        name=name,
    )(rel_bias, idx, view, view, view, view, view)
    return o.reshape(bsz * s, ATT_W), st.reshape(bsz * s, LANES)


def _merge_body(o1_ref, o2_ref, o3_ref, s1_ref, s2_ref, s3_ref, out_ref):
    o_refs = (o1_ref, o2_ref, o3_ref)
    s_refs = (s1_ref, s2_ref, s3_ref)
    for h in range(ATT_HEADS):
        hs = slice(h * ATT_HDIM, (h + 1) * ATT_HDIM)
        ms = [r[:, h:h + 1] for r in s_refs]
        dens = [r[:, ATT_HEADS + h:ATT_HEADS + h + 1] for r in s_refs]
        mx = jnp.maximum(jnp.maximum(ms[0], ms[1]), ms[2])
        ws = [d * jnp.exp(m - mx) for m, d in zip(ms, dens)]
        num = sum(w * r[:, hs].astype(F32) for w, r in zip(ws, o_refs))
        out_ref[:, hs] = (num / (ws[0] + ws[1] + ws[2])).astype(out_ref.dtype)


def _merge(outs, stats, tm, name):
    t = outs[0].shape[0]
    ospec = pl.BlockSpec((tm, ATT_W), lambda i: (i, 0))
    sspec = pl.BlockSpec((tm, LANES), lambda i: (i, 0))
    return pl.pallas_call(
        _merge_body,
        grid=(t // tm,),
        in_specs=[ospec] * 3 + [sspec] * 3,
        out_specs=ospec,
        out_shape=jax.ShapeDtypeStruct((t, ATT_W), BF16),
        compiler_params=_params(("arbitrary",), 32),
        name=name,
    )(*outs, *stats)


def _outproj_body(x_ref, a_ref, b_ref, c_ref, w_ref, o_ref):
    acc = x_ref[...]
    acc = acc + _dot(a_ref[...], w_ref[0:HG_W, :])
    acc = acc + _dot(b_ref[...], w_ref[HG_W:HG_W + RET_W, :])
    acc = acc + _dot(c_ref[...], w_ref[HG_W + RET_W:, :])
    o_ref[...] = acc


def _outproj(x, a, b, c, w, tm, name):
    t, d = x.shape
    row = lambda width: pl.BlockSpec((tm, width), lambda i: (i, 0))
    return pl.pallas_call(
        _outproj_body,
        grid=(t // tm,),
        in_specs=[row(d), row(HG_W), row(RET_W), row(ATT_W),
                  pl.BlockSpec(w.shape, lambda i: (0, 0))],
        out_specs=row(d),
        out_shape=jax.ShapeDtypeStruct((t, d), F32),
        compiler_params=_params(("arbitrary",), 48),
        name=name,
    )(x, a, b, c, w)


def _ffn1_body(x_ref, g_ref, wg_ref, wu_ref, cw_ref, cb_ref, o_ref, u_ref, carry_ref, *, tiles_per_seq):
    i = pl.program_id(0)
    j = pl.program_id(1)
    tm = x_ref.shape[0]

    @pl.when(j == 0)
    def _():
        u_ref[...] = (_rms(x_ref[...]) * g_ref[...]).astype(BF16)

    u = u_ref[...]
    gp = _dot(u, wg_ref[...])
    up = _dot(u, wu_ref[...])
    w0 = cw_ref[0:1, :]
    w1 = cw_ref[1:2, :]
    w2 = cw_ref[2:3, :]
    cb = cb_ref[...]

    gate = w2 * gp + w1 * pltpu.roll(gp, 1, axis=0) + w0 * pltpu.roll(gp, 2, axis=0) + cb
    o_ref[...] = (_silu(gate) * up).astype(o_ref.dtype)

    prev = jnp.where(i % tiles_per_seq == 0, 0.0, carry_ref[j])
    carry_ref[j] = gp[tm - SUBLANES:, :]
    top = gp[0:SUBLANES, :]
    r = lax.broadcasted_iota(jnp.int32, top.shape, 0)
    p1 = prev[SUBLANES - 1:SUBLANES, :]
    p2 = prev[SUBLANES - 2:SUBLANES - 1, :]
    t1 = jnp.where(r == 0, p1, pltpu.roll(top, 1, axis=0))
    t2 = jnp.where(r == 0, p2, jnp.where(r == 1, p1, pltpu.roll(top, 2, axis=0)))
    gate_top = w2 * top + w1 * t1 + w0 * t2 + cb
    o_ref[0:SUBLANES, :] = (_silu(gate_top) * up[0:SUBLANES, :]).astype(o_ref.dtype)


def _ffn1(x, g, wg, wu, cw, cb, seq_len, tm, tn, name):
    t, d = x.shape
    n = wg.shape[1]
    nj = n // tn
    return pl.pallas_call(
        functools.partial(_ffn1_body, tiles_per_seq=seq_len // tm),
        grid=(t // tm, nj),
        in_specs=[
            pl.BlockSpec((tm, d), lambda i, j: (i, 0)),
            pl.BlockSpec((1, d), lambda i, j: (0, 0)),
            pl.BlockSpec((d, tn), lambda i, j: (0, j)),
            pl.BlockSpec((d, tn), lambda i, j: (0, j)),
            pl.BlockSpec((cw.shape[0], tn), lambda i, j: (0, j)),
            pl.BlockSpec((1, tn), lambda i, j: (0, j)),
        ],
        out_specs=pl.BlockSpec((tm, tn), lambda i, j: (i, j)),
        out_shape=jax.ShapeDtypeStruct((t, n), BF16),
        scratch_shapes=[pltpu.VMEM((tm, d), BF16), pltpu.VMEM((nj, SUBLANES, tn), F32)],
        compiler_params=_params(("arbitrary", "arbitrary"), 56),
        name=name,
    )(x, g.reshape(1, d), wg, wu, cw, cb.reshape(1, n))


def _ffn2_body(x_ref, h_ref, w_ref, gf_ref, o_ref, acc_ref, *, final_norm):
    k = pl.program_id(1)

    @pl.when(k == 0)
    def _():
        acc_ref[...] = x_ref[...]

    acc_ref[...] += _dot(h_ref[...], w_ref[...])

    @pl.when(k == pl.num_programs(1) - 1)
    def _():
        y = acc_ref[...]
        if final_norm:
            y = _rms(y) * gf_ref[...]
        o_ref[...] = y


def _ffn2(x, h, w, g_final, final_norm, tm, tk, name):
    t, d = x.shape
    kdim = h.shape[1]
    return pl.pallas_call(
        functools.partial(_ffn2_body, final_norm=final_norm),
        grid=(t // tm, kdim // tk),
        in_specs=[
            pl.BlockSpec((tm, d), lambda i, k: (i, 0)),
            pl.BlockSpec((tm, tk), lambda i, k: (i, k)),
            pl.BlockSpec((tk, d), lambda i, k: (k, 0)),
            pl.BlockSpec((1, d), lambda i, k: (0, 0)),
        ],
        out_specs=pl.BlockSpec((tm, d), lambda i, k: (i, 0)),
        out_shape=jax.ShapeDtypeStruct((t, d), F32),
        scratch_shapes=[pltpu.VMEM((tm, d), F32)],
        compiler_params=_params(("arbitrary", "arbitrary"), 48),
        name=name,
    )(x, h, w, g_final.reshape(1, d))


def _rope_tables(s):
    inv_freq = ROPE_BASE ** (-jnp.arange(0, RET_KDIM, 2, dtype=F32) / RET_KDIM)
    ang = jnp.arange(s, dtype=F32)[:, None] * inv_freq[None, :]
    cos, sin = jnp.cos(ang), jnp.sin(ang)
    cos_t = jnp.tile(jnp.concatenate([cos, cos], axis=1), (1, RET_HEADS))
    sin_t = jnp.tile(jnp.concatenate([-sin, sin], axis=1), (1, RET_HEADS))
    return cos_t, sin_t


def kernel(x, norm_mix, w_in, hg_lower_bound, w_out, norm_ffn, w_gate, conv_w, conv_b,
           w_up, w_down, rel_bias, norm_final):
    bsz, s, d = x.shape
    t = bsz * s
    pad = D_FF_PAD - D_FF
    cos_t, sin_t = _rope_tables(s)
    x2 = x.reshape(t, d)
    for l in range(DEPTH):
        w_in_bf = w_in[l].astype(BF16)
        hg = _norm_matmul(x2, norm_mix[l], w_in_bf[:, :HG_COLS], F32, 1024, 512, f"in_hgrn_l{l}")
        rt = _norm_matmul(x2, norm_mix[l], w_in_bf[:, HG_COLS:HG_COLS + RET_COLS], BF16, 1024, 512, f"in_ret_l{l}")
        at = _norm_matmul(x2, norm_mix[l], w_in_bf[:, HG_COLS + RET_COLS:], BF16, 1024, 512, f"in_attn_l{l}")

        a = _hgrn(hg.reshape(bsz, s, HG_COLS), hg_lower_bound, l, block=1024, chunk=32)
        b = _retention(rt.reshape(bsz, s, RET_COLS), cos_t, sin_t, block=512, chunk=128, name=f"retention_l{l}")
        at3 = at.reshape(bsz, s, ATT_COLS)
        outs, stats = zip(*[_attn_branch(at3, rel_bias, dil, f"attn_d{dil}_l{l}") for _, dil in DILATED_CONFIGS])
        c = _merge(outs, stats, 512, f"attn_merge_l{l}")

        x2 = _outproj(x2, a.reshape(t, HG_W), b.reshape(t, RET_W), c, w_out[l].astype(BF16), 512, f"out_proj_l{l}")

        wg = jnp.pad(w_gate[l], ((0, 0), (0, pad))).astype(BF16)
        wu = jnp.pad(w_up[l], ((0, 0), (0, pad))).astype(BF16)
        wd = jnp.pad(w_down[l], ((0, pad), (0, 0))).astype(BF16)
        cw = jnp.pad(conv_w[l], ((0, 0), (0, pad)))
        cb = jnp.pad(conv_b[l], ((0, pad),))
        hmid = _ffn1(x2, norm_ffn[l], wg, wu, cw, cb, s, 1024, 512, f"ffn_gate_up_l{l}")
        x2 = _ffn2(x2, hmid, wd, norm_final, l == DEPTH - 1, 512, 1408, f"ffn_down_l{l}")
    return x2.reshape(bsz, s, d)
```

```python
import functools
import math

import numpy as np
import jax
import jax.numpy as jnp
from jax import lax
from jax.experimental import pallas as pl
from jax.experimental.pallas import tpu as pltpu

F32 = jnp.float32
BF16 = jnp.bfloat16

D_MODEL = 2048
DEPTH = 2
HG_HEADS = 4
HG_KDIM = 128
HG_VDIM = 128
RET_HEADS = 4
RET_KDIM = 64
RET_VDIM = 128
ROPE_BASE = 10000.0
ATT_HEADS = 8
ATT_HDIM = 128
DILATED_CONFIGS = ((128, 1), (512, 4), (2048, 16))
N_BUCKETS = 32
MAX_DISTANCE = 2048
D_FF = 5504
EPS = 1e-6

HG_QK = HG_HEADS * HG_KDIM
HG_W = HG_HEADS * HG_VDIM
RET_QK = RET_HEADS * RET_KDIM
RET_W = RET_HEADS * RET_VDIM
ATT_W = ATT_HEADS * ATT_HDIM
HG_COLS = 2 * HG_QK + 2 * HG_W
RET_COLS = 2 * RET_QK + 2 * RET_W
ATT_COLS = 3 * ATT_W

LANES = 128
SUBLANES = 8
D_FF_PAD = 5632
ATT_BLK = 128
NEG = -1e30
LOG2E = math.log2(math.e)
MIB = 1024 * 1024


def _dot(a, b):
    return jnp.dot(a, b, preferred_element_type=F32)


def _dot_nt(a, b):
    return lax.dot_general(a, b, (((1,), (1,)), ((), ())), preferred_element_type=F32)


def _dot_tn(a, b):
    return lax.dot_general(a, b, (((0,), (0,)), ((), ())), preferred_element_type=F32)


def _params(semantics, vmem_mib):
    return pltpu.CompilerParams(dimension_semantics=semantics, vmem_limit_bytes=vmem_mib * MIB)


def _rms(x):
    return x * lax.rsqrt(jnp.mean(x * x, axis=-1, keepdims=True) + EPS)


def _silu(x):
    return x * jax.nn.sigmoid(x)


def _norm_matmul_body(x_ref, g_ref, w_ref, o_ref, h_ref):
    @pl.when(pl.program_id(1) == 0)
    def _():
        h_ref[...] = (_rms(x_ref[...]) * g_ref[...]).astype(BF16)

    o_ref[...] = _dot(h_ref[...], w_ref[...]).astype(o_ref.dtype)


def _norm_matmul(x, g, w, out_dtype, tm, tn, name):
    t, d = x.shape
    n = w.shape[1]
    return pl.pallas_call(
        _norm_matmul_body,
        grid=(t // tm, n // tn),
        in_specs=[
            pl.BlockSpec((tm, d), lambda i, j: (i, 0)),
            pl.BlockSpec((1, d), lambda i, j: (0, 0)),
            pl.BlockSpec((d, tn), lambda i, j: (0, j)),
        ],
        out_specs=pl.BlockSpec((tm, tn), lambda i, j: (i, j)),
        out_shape=jax.ShapeDtypeStruct((t, n), out_dtype),
        scratch_shapes=[pltpu.VMEM((tm, d), BF16)],
        compiler_params=_params(("arbitrary", "arbitrary"), 48),
        name=name,
    )(x, g.reshape(1, d), w)


def _hgrn_body(q_ref, f_ref, i_ref, g_ref, lbp_ref, o_ref, st_ref, b_ref, k_ref, *,
               layer, chunk, n_chunks, chunks_per_iter):
    c = chunk
    n_groups = c // SUBLANES

    @pl.when(pl.program_id(1) == 0)
    def _():
        st_ref[...] = jnp.zeros_like(st_ref)

    p = lbp_ref[...]
    e = jnp.exp(p - jnp.max(p, axis=0, keepdims=True))
    sm = e / jnp.sum(e, axis=0, keepdims=True)
    lb_all = jnp.zeros((1, HG_QK), F32)
    for m in range(1, layer + 1):
        lb_all = lb_all + sm[m:m + 1, :]

    tri = lax.broadcasted_iota(jnp.int32, (c, c), 0) >= lax.broadcasted_iota(jnp.int32, (c, c), 1)
    tri_bf = jnp.where(tri, 1.0, 0.0).astype(BF16)
    col = lax.broadcasted_iota(jnp.int32, (SUBLANES, c), 1)

    def stage_gates(rows, h):
        hs = slice(h * HG_KDIM, (h + 1) * HG_KDIM)
        lb = lb_all[:, hs]
        oml = 1.0 - lb
        xf = f_ref[rows, hs]
        log2f = jnp.log2(lb + oml * jax.nn.sigmoid(xf))
        kk = oml * jax.nn.sigmoid(-xf)
        hi = log2f.astype(BF16)
        r1 = log2f - hi.astype(F32)
        mid = r1.astype(BF16)
        lo = (r1 - mid.astype(F32)).astype(BF16)
        cs = _dot(tri_bf, jnp.concatenate([hi, mid, lo], axis=1))
        b = cs[:, 0:LANES] + cs[:, LANES:2 * LANES] + cs[:, 2 * LANES:3 * LANES]
        return b, kk, _silu(q_ref[rows, hs]), i_ref[rows, hs].astype(BF16)

    def stage_cross(slot, b, kk, qs):
        out = []
        for g in range(1, n_groups):
            r0 = g * SUBLANES
            ref = b_ref[slot, pl.ds(r0 - 1, 1), :]
            qt = (qs[r0:r0 + SUBLANES, :] * jnp.exp2(b[r0:r0 + SUBLANES, :] - ref)).astype(BF16)
            kt = (kk * jnp.exp2(jnp.minimum(ref - b, 0.0))).astype(BF16)
            out.append(_dot_nt(qt, kt))
        return out

    def stage_scores(slot, b, qs, cross):
        groups = []
        for g in range(n_groups):
            r0 = g * SUBLANES
            qg = qs[r0:r0 + SUBLANES, :]
            bg = b[r0:r0 + SUBLANES, :]
            ag = cross[g - 1] if g else jnp.zeros((SUBLANES, c), F32)
            for s in range(r0, r0 + SUBLANES):
                bs = b_ref[slot, pl.ds(s, 1), :]
                ks = k_ref[slot, pl.ds(s, 1), :]
                rs = jnp.sum(qg * jnp.exp2(bg - bs) * ks, axis=-1, keepdims=True)
                ag = jnp.where(col == s, rs, ag)
            groups.append(ag)
        return jnp.where(tri, jnp.concatenate(groups, axis=0), 0.0).astype(BF16)

    def one_iter(it, carry):
        streams = [(u, h) for u in range(chunks_per_iter) for h in range(HG_HEADS)]
        rows = [pl.ds(pl.multiple_of((it * chunks_per_iter + u) * c, c), c) for u in range(chunks_per_iter)]
        gates = [stage_gates(rows[u], h) for u, h in streams]
        for slot, (b, kk, _, _) in enumerate(gates):
            b_ref[slot] = b
            k_ref[slot] = kk
        cross = [stage_cross(slot, b, kk, qs) for slot, (b, kk, qs, _) in enumerate(gates)]
        scores = [stage_scores(slot, b, qs, cross[slot]) for slot, (b, _, qs, _) in enumerate(gates)]
        st = [st_ref[h] for h in range(HG_HEADS)]
        for slot, (u, h) in enumerate(streams):
            b, kk, qs, v = gates[slot]
            hs = slice(h * HG_KDIM, (h + 1) * HG_KDIM)
            b_last = b_ref[slot, pl.ds(c - 1, 1), :]
            o = _dot(scores[slot], v) + _dot_nt((qs * jnp.exp2(b)).astype(BF16), st[h].astype(BF16))
            kt = (kk * jnp.exp2(b_last - b)).astype(BF16)
            st[h] = st[h] * jnp.exp2(b_last) + _dot_tn(v, kt)
            o_ref[rows[u], hs] = (_rms(o) * _silu(g_ref[rows[u], hs])).astype(o_ref.dtype)
        for h in range(HG_HEADS):
            st_ref[h] = st[h]
        return carry

    lax.fori_loop(0, n_chunks // chunks_per_iter, one_iter, 0)


def _hgrn(hg, lb_param, layer, block, chunk, chunks_per_iter=1):
    bsz, s, _ = hg.shape
    slots = HG_HEADS * chunks_per_iter
    spec = lambda seg: pl.BlockSpec((None, block, HG_QK), lambda b, i, seg=seg: (b, i, seg))
    return pl.pallas_call(
        functools.partial(_hgrn_body, layer=layer, chunk=chunk, n_chunks=block // chunk,
                          chunks_per_iter=chunks_per_iter),
        grid=(bsz, s // block),
        in_specs=[spec(0), spec(1), spec(2), spec(3),
                  pl.BlockSpec((DEPTH, HG_QK), lambda b, i: (0, 0))],
        out_specs=pl.BlockSpec((None, block, HG_W), lambda b, i: (b, i, 0)),
        out_shape=jax.ShapeDtypeStruct((bsz, s, HG_W), BF16),
        scratch_shapes=[pltpu.VMEM((HG_HEADS, HG_VDIM, HG_KDIM), F32),
                        pltpu.VMEM((slots, chunk, LANES), F32),
                        pltpu.VMEM((slots, chunk, LANES), F32)],
        compiler_params=_params(("arbitrary", "arbitrary"), 32),
        name=f"hgrn2_l{layer}",
    )(hg, hg, hg, hg, lb_param)


def _ret_log_gamma(h):
    return math.log(1.0 - 2.0 ** (-5.0 - h))


def _ret_body(q_ref, k_ref, v_ref, g_ref, cos_ref, sin_ref, o_ref,
              st_ref, dm_ref, fs_ref, te_ref, bd_ref, *, chunk, n_chunks):
    c = chunk
    qk = RET_QK
    lane_head = lax.broadcasted_iota(jnp.int32, (c, qk), 1) // RET_KDIM

    def lane_log_gamma(head_idx):
        lg = jnp.full(head_idx.shape, _ret_log_gamma(0), F32)
        for h in range(1, RET_HEADS):
            lg = jnp.where(head_idx == h, _ret_log_gamma(h), lg)
        return lg

    @pl.when((pl.program_id(0) == 0) & (pl.program_id(1) == 0))
    def _():
        t = lax.broadcasted_iota(jnp.int32, (c, qk), 0).astype(F32)
        lg = lane_log_gamma(lane_head)
        fs_ref[...] = jnp.exp(lg * (t + 1.0))
        te_ref[...] = jnp.exp(lg * (c - 1.0 - t))
        dist = (lax.broadcasted_iota(jnp.int32, (c, c), 0) - lax.broadcasted_iota(jnp.int32, (c, c), 1))
        for h in range(RET_HEADS):
            dm_ref[h] = jnp.where(dist >= 0, jnp.exp(_ret_log_gamma(h) * jnp.maximum(dist, 0).astype(F32)), 0.0)
        rh = lax.broadcasted_iota(jnp.int32, (qk, RET_W), 0) // RET_KDIM
        ch = lax.broadcasted_iota(jnp.int32, (qk, RET_W), 1) // RET_VDIM
        bd_ref[...] = jnp.where(rh == ch, jnp.exp(lane_log_gamma(rh) * float(c)), 0.0)

    @pl.when(pl.program_id(1) == 0)
    def _():
        st_ref[...] = jnp.zeros_like(st_ref)

    in_first_half = (lax.broadcasted_iota(jnp.int32, (c, LANES), 1) % RET_KDIM) < (RET_KDIM // 2)

    def rope(t, cos, sin):
        parts = []
        for p in range(qk // LANES):
            th = t[:, p * LANES:(p + 1) * LANES]
            back = pltpu.roll(th, RET_KDIM // 2, axis=1)
            fwd = pltpu.roll(th, LANES - RET_KDIM // 2, axis=1)
            parts.append(jnp.where(in_first_half, fwd, back))
        return t * cos + jnp.concatenate(parts, axis=1) * sin

    def one_chunk(ci, carry):
        rows = pl.ds(pl.multiple_of(ci * c, c), c)
        cos = cos_ref[rows, :]
        sin = sin_ref[rows, :]
        qr = rope(q_ref[rows, :].astype(F32), cos, sin)
        kr = rope(k_ref[rows, :].astype(F32), cos, sin) * (RET_KDIM ** -0.5)
        kr_bf = kr.astype(BF16)
        v = v_ref[rows, :]
        st = st_ref[...]
        inter = _dot((qr * fs_ref[...]).astype(BF16), st.astype(BF16))
        g = g_ref[rows, :].astype(F32)
        for h in range(RET_HEADS):
            hs = slice(h * RET_VDIM, (h + 1) * RET_VDIM)
            qm = jnp.where(lane_head == h, qr, 0.0).astype(BF16)
            sc = _dot_nt(qm, kr_bf) * dm_ref[h]
            oh = _dot(sc.astype(BF16), v[:, hs]) + inter[:, hs]
            o_ref[rows, hs] = (_rms(oh) * _silu(g[:, hs])).astype(o_ref.dtype)
        upd = _dot_tn((kr * te_ref[...]).astype(BF16), v)
        bd = bd_ref[...]
        st_ref[...] = st * bd + jnp.where(bd > 0.0, upd, 0.0)
        return carry

    lax.fori_loop(0, n_chunks, one_chunk, 0)


def _retention(rt, cos_t, sin_t, block, chunk, name):
    bsz, s, _ = rt.shape
    return pl.pallas_call(
        functools.partial(_ret_body, chunk=chunk, n_chunks=block // chunk),
        grid=(bsz, s // block),
        in_specs=[
            pl.BlockSpec((None, block, RET_QK), lambda b, i: (b, i, 0)),
            pl.BlockSpec((None, block, RET_QK), lambda b, i: (b, i, 1)),
            pl.BlockSpec((None, block, RET_W), lambda b, i: (b, i, 1)),
            pl.BlockSpec((None, block, RET_W), lambda b, i: (b, i, 2)),
            pl.BlockSpec((block, RET_QK), lambda b, i: (i, 0)),
            pl.BlockSpec((block, RET_QK), lambda b, i: (i, 0)),
        ],
        out_specs=pl.BlockSpec((None, block, RET_W), lambda b, i: (b, i, 0)),
        out_shape=jax.ShapeDtypeStruct((bsz, s, RET_W), BF16),
        scratch_shapes=[
            pltpu.VMEM((RET_QK, RET_W), F32),
            pltpu.VMEM((RET_HEADS, chunk, chunk), F32),
            pltpu.VMEM((chunk, RET_QK), F32),
            pltpu.VMEM((chunk, RET_QK), F32),
            pltpu.VMEM((RET_QK, RET_W), F32),
        ],
        compiler_params=_params(("arbitrary", "arbitrary"), 32),
        name=name,
    )(rt, rt, rt, rt, cos_t, sin_t)


def _t5_bucket_np(distance):
    max_exact = N_BUCKETS // 2
    n = np.maximum(distance, 1).astype(np.float32)
    large = max_exact + (np.log(n / max_exact) / math.log(MAX_DISTANCE / max_exact)
                         * (N_BUCKETS - max_exact)).astype(np.int32)
    large = np.minimum(large, N_BUCKETS - 1)
    return np.where(distance < max_exact, distance, large)


def _bucket_table(dilation):
    a = np.arange(ATT_BLK)[:, None]
    kk = np.arange(2 * ATT_BLK)[None, :]
    j = a + ATT_BLK - kk
    valid = (j >= 0) & (j <= ATT_BLK)
    bucket = _t5_bucket_np(np.clip(j, 0, ATT_BLK) * dilation)
    return np.where(valid, bucket, -1).astype(np.int32)


def _in_attn_body(x_ref, g_ref, w_ref, *refs):
    outs, (h_ref, res_ref) = refs[:-2], refs[-2:]
    dilations = [d for _, d in DILATED_CONFIGS]
    j = pl.program_id(1)
    tm = x_ref.shape[0]

    @pl.when(j == 0)
    def _():
        h_ref[...] = (_rms(x_ref[...]) * g_ref[...]).astype(BF16)

    res = _dot(h_ref[...], w_ref[...])
    for seg in range(3):
        @pl.when(j == seg)
        def _(seg=seg):
            val = res * (ATT_HDIM ** -0.5 * LOG2E) if seg == 0 else res
            for cb in range(ATT_W // LANES):
                res_ref[cb] = val[:, cb * LANES:(cb + 1) * LANES]
            for di, d in enumerate(dilations):
                o_ref = outs[3 * di + seg]
                if d == 1:
                    o_ref[...] = val.astype(BF16)
                    continue
                for r in range(d):
                    for cb in range(ATT_W // LANES):
                        lanes = slice(r * ATT_W + cb * LANES, r * ATT_W + (cb + 1) * LANES)
                        o_ref[:, lanes] = res_ref.at[cb][pl.ds(r, tm // d, stride=d), :].astype(BF16)


def _in_attn(x, g, w, tm, name):
    t, dm = x.shape
    dilations = [d for _, d in DILATED_CONFIGS]
    out_shape, out_specs = [], []
    for d in dilations:
        for _ in range(3):
            out_shape.append(jax.ShapeDtypeStruct((t // d, d * ATT_W), BF16))
            out_specs.append(pl.BlockSpec((tm // d, d * ATT_W), lambda i, j: (i, 0)))
    outs = pl.pallas_call(
        _in_attn_body,
        grid=(t // tm, 3),
        in_specs=[
            pl.BlockSpec((tm, dm), lambda i, j: (i, 0)),
            pl.BlockSpec((1, dm), lambda i, j: (0, 0)),
            pl.BlockSpec((dm, ATT_W), lambda i, j: (0, j)),
        ],
        out_specs=out_specs,
        out_shape=out_shape,
        scratch_shapes=[pltpu.VMEM((tm, dm), BF16), pltpu.VMEM((ATT_W // LANES, tm, LANES), F32)],
        compiler_params=_params(("arbitrary", "arbitrary"), 56),
        name=name,
    )(x, g.reshape(1, dm), w)
    return [tuple(outs[3 * di:3 * di + 3]) for di in range(len(dilations))]


def _attn_body(rb_ref, idx_ref, q_ref, kc_ref, kp_ref, vc_ref, vp_ref, o_ref, st_ref, bias_ref):
    blk = ATT_BLK
    heads = range(ATT_HEADS)

    @pl.when((pl.program_id(0) == 0) & (pl.program_id(1) == 0) & (pl.program_id(2) == 0))
    def _():
        idx = idx_ref[...]
        in_prev = lax.broadcasted_iota(jnp.int32, idx.shape, 1) < blk
        for h in heads:
            acc = jnp.full(idx.shape, NEG, F32)
            for n in range(N_BUCKETS):
                acc = jnp.where(idx == n, rb_ref[n, h] * LOG2E, acc)
            bias_ref[0, h] = acc
            bias_ref[1, h] = jnp.where(in_prev, NEG, acc)

    sel = jnp.where(pl.program_id(2) == 0, 1, 0)
    hs = [slice(h * ATT_HDIM, (h + 1) * ATT_HDIM) for h in heads]
    logits = [(_dot_nt(q_ref[:, hs[h]], kp_ref[:, hs[h]]) + bias_ref[sel, h, :, 0:blk],
               _dot_nt(q_ref[:, hs[h]], kc_ref[:, hs[h]]) + bias_ref[sel, h, :, blk:2 * blk]) for h in heads]
    probs = []
    lane = lax.broadcasted_iota(jnp.int32, (blk, LANES), 1)
    stats = jnp.zeros((blk, LANES), F32)
    for h in heads:
        lp, lc = logits[h]
        m = jnp.max(jnp.maximum(lp, lc), axis=-1, keepdims=True)
        pp = jnp.exp2(lp - m)
        pc = jnp.exp2(lc - m)
        den = jnp.sum(pp + pc, axis=-1, keepdims=True)
        probs.append((pp.astype(BF16), pc.astype(BF16), 1.0 / den))
        stats = jnp.where(lane == h, m, stats)
        stats = jnp.where(lane == ATT_HEADS + h, den, stats)
    st_ref[...] = stats
    for h in heads:
        pp, pc, inv = probs[h]
        o = _dot(pp, vp_ref[:, hs[h]]) + _dot(pc, vc_ref[:, hs[h]])
        o_ref[:, hs[h]] = (o * inv).astype(o_ref.dtype)


def _attn_branch(q, k, v, rel_bias, bsz, dilation, name):
    rows = q.shape[0]
    l = rows // bsz
    blk = ATT_BLK
    nb = l // blk
    q, k, v = (a.reshape(bsz, l, dilation * ATT_W) for a in (q, k, v))
    idx = jnp.asarray(_bucket_table(dilation))
    cur = pl.BlockSpec((None, blk, ATT_W), lambda b, r, i: (b, i, r))
    prev = pl.BlockSpec((None, blk, ATT_W), lambda b, r, i: (b, jnp.maximum(i - 1, 0), r))
    o, st = pl.pallas_call(
        _attn_body,
        grid=(bsz, dilation, nb),
        in_specs=[
            pl.BlockSpec(memory_space=pltpu.SMEM),
            pl.BlockSpec((blk, 2 * blk), lambda b, r, i: (0, 0)),
            cur, cur, prev, cur, prev,
        ],
        out_specs=[
            pl.BlockSpec((None, blk, ATT_W), lambda b, r, i: (b, i, r)),
            pl.BlockSpec((None, blk, LANES), lambda b, r, i: (b, i, r)),
        ],
        out_shape=[
            jax.ShapeDtypeStruct((bsz, l, dilation * ATT_W), BF16),
            jax.ShapeDtypeStruct((bsz, l, dilation * LANES), F32),
        ],
        scratch_shapes=[pltpu.VMEM((2, ATT_HEADS, blk, 2 * blk), F32)],
        compiler_params=_params(("arbitrary", "arbitrary", "arbitrary"), 32),
        name=name,
    )(rel_bias, idx, q, k, k, v, v)
    return o.reshape(rows, dilation * ATT_W), st.reshape(rows, dilation * LANES)


def _merge_body(*refs):
    n = len(DILATED_CONFIGS)
    o_refs, s_refs, out_ref, scratch = refs[:n], refs[n:2 * n], refs[2 * n], list(refs[2 * n + 1:])
    tm = out_ref.shape[0]
    heads, stats = [], []
    for (_, d), o_ref, s_ref in zip(DILATED_CONFIGS, o_refs, s_refs):
        if d == 1:
            heads.append(lambda h, o_ref=o_ref: o_ref[:, h * ATT_HDIM:(h + 1) * ATT_HDIM].astype(F32))
            stats.append(s_ref[...])
            continue
        po_ref, ps_ref = scratch.pop(0), scratch.pop(0)
        for r in range(d):
            rows = pl.ds(r, tm // d, stride=d)
            for h in range(ATT_HEADS):
                lanes = slice(r * ATT_W + h * ATT_HDIM, r * ATT_W + (h + 1) * ATT_HDIM)
                po_ref.at[h][rows, :] = o_ref[:, lanes].astype(F32)
            ps_ref[rows, :] = s_ref[:, r * LANES:(r + 1) * LANES]
        heads.append(lambda h, po_ref=po_ref: po_ref[h])
        stats.append(ps_ref[...])
    mx = functools.reduce(jnp.maximum, stats)
    ws = [pltpu.roll(s, LANES - ATT_HEADS, axis=1) * jnp.exp2(s - mx) for s in stats]
    inv = 1.0 / functools.reduce(lambda a, b: a + b, ws)
    ws = [w * inv for w in ws]
    for h in range(ATT_HEADS):
        acc = ws[0][:, h:h + 1] * heads[0](h)
        for w, get in zip(ws[1:], heads[1:]):
            acc = acc + w[:, h:h + 1] * get(h)
        out_ref[:, h * ATT_HDIM:(h + 1) * ATT_HDIM] = acc.astype(out_ref.dtype)


def _merge(outs, stats, tm, name):
    t = outs[0].shape[0]
    in_specs, scratch = [], []
    for width in (ATT_W, LANES):
        for _, d in DILATED_CONFIGS:
            in_specs.append(pl.BlockSpec((tm // d, d * width), lambda i: (i, 0)))
    for _, d in DILATED_CONFIGS:
        if d > 1:
            scratch += [pltpu.VMEM((ATT_HEADS, tm, ATT_HDIM), F32), pltpu.VMEM((tm, LANES), F32)]
    return pl.pallas_call(
        _merge_body,
        grid=(t // tm,),
        in_specs=in_specs,
        out_specs=pl.BlockSpec((tm, ATT_W), lambda i: (i, 0)),
        out_shape=jax.ShapeDtypeStruct((t, ATT_W), BF16),
        scratch_shapes=scratch,
        compiler_params=_params(("arbitrary",), 32),
        name=name,
    )(*outs, *stats)


def _outproj_body(x_ref, a_ref, b_ref, c_ref, w_ref, o_ref):
    acc = x_ref[...]
    acc = acc + _dot(a_ref[...], w_ref[0:HG_W, :])
    acc = acc + _dot(b_ref[...], w_ref[HG_W:HG_W + RET_W, :])
    acc = acc + _dot(c_ref[...], w_ref[HG_W + RET_W:, :])
    o_ref[...] = acc


def _outproj(x, a, b, c, w, tm, name):
    t, d = x.shape
    row = lambda width: pl.BlockSpec((tm, width), lambda i: (i, 0))
    return pl.pallas_call(
        _outproj_body,
        grid=(t // tm,),
        in_specs=[row(d), row(HG_W), row(RET_W), row(ATT_W),
                  pl.BlockSpec(w.shape, lambda i: (0, 0))],
        out_specs=row(d),
        out_shape=jax.ShapeDtypeStruct((t, d), F32),
        compiler_params=_params(("arbitrary",), 48),
        name=name,
    )(x, a, b, c, w)


def _ffn1_body(x_ref, g_ref, wg_ref, wu_ref, cw_ref, cb_ref, o_ref, u_ref, carry_ref, *, tiles_per_seq):
    i = pl.program_id(0)
    j = pl.program_id(1)
    tm = x_ref.shape[0]

    @pl.when(j == 0)
    def _():
        u_ref[...] = (_rms(x_ref[...]) * g_ref[...]).astype(BF16)

    u = u_ref[...]
    gp = _dot(u, wg_ref[...])
    up = _dot(u, wu_ref[...])
    w0 = cw_ref[0:1, :]
    w1 = cw_ref[1:2, :]
    w2 = cw_ref[2:3, :]
    cb = cb_ref[...]

    gate = w2 * gp + w1 * pltpu.roll(gp, 1, axis=0) + w0 * pltpu.roll(gp, 2, axis=0) + cb
    o_ref[...] = (_silu(gate) * up).astype(o_ref.dtype)

    prev = jnp.where(i % tiles_per_seq == 0, 0.0, carry_ref[j])
    carry_ref[j] = gp[tm - SUBLANES:, :]
    top = gp[0:SUBLANES, :]
    r = lax.broadcasted_iota(jnp.int32, top.shape, 0)
    p1 = prev[SUBLANES - 1:SUBLANES, :]
    p2 = prev[SUBLANES - 2:SUBLANES - 1, :]
    t1 = jnp.where(r == 0, p1, pltpu.roll(top, 1, axis=0))
    t2 = jnp.where(r == 0, p2, jnp.where(r == 1, p1, pltpu.roll(top, 2, axis=0)))
    gate_top = w2 * top + w1 * t1 + w0 * t2 + cb
    o_ref[0:SUBLANES, :] = (_silu(gate_top) * up[0:SUBLANES, :]).astype(o_ref.dtype)


def _ffn1(x, g, wg, wu, cw, cb, seq_len, tm, tn, name):
    t, d = x.shape
    n = wg.shape[1]
    nj = n // tn
    return pl.pallas_call(
        functools.partial(_ffn1_body, tiles_per_seq=seq_len // tm),
        grid=(t // tm, nj),
        in_specs=[
            pl.BlockSpec((tm, d), lambda i, j: (i, 0)),
            pl.BlockSpec((1, d), lambda i, j: (0, 0)),
            pl.BlockSpec((d, tn), lambda i, j: (0, j)),
            pl.BlockSpec((d, tn), lambda i, j: (0, j)),
            pl.BlockSpec((cw.shape[0], tn), lambda i, j: (0, j)),
            pl.BlockSpec((1, tn), lambda i, j: (0, j)),
        ],
        out_specs=pl.BlockSpec((tm, tn), lambda i, j: (i, j)),
        out_shape=jax.ShapeDtypeStruct((t, n), BF16),
        scratch_shapes=[pltpu.VMEM((tm, d), BF16), pltpu.VMEM((nj, SUBLANES, tn), F32)],
        compiler_params=_params(("arbitrary", "arbitrary"), 56),
        name=name,
    )(x, g.reshape(1, d), wg, wu, cw, cb.reshape(1, n))


def _ffn2_body(x_ref, h_ref, w_ref, gf_ref, o_ref, acc_ref, *, final_norm):
    k = pl.program_id(1)

    @pl.when(k == 0)
    def _():
        acc_ref[...] = x_ref[...]

    acc_ref[...] += _dot(h_ref[...], w_ref[...])

    @pl.when(k == pl.num_programs(1) - 1)
    def _():
        y = acc_ref[...]
        if final_norm:
            y = _rms(y) * gf_ref[...]
        o_ref[...] = y


def _ffn2(x, h, w, g_final, final_norm, tm, tk, name):
    t, d = x.shape
    kdim = h.shape[1]
    return pl.pallas_call(
        functools.partial(_ffn2_body, final_norm=final_norm),
        grid=(t // tm, kdim // tk),
        in_specs=[
            pl.BlockSpec((tm, d), lambda i, k: (i, 0)),
            pl.BlockSpec((tm, tk), lambda i, k: (i, k)),
            pl.BlockSpec((tk, d), lambda i, k: (k, 0)),
            pl.BlockSpec((1, d), lambda i, k: (0, 0)),
        ],
        out_specs=pl.BlockSpec((tm, d), lambda i, k: (i, 0)),
        out_shape=jax.ShapeDtypeStruct((t, d), F32),
        scratch_shapes=[pltpu.VMEM((tm, d), F32)],
        compiler_params=_params(("arbitrary", "arbitrary"), 48),
        name=name,
    )(x, h, w, g_final.reshape(1, d))


def _rope_tables(s):
    inv_freq = ROPE_BASE ** (-jnp.arange(0, RET_KDIM, 2, dtype=F32) / RET_KDIM)
    ang = jnp.arange(s, dtype=F32)[:, None] * inv_freq[None, :]
    cos, sin = jnp.cos(ang), jnp.sin(ang)
    cos_t = jnp.tile(jnp.concatenate([cos, cos], axis=1), (1, RET_HEADS))
    sin_t = jnp.tile(jnp.concatenate([-sin, sin], axis=1), (1, RET_HEADS))
    return cos_t, sin_t


def kernel(x, norm_mix, w_in, hg_lower_bound, w_out, norm_ffn, w_gate, conv_w, conv_b,
           w_up, w_down, rel_bias, norm_final):
    bsz, s, d = x.shape
    t = bsz * s
    pad = D_FF_PAD - D_FF
    cos_t, sin_t = _rope_tables(s)
    x2 = x.reshape(t, d)
    for l in range(DEPTH):
        w_in_bf = w_in[l].astype(BF16)
        hg = _norm_matmul(x2, norm_mix[l], w_in_bf[:, :HG_COLS], F32, 1024, 512, f"in_hgrn_l{l}")
        rt = _norm_matmul(x2, norm_mix[l], w_in_bf[:, HG_COLS:HG_COLS + RET_COLS], BF16, 1024, 512, f"in_ret_l{l}")
        qkv = _in_attn(x2, norm_mix[l], w_in_bf[:, HG_COLS + RET_COLS:], 512, f"in_attn_l{l}")

        a = _hgrn(hg.reshape(bsz, s, HG_COLS), hg_lower_bound, l, block=512, chunk=32, chunks_per_iter=2)
        b = _retention(rt.reshape(bsz, s, RET_COLS), cos_t, sin_t, block=512, chunk=128, name=f"retention_l{l}")
        outs, stats = zip(*[_attn_branch(*qkv[di], rel_bias, bsz, dil, f"attn_d{dil}_l{l}")
                            for di, (_, dil) in enumerate(DILATED_CONFIGS)])
        c = _merge(outs, stats, 512, f"attn_merge_l{l}")

        x2 = _outproj(x2, a.reshape(t, HG_W), b.reshape(t, RET_W), c, w_out[l].astype(BF16), 512, f"out_proj_l{l}")

        wg = jnp.pad(w_gate[l], ((0, 0), (0, pad))).astype(BF16)
        wu = jnp.pad(w_up[l], ((0, 0), (0, pad))).astype(BF16)
        wd = jnp.pad(w_down[l], ((0, pad), (0, 0))).astype(BF16)
        cw = jnp.pad(conv_w[l], ((0, 0), (0, pad)))
        cb = jnp.pad(conv_b[l], ((0, pad),))
        hmid = _ffn1(x2, norm_ffn[l], wg, wu, cw, cb, s, 1024, 512, f"ffn_gate_up_l{l}")
        x2 = _ffn2(x2, hmid, wd, norm_final, l == DEPTH - 1, 512, 1408, f"ffn_down_l{l}")
    return x2.reshape(bsz, s, d)
```

```python
import functools
import math

import numpy as np
import jax
import jax.numpy as jnp
from jax import lax
from jax.experimental import pallas as pl
from jax.experimental.pallas import tpu as pltpu

F32 = jnp.float32
BF16 = jnp.bfloat16

D_MODEL = 2048
DEPTH = 2
HG_HEADS = 4
HG_KDIM = 128
HG_VDIM = 128
RET_HEADS = 4
RET_KDIM = 64
RET_VDIM = 128
ROPE_BASE = 10000.0
ATT_HEADS = 8
ATT_HDIM = 128
DILATED_CONFIGS = ((128, 1), (512, 4), (2048, 16))
N_BUCKETS = 32
MAX_DISTANCE = 2048
D_FF = 5504
EPS = 1e-6

HG_QK = HG_HEADS * HG_KDIM
HG_W = HG_HEADS * HG_VDIM
RET_QK = RET_HEADS * RET_KDIM
RET_W = RET_HEADS * RET_VDIM
ATT_W = ATT_HEADS * ATT_HDIM
HG_COLS = 2 * HG_QK + 2 * HG_W
RET_COLS = 2 * RET_QK + 2 * RET_W
ATT_COLS = 3 * ATT_W

LANES = 128
SUBLANES = 8
D_FF_PAD = 5632
ATT_BLK = 128
IN_ATTN_PIECE = 512
NEG = -1e30
LOG2E = math.log2(math.e)
MIB = 1024 * 1024


def _dot(a, b):
    return jnp.dot(a, b, preferred_element_type=F32)


def _dot_nt(a, b):
    return lax.dot_general(a, b, (((1,), (1,)), ((), ())), preferred_element_type=F32)


def _dot_tn(a, b):
    return lax.dot_general(a, b, (((0,), (0,)), ((), ())), preferred_element_type=F32)


def _params(semantics, vmem_mib):
    return pltpu.CompilerParams(dimension_semantics=semantics, vmem_limit_bytes=vmem_mib * MIB)


def _rms(x):
    return x * lax.rsqrt(jnp.mean(x * x, axis=-1, keepdims=True) + EPS)


def _silu(x):
    return x * jax.nn.sigmoid(x)


def _norm_body(x_ref, g_ref, o_ref):
    o_ref[...] = (_rms(x_ref[...]) * g_ref[...]).astype(o_ref.dtype)


def _norm(x, g, tm, name):
    t, d = x.shape
    return pl.pallas_call(
        _norm_body,
        grid=(t // tm,),
        in_specs=[pl.BlockSpec((tm, d), lambda i: (i, 0)), pl.BlockSpec((1, d), lambda i: (0, 0))],
        out_specs=pl.BlockSpec((tm, d), lambda i: (i, 0)),
        out_shape=jax.ShapeDtypeStruct((t, d), BF16),
        compiler_params=_params(("arbitrary",), 40),
        name=name,
    )(x, g.reshape(1, d))


def _matmul_body(h_ref, w_ref, o_ref):
    o_ref[...] = _dot(h_ref[...], w_ref[...]).astype(o_ref.dtype)


def _matmul(h, w, col0, n, out_dtype, tm, tn, name):
    t, d = h.shape
    j0 = col0 // tn
    return pl.pallas_call(
        _matmul_body,
        grid=(t // tm, n // tn),
        in_specs=[pl.BlockSpec((tm, d), lambda i, j: (i, 0)), pl.BlockSpec((d, tn), lambda i, j: (0, j + j0))],
        out_specs=pl.BlockSpec((tm, tn), lambda i, j: (i, j)),
        out_shape=jax.ShapeDtypeStruct((t, n), out_dtype),
        compiler_params=_params(("arbitrary", "arbitrary"), 48),
        name=name,
    )(h, w)


def _hgrn_body(q_ref, f_ref, i_ref, g_ref, lbp_ref, o_ref, st_ref, b_ref, k_ref, *,
               layer, chunk, n_chunks, chunks_per_iter):
    c = chunk
    n_groups = c // SUBLANES

    @pl.when(pl.program_id(1) == 0)
    def _():
        st_ref[...] = jnp.zeros_like(st_ref)

    p = lbp_ref[...]
    e = jnp.exp(p - jnp.max(p, axis=0, keepdims=True))
    sm = e / jnp.sum(e, axis=0, keepdims=True)
    lb_all = jnp.zeros((1, HG_QK), F32)
    for m in range(1, layer + 1):
        lb_all = lb_all + sm[m:m + 1, :]

    tri = lax.broadcasted_iota(jnp.int32, (c, c), 0) >= lax.broadcasted_iota(jnp.int32, (c, c), 1)
    tri_bf = jnp.where(tri, 1.0, 0.0).astype(BF16)
    col = lax.broadcasted_iota(jnp.int32, (SUBLANES, c), 1)

    def stage_gates(rows, h):
        hs = slice(h * HG_KDIM, (h + 1) * HG_KDIM)
        lb = lb_all[:, hs]
        oml = 1.0 - lb
        xf = f_ref[rows, hs]
        sig = jax.nn.sigmoid(xf)
        log2f = jnp.log2(lb + oml * sig)
        kk = oml * (1.0 - sig)
        hi = log2f.astype(BF16)
        r1 = log2f - hi.astype(F32)
        mid = r1.astype(BF16)
        lo = (r1 - mid.astype(F32)).astype(BF16)
        cs = _dot(tri_bf, jnp.concatenate([hi, mid, lo], axis=1))
        b = cs[:, 0:LANES] + cs[:, LANES:2 * LANES] + cs[:, 2 * LANES:3 * LANES]
        return b, kk, _silu(q_ref[rows, hs]), i_ref[rows, hs].astype(BF16)

    def stage_cross(slot, b, kk, qs):
        out = []
        for g in range(1, n_groups):
            r0 = g * SUBLANES
            ref = b_ref[slot, pl.ds(r0 - 1, 1), :]
            qt = (qs[r0:r0 + SUBLANES, :] * jnp.exp2(b[r0:r0 + SUBLANES, :] - ref)).astype(BF16)
            kt = jnp.concatenate([kk[0:r0, :] * jnp.exp2(ref - b[0:r0, :]), jnp.zeros((c - r0, LANES), F32)], axis=0)
            out.append(_dot_nt(qt, kt.astype(BF16)))
        return out

    def stage_scores(slot, b, qs, cross):
        groups = []
        for g in range(n_groups):
            r0 = g * SUBLANES
            qg = qs[r0:r0 + SUBLANES, :]
            bg = b[r0:r0 + SUBLANES, :]
            ag = cross[g - 1] if g else jnp.zeros((SUBLANES, c), F32)
            for s in range(r0, r0 + SUBLANES):
                bs = b_ref[slot, pl.ds(s, 1), :]
                ks = k_ref[slot, pl.ds(s, 1), :]
                rs = jnp.sum(qg * jnp.exp2(bg - bs) * ks, axis=-1, keepdims=True)
                ag = jnp.where(col == s, rs, ag)
            groups.append(ag)
        return jnp.where(tri, jnp.concatenate(groups, axis=0), 0.0).astype(BF16)

    def one_iter(it, carry):
        streams = [(u, h) for u in range(chunks_per_iter) for h in range(HG_HEADS)]
        rows = [pl.ds(pl.multiple_of((it * chunks_per_iter + u) * c, c), c) for u in range(chunks_per_iter)]
        gates = [stage_gates(rows[u], h) for u, h in streams]
        for slot, (b, kk, _, _) in enumerate(gates):
            b_ref[slot] = b
            k_ref[slot] = kk
        cross = [stage_cross(slot, b, kk, qs) for slot, (b, kk, qs, _) in enumerate(gates)]
        scores = [stage_scores(slot, b, qs, cross[slot]) for slot, (b, _, qs, _) in enumerate(gates)]
        st = [st_ref[h] for h in range(HG_HEADS)]
        for slot, (u, h) in enumerate(streams):
            b, kk, qs, v = gates[slot]
            hs = slice(h * HG_KDIM, (h + 1) * HG_KDIM)
            b_last = b_ref[slot, pl.ds(c - 1, 1), :]
            o = _dot(scores[slot], v) + _dot_nt((qs * jnp.exp2(b)).astype(BF16), st[h].astype(BF16))
            kt = (kk * jnp.exp2(b_last - b)).astype(BF16)
            st[h] = st[h] * jnp.exp2(b_last) + _dot_tn(v, kt)
            o_ref[rows[u], hs] = (_rms(o) * _silu(g_ref[rows[u], hs])).astype(o_ref.dtype)
        for h in range(HG_HEADS):
            st_ref[h] = st[h]
        return carry

    lax.fori_loop(0, n_chunks // chunks_per_iter, one_iter, 0)


def _hgrn(hg, lb_param, layer, block, chunk, chunks_per_iter=1):
    bsz, s, _ = hg.shape
    slots = HG_HEADS * chunks_per_iter
    spec = lambda seg: pl.BlockSpec((None, block, HG_QK), lambda b, i, seg=seg: (b, i, seg))
    return pl.pallas_call(
        functools.partial(_hgrn_body, layer=layer, chunk=chunk, n_chunks=block // chunk,
                          chunks_per_iter=chunks_per_iter),
        grid=(bsz, s // block),
        in_specs=[spec(0), spec(1), spec(2), spec(3),
                  pl.BlockSpec((DEPTH, HG_QK), lambda b, i: (0, 0))],
        out_specs=pl.BlockSpec((None, block, HG_W), lambda b, i: (b, i, 0)),
        out_shape=jax.ShapeDtypeStruct((bsz, s, HG_W), BF16),
        scratch_shapes=[pltpu.VMEM((HG_HEADS, HG_VDIM, HG_KDIM), F32),
                        pltpu.VMEM((slots, chunk, LANES), F32),
                        pltpu.VMEM((slots, chunk, LANES), F32)],
        compiler_params=_params(("arbitrary", "arbitrary"), 32),
        name=f"hgrn2_l{layer}",
    )(hg, hg, hg, hg, lb_param)


def _ret_log_gamma(h):
    return math.log(1.0 - 2.0 ** (-5.0 - h))


def _ret_body(q_ref, k_ref, v_ref, g_ref, cos_ref, sin_ref, o_ref,
              st_ref, dm_ref, fs_ref, te_ref, bd_ref, *, chunk, n_chunks):
    c = chunk
    qk = RET_QK
    lane_head = lax.broadcasted_iota(jnp.int32, (c, qk), 1) // RET_KDIM

    def lane_log_gamma(head_idx):
        lg = jnp.full(head_idx.shape, _ret_log_gamma(0), F32)
        for h in range(1, RET_HEADS):
            lg = jnp.where(head_idx == h, _ret_log_gamma(h), lg)
        return lg

    @pl.when((pl.program_id(0) == 0) & (pl.program_id(1) == 0))
    def _():
        t = lax.broadcasted_iota(jnp.int32, (c, qk), 0).astype(F32)
        lg = lane_log_gamma(lane_head)
        fs_ref[...] = jnp.exp(lg * (t + 1.0))
        te_ref[...] = jnp.exp(lg * (c - 1.0 - t))
        dist = (lax.broadcasted_iota(jnp.int32, (c, c), 0) - lax.broadcasted_iota(jnp.int32, (c, c), 1))
        for h in range(RET_HEADS):
            dm_ref[h] = jnp.where(dist >= 0, jnp.exp(_ret_log_gamma(h) * jnp.maximum(dist, 0).astype(F32)), 0.0)
        rh = lax.broadcasted_iota(jnp.int32, (qk, RET_W), 0) // RET_KDIM
        ch = lax.broadcasted_iota(jnp.int32, (qk, RET_W), 1) // RET_VDIM
        bd_ref[...] = jnp.where(rh == ch, jnp.exp(lane_log_gamma(rh) * float(c)), 0.0)

    @pl.when(pl.program_id(1) == 0)
    def _():
        st_ref[...] = jnp.zeros_like(st_ref)

    in_first_half = (lax.broadcasted_iota(jnp.int32, (c, LANES), 1) % RET_KDIM) < (RET_KDIM // 2)

    def rope(t, cos, sin):
        parts = []
        for p in range(qk // LANES):
            th = t[:, p * LANES:(p + 1) * LANES]
            back = pltpu.roll(th, RET_KDIM // 2, axis=1)
            fwd = pltpu.roll(th, LANES - RET_KDIM // 2, axis=1)
            parts.append(jnp.where(in_first_half, fwd, back))
        return t * cos + jnp.concatenate(parts, axis=1) * sin

    def one_chunk(ci, carry):
        rows = pl.ds(pl.multiple_of(ci * c, c), c)
        cos = cos_ref[rows, :]
        sin = sin_ref[rows, :]
        qr = rope(q_ref[rows, :].astype(F32), cos, sin)
        kr = rope(k_ref[rows, :].astype(F32), cos, sin) * (RET_KDIM ** -0.5)
        kr_bf = kr.astype(BF16)
        v = v_ref[rows, :]
        st = st_ref[...]
        inter = _dot((qr * fs_ref[...]).astype(BF16), st.astype(BF16))
        g = g_ref[rows, :].astype(F32)
        for h in range(RET_HEADS):
            hs = slice(h * RET_VDIM, (h + 1) * RET_VDIM)
            qm = jnp.where(lane_head == h, qr, 0.0).astype(BF16)
            sc = _dot_nt(qm, kr_bf) * dm_ref[h]
            oh = _dot(sc.astype(BF16), v[:, hs]) + inter[:, hs]
            o_ref[rows, hs] = (_rms(oh) * _silu(g[:, hs])).astype(o_ref.dtype)
        upd = _dot_tn((kr * te_ref[...]).astype(BF16), v)
        bd = bd_ref[...]
        st_ref[...] = st * bd + jnp.where(bd > 0.0, upd, 0.0)
        return carry

    lax.fori_loop(0, n_chunks, one_chunk, 0)


def _retention(rt, cos_t, sin_t, block, chunk, name):
    bsz, s, _ = rt.shape
    return pl.pallas_call(
        functools.partial(_ret_body, chunk=chunk, n_chunks=block // chunk),
        grid=(bsz, s // block),
        in_specs=[
            pl.BlockSpec((None, block, RET_QK), lambda b, i: (b, i, 0)),
            pl.BlockSpec((None, block, RET_QK), lambda b, i: (b, i, 1)),
            pl.BlockSpec((None, block, RET_W), lambda b, i: (b, i, 1)),
            pl.BlockSpec((None, block, RET_W), lambda b, i: (b, i, 2)),
            pl.BlockSpec((block, RET_QK), lambda b, i: (i, 0)),
            pl.BlockSpec((block, RET_QK), lambda b, i: (i, 0)),
        ],
        out_specs=pl.BlockSpec((None, block, RET_W), lambda b, i: (b, i, 0)),
        out_shape=jax.ShapeDtypeStruct((bsz, s, RET_W), BF16),
        scratch_shapes=[
            pltpu.VMEM((RET_QK, RET_W), F32),
            pltpu.VMEM((RET_HEADS, chunk, chunk), F32),
            pltpu.VMEM((chunk, RET_QK), F32),
            pltpu.VMEM((chunk, RET_QK), F32),
            pltpu.VMEM((RET_QK, RET_W), F32),
        ],
        compiler_params=_params(("arbitrary", "arbitrary"), 32),
        name=name,
    )(rt, rt, rt, rt, cos_t, sin_t)


def _t5_bucket_np(distance):
    max_exact = N_BUCKETS // 2
    n = np.maximum(distance, 1).astype(np.float32)
    large = max_exact + (np.log(n / max_exact) / math.log(MAX_DISTANCE / max_exact)
                         * (N_BUCKETS - max_exact)).astype(np.int32)
    large = np.minimum(large, N_BUCKETS - 1)
    return np.where(distance < max_exact, distance, large)


def _bucket_table(dilation):
    a = np.arange(ATT_BLK)[:, None]
    kk = np.arange(2 * ATT_BLK)[None, :]
    j = a + ATT_BLK - kk
    valid = (j >= 0) & (j <= ATT_BLK)
    bucket = _t5_bucket_np(np.clip(j, 0, ATT_BLK) * dilation)
    return np.where(valid, bucket, -1).astype(np.int32)


def _in_attn_body(h_ref, w_ref, *refs):
    outs, res_ref = refs[:-1], refs[-1]
    dilations = [d for _, d in DILATED_CONFIGS]
    tm = h_ref.shape[0]
    piece = IN_ATTN_PIECE
    blocks = piece // LANES
    h = h_ref[...]
    for p in range(ATT_COLS // piece):
        seg, off = divmod(p * piece, ATT_W)
        val = _dot(h, w_ref[:, p * piece:(p + 1) * piece])
        if seg == 0:
            val = val * (ATT_HDIM ** -0.5 * LOG2E)
        for cb in range(blocks):
            res_ref[p * blocks + cb] = val[:, cb * LANES:(cb + 1) * LANES]
        for di, d in enumerate(dilations):
            o_ref = outs[3 * di + seg]
            if d == 1:
                o_ref[:, off:off + piece] = val.astype(BF16)
                continue
            for r in range(d):
                for cb in range(blocks):
                    lo = r * ATT_W + off + cb * LANES
                    rows = pl.ds(r, tm // d, stride=d)
                    o_ref[:, lo:lo + LANES] = res_ref.at[p * blocks + cb][rows, :].astype(BF16)


def _in_attn(h, w, tm, name):
    t, dm = h.shape
    dilations = [d for _, d in DILATED_CONFIGS]
    out_shape, out_specs = [], []
    for d in dilations:
        for _ in range(3):
            out_shape.append(jax.ShapeDtypeStruct((t // d, d * ATT_W), BF16))
            out_specs.append(pl.BlockSpec((tm // d, d * ATT_W), lambda i: (i, 0)))
    outs = pl.pallas_call(
        _in_attn_body,
        grid=(t // tm,),
        in_specs=[
            pl.BlockSpec((tm, dm), lambda i: (i, 0)),
            pl.BlockSpec((dm, ATT_COLS), lambda i: (0, 0), pipeline_mode=pl.Buffered(1)),
        ],
        out_specs=out_specs,
        out_shape=out_shape,
        scratch_shapes=[pltpu.VMEM((ATT_COLS // LANES, tm, LANES), F32)],
        compiler_params=_params(("arbitrary",), 56),
        name=name,
    )(h, w)
    return [tuple(outs[3 * di:3 * di + 3]) for di in range(len(dilations))]


def _attn_body(rb_ref, idx_ref, q_ref, kc_ref, kp_ref, vc_ref, vp_ref, o_ref, st_ref, bias_ref, *, n_sub):
    blk = ATT_BLK
    heads = range(ATT_HEADS)

    @pl.when((pl.program_id(0) == 0) & (pl.program_id(1) == 0) & (pl.program_id(2) == 0))
    def _():
        idx = idx_ref[...]
        in_prev = lax.broadcasted_iota(jnp.int32, idx.shape, 1) < blk
        for h in heads:
            acc = jnp.full(idx.shape, NEG, F32)
            for n in range(N_BUCKETS):
                acc = jnp.where(idx == n, rb_ref[n, h] * LOG2E, acc)
            bias_ref[0, h] = acc
            bias_ref[1, h] = jnp.where(in_prev, NEG, acc)

    sel0 = jnp.where(pl.program_id(2) == 0, 1, 0)
    hs = [slice(h * ATT_HDIM, (h + 1) * ATT_HDIM) for h in heads]
    lane = lax.broadcasted_iota(jnp.int32, (blk, LANES), 1)

    def qk(t):
        rows = slice(t * blk, (t + 1) * blk)
        out = []
        for h in heads:
            q = q_ref[rows, hs[h]]
            if t == 0:
                out.append((_dot_nt(q, kp_ref[:, hs[h]]) + bias_ref[sel0, h, :, 0:blk],
                            _dot_nt(q, kc_ref[rows, hs[h]]) + bias_ref[sel0, h, :, blk:2 * blk]))
            else:
                l2 = _dot_nt(q, kc_ref[(t - 1) * blk:(t + 1) * blk, hs[h]]) + bias_ref[0, h]
                out.append((l2[:, 0:blk], l2[:, blk:2 * blk]))
        return out

    def softmax(t, logits):
        probs = []
        stats = jnp.zeros((blk, LANES), F32)
        for h in heads:
            lp, lc = logits[h]
            m = jnp.max(jnp.maximum(lp, lc), axis=-1, keepdims=True)
            pp = jnp.exp2(lp - m)
            pc = jnp.exp2(lc - m)
            den = jnp.sum(pp + pc, axis=-1, keepdims=True)
            probs.append((pp.astype(BF16), pc.astype(BF16), 1.0 / den))
            stats = jnp.where(lane == h, m, stats)
            stats = jnp.where(lane == ATT_HEADS + h, den, stats)
        st_ref[t * blk:(t + 1) * blk, :] = stats
        return probs

    def pv(t, probs):
        rows = slice(t * blk, (t + 1) * blk)
        for h in heads:
            pp, pc, inv = probs[h]
            vprev = vp_ref[:, hs[h]] if t == 0 else vc_ref[(t - 1) * blk:t * blk, hs[h]]
            o = _dot(pp, vprev) + _dot(pc, vc_ref[rows, hs[h]])
            o_ref[rows, hs[h]] = (o * inv).astype(o_ref.dtype)

    logits = qk(0)
    for t in range(n_sub):
        probs = softmax(t, logits)
        if t + 1 < n_sub:
            logits = qk(t + 1)
        pv(t, probs)


def _attn_branch(q, k, v, rel_bias, bsz, dilation, n_sub, name):
    rows = q.shape[0]
    l = rows // bsz
    blk = ATT_BLK
    lb = n_sub * blk
    nb = l // lb
    q, k, v = (a.reshape(bsz, l, dilation * ATT_W) for a in (q, k, v))
    idx = jnp.asarray(_bucket_table(dilation))
    cur = pl.BlockSpec((None, lb, ATT_W), lambda b, r, i: (b, i, r))
    prev = pl.BlockSpec((None, blk, ATT_W), lambda b, r, i: (b, jnp.maximum(i * n_sub - 1, 0), r))
    o, st = pl.pallas_call(
        functools.partial(_attn_body, n_sub=n_sub),
        grid=(bsz, dilation, nb),
        in_specs=[
            pl.BlockSpec(memory_space=pltpu.SMEM),
            pl.BlockSpec((blk, 2 * blk), lambda b, r, i: (0, 0)),
            cur, cur, prev, cur, prev,
        ],
        out_specs=[
            pl.BlockSpec((None, lb, ATT_W), lambda b, r, i: (b, i, r)),
            pl.BlockSpec((None, lb, LANES), lambda b, r, i: (b, i, r)),
        ],
        out_shape=[
            jax.ShapeDtypeStruct((bsz, l, dilation * ATT_W), BF16),
            jax.ShapeDtypeStruct((bsz, l, dilation * LANES), F32),
        ],
        scratch_shapes=[pltpu.VMEM((2, ATT_HEADS, blk, 2 * blk), F32)],
        compiler_params=_params(("arbitrary", "arbitrary", "arbitrary"), 32),
        name=name,
    )(rel_bias, idx, q, k, k, v, v)
    return o.reshape(rows, dilation * ATT_W), st.reshape(rows, dilation * LANES)


def _merge_body(*refs):
    n = len(DILATED_CONFIGS)
    o_refs, s_refs, out_ref, scratch = refs[:n], refs[n:2 * n], refs[2 * n], list(refs[2 * n + 1:])
    tm = out_ref.shape[0]
    heads, stats = [], []
    for (_, d), o_ref, s_ref in zip(DILATED_CONFIGS, o_refs, s_refs):
        if d == 1:
            heads.append(lambda h, o_ref=o_ref: o_ref[:, h * ATT_HDIM:(h + 1) * ATT_HDIM].astype(F32))
            stats.append(s_ref[...])
            continue
        po_ref, ps_ref = scratch.pop(0), scratch.pop(0)
        for r in range(d):
            rows = pl.ds(r, tm // d, stride=d)
            for h in range(ATT_HEADS):
                lanes = slice(r * ATT_W + h * ATT_HDIM, r * ATT_W + (h + 1) * ATT_HDIM)
                po_ref.at[h][rows, :] = o_ref[:, lanes].astype(F32)
            ps_ref[rows, :] = s_ref[:, r * LANES:(r + 1) * LANES]
        heads.append(lambda h, po_ref=po_ref: po_ref[h])
        stats.append(ps_ref[...])
    mx = functools.reduce(jnp.maximum, stats)
    ws = [pltpu.roll(s, LANES - ATT_HEADS, axis=1) * jnp.exp2(s - mx) for s in stats]
    inv = 1.0 / functools.reduce(lambda a, b: a + b, ws)
    ws = [w * inv for w in ws]
    for h in range(ATT_HEADS):
        acc = ws[0][:, h:h + 1] * heads[0](h)
        for w, get in zip(ws[1:], heads[1:]):
            acc = acc + w[:, h:h + 1] * get(h)
        out_ref[:, h * ATT_HDIM:(h + 1) * ATT_HDIM] = acc.astype(out_ref.dtype)


def _merge(outs, stats, tm, name):
    t = outs[0].shape[0]
    in_specs, scratch = [], []
    for width in (ATT_W, LANES):
        for _, d in DILATED_CONFIGS:
            in_specs.append(pl.BlockSpec((tm // d, d * width), lambda i: (i, 0)))
    for _, d in DILATED_CONFIGS:
        if d > 1:
            scratch += [pltpu.VMEM((ATT_HEADS, tm, ATT_HDIM), F32), pltpu.VMEM((tm, LANES), F32)]
    return pl.pallas_call(
        _merge_body,
        grid=(t // tm,),
        in_specs=in_specs,
        out_specs=pl.BlockSpec((tm, ATT_W), lambda i: (i, 0)),
        out_shape=jax.ShapeDtypeStruct((t, ATT_W), BF16),
        scratch_shapes=scratch,
        compiler_params=_params(("arbitrary",), 32),
        name=name,
    )(*outs, *stats)


def _outproj_body(x_ref, a_ref, b_ref, c_ref, w_ref, g_ref, o_ref, u_ref):
    acc = x_ref[...]
    acc = acc + _dot(a_ref[...], w_ref[0:HG_W, :])
    acc = acc + _dot(b_ref[...], w_ref[HG_W:HG_W + RET_W, :])
    acc = acc + _dot(c_ref[...], w_ref[HG_W + RET_W:, :])
    o_ref[...] = acc
    u_ref[...] = (_rms(acc) * g_ref[...]).astype(u_ref.dtype)


def _outproj(x, a, b, c, w, g, tm, name):
    t, d = x.shape
    row = lambda width: pl.BlockSpec((tm, width), lambda i: (i, 0))
    return pl.pallas_call(
        _outproj_body,
        grid=(t // tm,),
        in_specs=[row(d), row(HG_W), row(RET_W), row(ATT_W),
                  pl.BlockSpec(w.shape, lambda i: (0, 0), pipeline_mode=pl.Buffered(1)),
                  pl.BlockSpec((1, d), lambda i: (0, 0))],
        out_specs=[row(d), row(d)],
        out_shape=[jax.ShapeDtypeStruct((t, d), F32), jax.ShapeDtypeStruct((t, d), BF16)],
        compiler_params=_params(("arbitrary",), 48),
        name=name,
    )(x, a, b, c, w, g.reshape(1, d))


def _ffn1_body(u_ref, wg_ref, wu_ref, cw_ref, cb_ref, o_ref, carry_ref, *, tiles_per_seq):
    i = pl.program_id(0)
    j = pl.program_id(1)
    tm = u_ref.shape[0]
    u = u_ref[...]
    gp = _dot(u, wg_ref[...])
    up = _dot(u, wu_ref[...])
    w0 = cw_ref[0:1, :]
    w1 = cw_ref[1:2, :]
    w2 = cw_ref[2:3, :]
    cb = cb_ref[...]

    gate = w2 * gp + w1 * pltpu.roll(gp, 1, axis=0) + w0 * pltpu.roll(gp, 2, axis=0) + cb
    o_ref[...] = (_silu(gate) * up).astype(o_ref.dtype)

    prev = jnp.where(i % tiles_per_seq == 0, 0.0, carry_ref[j])
    carry_ref[j] = gp[tm - SUBLANES:, :]
    top = gp[0:SUBLANES, :]
    r = lax.broadcasted_iota(jnp.int32, top.shape, 0)
    p1 = prev[SUBLANES - 1:SUBLANES, :]
    p2 = prev[SUBLANES - 2:SUBLANES - 1, :]
    t1 = jnp.where(r == 0, p1, pltpu.roll(top, 1, axis=0))
    t2 = jnp.where(r == 0, p2, jnp.where(r == 1, p1, pltpu.roll(top, 2, axis=0)))
    gate_top = w2 * top + w1 * t1 + w0 * t2 + cb
    o_ref[0:SUBLANES, :] = (_silu(gate_top) * up[0:SUBLANES, :]).astype(o_ref.dtype)


def _ffn1(u, wg, wu, cw, cb, seq_len, tm, tn, name):
    t, d = u.shape
    n = wg.shape[1]
    nj = n // tn
    return pl.pallas_call(
        functools.partial(_ffn1_body, tiles_per_seq=seq_len // tm),
        grid=(t // tm, nj),
        in_specs=[
            pl.BlockSpec((tm, d), lambda i, j: (i, 0)),
            pl.BlockSpec((d, tn), lambda i, j: (0, j)),
            pl.BlockSpec((d, tn), lambda i, j: (0, j)),
            pl.BlockSpec((cw.shape[0], tn), lambda i, j: (0, j)),
            pl.BlockSpec((1, tn), lambda i, j: (0, j)),
        ],
        out_specs=pl.BlockSpec((tm, tn), lambda i, j: (i, j)),
        out_shape=jax.ShapeDtypeStruct((t, n), BF16),
        scratch_shapes=[pltpu.VMEM((nj, SUBLANES, tn), F32)],
        compiler_params=_params(("arbitrary", "arbitrary"), 48),
        name=name,
    )(u, wg, wu, cw, cb.reshape(1, n))


def _ffn2_body(x_ref, h_ref, w_ref, g_ref, *refs, last_layer):
    out_refs, acc_ref = refs[:-1], refs[-1]
    k = pl.program_id(1)

    @pl.when(k == 0)
    def _():
        acc_ref[...] = x_ref[...]

    acc_ref[...] += _dot(h_ref[...], w_ref[...])

    @pl.when(k == pl.num_programs(1) - 1)
    def _():
        y = acc_ref[...]
        normed = _rms(y) * g_ref[...]
        if last_layer:
            out_refs[0][...] = normed
        else:
            out_refs[0][...] = y
            out_refs[1][...] = normed.astype(out_refs[1].dtype)


def _ffn2(x, h, w, g, last_layer, tm, tk, name):
    t, d = x.shape
    kdim = h.shape[1]
    row = pl.BlockSpec((tm, d), lambda i, k: (i, 0))
    out_shape = [jax.ShapeDtypeStruct((t, d), F32)]
    if not last_layer:
        out_shape.append(jax.ShapeDtypeStruct((t, d), BF16))
    return pl.pallas_call(
        functools.partial(_ffn2_body, last_layer=last_layer),
        grid=(t // tm, kdim // tk),
        in_specs=[
            row,
            pl.BlockSpec((tm, tk), lambda i, k: (i, k)),
            pl.BlockSpec((tk, d), lambda i, k: (k, 0)),
            pl.BlockSpec((1, d), lambda i, k: (0, 0)),
        ],
        out_specs=[row] * len(out_shape),
        out_shape=out_shape,
        scratch_shapes=[pltpu.VMEM((tm, d), F32)],
        compiler_params=_params(("arbitrary", "arbitrary"), 48),
        name=name,
    )(x, h, w, g.reshape(1, d))


def _rope_tables(s):
    inv_freq = ROPE_BASE ** (-jnp.arange(0, RET_KDIM, 2, dtype=F32) / RET_KDIM)
    ang = jnp.arange(s, dtype=F32)[:, None] * inv_freq[None, :]
    cos, sin = jnp.cos(ang), jnp.sin(ang)
    cos_t = jnp.tile(jnp.concatenate([cos, cos], axis=1), (1, RET_HEADS))
    sin_t = jnp.tile(jnp.concatenate([-sin, sin], axis=1), (1, RET_HEADS))
    return cos_t, sin_t


def kernel(x, norm_mix, w_in, hg_lower_bound, w_out, norm_ffn, w_gate, conv_w, conv_b,
           w_up, w_down, rel_bias, norm_final):
    bsz, s, d = x.shape
    t = bsz * s
    pad = D_FF_PAD - D_FF
    cos_t, sin_t = _rope_tables(s)
    x2 = x.reshape(t, d)
    h = _norm(x2, norm_mix[0], 1024, "norm_in")
    for l in range(DEPTH):
        last = l == DEPTH - 1
        w_in_bf = w_in[l].astype(BF16)
        hg = _matmul(h, w_in_bf, 0, HG_COLS, F32, 1024, 512, f"in_hgrn_l{l}")
        rt = _matmul(h, w_in_bf, HG_COLS, RET_COLS, BF16, 1024, 512, f"in_ret_l{l}")
        qkv = _in_attn(h, w_in_bf[:, HG_COLS + RET_COLS:], 512, f"in_attn_l{l}")

        a = _hgrn(hg.reshape(bsz, s, HG_COLS), hg_lower_bound, l, block=512, chunk=32, chunks_per_iter=2)
        b = _retention(rt.reshape(bsz, s, RET_COLS), cos_t, sin_t, block=512, chunk=128, name=f"retention_l{l}")
        outs, stats = zip(*[_attn_branch(*qkv[di], rel_bias, bsz, dil, 4, f"attn_d{dil}_l{l}")
                            for di, (_, dil) in enumerate(DILATED_CONFIGS)])
        c = _merge(outs, stats, 512, f"attn_merge_l{l}")

        x2, u = _outproj(x2, a.reshape(t, HG_W), b.reshape(t, RET_W), c, w_out[l].astype(BF16), norm_ffn[l],
                         512, f"out_proj_l{l}")

        wg = jnp.pad(w_gate[l], ((0, 0), (0, pad))).astype(BF16)
        wu = jnp.pad(w_up[l], ((0, 0), (0, pad))).astype(BF16)
        wd = jnp.pad(w_down[l], ((0, pad), (0, 0))).astype(BF16)
        cw = jnp.pad(conv_w[l], ((0, 0), (0, pad)))
        cb = jnp.pad(conv_b[l], ((0, pad),))
        hmid = _ffn1(u, wg, wu, cw, cb, s, 1024, 512, f"ffn_gate_up_l{l}")
        res = _ffn2(x2, hmid, wd, norm_final if last else norm_mix[l + 1], last, 512, 1408, f"ffn_down_l{l}")
        if last:
            x2 = res[0]
        else:
            x2, h = res
    return x2.reshape(bsz, s, d)
```

```python
import functools
import math

import numpy as np
import jax
import jax.numpy as jnp
from jax import lax
from jax.experimental import pallas as pl
from jax.experimental.pallas import tpu as pltpu

F32 = jnp.float32
BF16 = jnp.bfloat16

D_MODEL = 2048
DEPTH = 2
HG_HEADS = 4
HG_KDIM = 128
HG_VDIM = 128
RET_HEADS = 4
RET_KDIM = 64
RET_VDIM = 128
ROPE_BASE = 10000.0
ATT_HEADS = 8
ATT_HDIM = 128
DILATED_CONFIGS = ((128, 1), (512, 4), (2048, 16))
N_BUCKETS = 32
MAX_DISTANCE = 2048
D_FF = 5504
EPS = 1e-6

HG_QK = HG_HEADS * HG_KDIM
HG_W = HG_HEADS * HG_VDIM
RET_QK = RET_HEADS * RET_KDIM
RET_W = RET_HEADS * RET_VDIM
ATT_W = ATT_HEADS * ATT_HDIM
HG_COLS = 2 * HG_QK + 2 * HG_W
RET_COLS = 2 * RET_QK + 2 * RET_W
ATT_COLS = 3 * ATT_W

LANES = 128
SUBLANES = 8
D_FF_PAD = 5632
ATT_BLK = 128
IN_ATTN_PIECE = 512
NEG = -1e30
LOG2E = math.log2(math.e)
MIB = 1024 * 1024


def _dot(a, b):
    return jnp.dot(a, b, preferred_element_type=F32)


def _dot_nt(a, b):
    return lax.dot_general(a, b, (((1,), (1,)), ((), ())), preferred_element_type=F32)


def _dot_tn(a, b):
    return lax.dot_general(a, b, (((0,), (0,)), ((), ())), preferred_element_type=F32)


def _params(semantics, vmem_mib):
    return pltpu.CompilerParams(dimension_semantics=semantics, vmem_limit_bytes=vmem_mib * MIB)


def _rms(x):
    return x * lax.rsqrt(jnp.mean(x * x, axis=-1, keepdims=True) + EPS)


def _silu(x):
    return x * jax.nn.sigmoid(x)


def _norm_body(x_ref, g_ref, o_ref):
    o_ref[...] = (_rms(x_ref[...]) * g_ref[...]).astype(o_ref.dtype)


def _norm(x, g, tm, name):
    t, d = x.shape
    return pl.pallas_call(
        _norm_body,
        grid=(t // tm,),
        in_specs=[pl.BlockSpec((tm, d), lambda i: (i, 0)), pl.BlockSpec((1, d), lambda i: (0, 0))],
        out_specs=pl.BlockSpec((tm, d), lambda i: (i, 0)),
        out_shape=jax.ShapeDtypeStruct((t, d), BF16),
        compiler_params=_params(("arbitrary",), 40),
        name=name,
    )(x, g.reshape(1, d))


def _matmul_body(h_ref, w_ref, o_ref):
    o_ref[...] = _dot(h_ref[...], w_ref[...]).astype(o_ref.dtype)


def _matmul(h, w, col0, n, out_dtype, tm, tn, name):
    t, d = h.shape
    j0 = col0 // tn
    return pl.pallas_call(
        _matmul_body,
        grid=(t // tm, n // tn),
        in_specs=[pl.BlockSpec((tm, d), lambda i, j: (i, 0)), pl.BlockSpec((d, tn), lambda i, j: (0, j + j0))],
        out_specs=pl.BlockSpec((tm, tn), lambda i, j: (i, j)),
        out_shape=jax.ShapeDtypeStruct((t, n), out_dtype),
        compiler_params=_params(("arbitrary", "arbitrary"), 48),
        name=name,
    )(h, w)


def _hgrn_body(q_ref, f_ref, i_ref, g_ref, lbp_ref, o_ref, st_ref, b_ref, k_ref, *,
               layer, chunk, n_chunks, chunks_per_iter):
    c = chunk
    n_groups = c // SUBLANES

    @pl.when(pl.program_id(1) == 0)
    def _():
        st_ref[...] = jnp.zeros_like(st_ref)

    p = lbp_ref[...]
    e = jnp.exp(p - jnp.max(p, axis=0, keepdims=True))
    sm = e / jnp.sum(e, axis=0, keepdims=True)
    lb_all = jnp.zeros((1, HG_QK), F32)
    for m in range(1, layer + 1):
        lb_all = lb_all + sm[m:m + 1, :]

    tri = lax.broadcasted_iota(jnp.int32, (c, c), 0) >= lax.broadcasted_iota(jnp.int32, (c, c), 1)
    tri_bf = jnp.where(tri, 1.0, 0.0).astype(BF16)
    col = lax.broadcasted_iota(jnp.int32, (SUBLANES, c), 1)

    def stage_gates(rows, h):
        hs = slice(h * HG_KDIM, (h + 1) * HG_KDIM)
        lb = lb_all[:, hs]
        oml = 1.0 - lb
        xf = f_ref[rows, hs]
        sig = jax.nn.sigmoid(xf)
        log2f = jnp.log2(lb + oml * sig)
        kk = oml * (1.0 - sig)
        hi = log2f.astype(BF16)
        r1 = log2f - hi.astype(F32)
        mid = r1.astype(BF16)
        lo = (r1 - mid.astype(F32)).astype(BF16)
        cs = _dot(tri_bf, jnp.concatenate([hi, mid, lo], axis=1))
        b = cs[:, 0:LANES] + cs[:, LANES:2 * LANES] + cs[:, 2 * LANES:3 * LANES]
        return b, kk, _silu(q_ref[rows, hs]), i_ref[rows, hs].astype(BF16)

    def stage_cross(slot, b, kk, qs):
        out = []
        for g in range(1, n_groups):
            r0 = g * SUBLANES
            ref = b_ref[slot, pl.ds(r0 - 1, 1), :]
            qt = (qs[r0:r0 + SUBLANES, :] * jnp.exp2(b[r0:r0 + SUBLANES, :] - ref)).astype(BF16)
            kt = jnp.concatenate([kk[0:r0, :] * jnp.exp2(ref - b[0:r0, :]), jnp.zeros((c - r0, LANES), F32)], axis=0)
            out.append(_dot_nt(qt, kt.astype(BF16)))
        return out

    def stage_scores(slot, b, qs, cross):
        groups = []
        for g in range(n_groups):
            r0 = g * SUBLANES
            qg = qs[r0:r0 + SUBLANES, :]
            bg = b[r0:r0 + SUBLANES, :]
            ag = cross[g - 1] if g else jnp.zeros((SUBLANES, c), F32)
            for s in range(r0, r0 + SUBLANES):
                bs = b_ref[slot, pl.ds(s, 1), :]
                ks = k_ref[slot, pl.ds(s, 1), :]
                rs = jnp.sum(qg * jnp.exp2(bg - bs) * ks, axis=-1, keepdims=True)
                ag = jnp.where(col == s, rs, ag)
            groups.append(ag)
        return jnp.where(tri, jnp.concatenate(groups, axis=0), 0.0).astype(BF16)

    def one_iter(it, carry):
        streams = [(u, h) for u in range(chunks_per_iter) for h in range(HG_HEADS)]
        rows = [pl.ds(pl.multiple_of((it * chunks_per_iter + u) * c, c), c) for u in range(chunks_per_iter)]
        gates = [stage_gates(rows[u], h) for u, h in streams]
        for slot, (b, kk, _, _) in enumerate(gates):
            b_ref[slot] = b
            k_ref[slot] = kk
        cross = [stage_cross(slot, b, kk, qs) for slot, (b, kk, qs, _) in enumerate(gates)]
        scores = [stage_scores(slot, b, qs, cross[slot]) for slot, (b, _, qs, _) in enumerate(gates)]
        st = [st_ref[h] for h in range(HG_HEADS)]
        for slot, (u, h) in enumerate(streams):
            b, kk, qs, v = gates[slot]
            hs = slice(h * HG_KDIM, (h + 1) * HG_KDIM)
            b_last = b_ref[slot, pl.ds(c - 1, 1), :]
            o = _dot(scores[slot], v) + _dot_nt((qs * jnp.exp2(b)).astype(BF16), st[h].astype(BF16))
            kt = (kk * jnp.exp2(b_last - b)).astype(BF16)
            st[h] = st[h] * jnp.exp2(b_last) + _dot_tn(v, kt)
            o_ref[rows[u], hs] = (_rms(o) * _silu(g_ref[rows[u], hs])).astype(o_ref.dtype)
        for h in range(HG_HEADS):
            st_ref[h] = st[h]
        return carry

    lax.fori_loop(0, n_chunks // chunks_per_iter, one_iter, 0)


def _hgrn(hg, lb_param, layer, block, chunk, chunks_per_iter=1):
    bsz, s, _ = hg.shape
    slots = HG_HEADS * chunks_per_iter
    spec = lambda seg: pl.BlockSpec((None, block, HG_QK), lambda b, i, seg=seg: (b, i, seg))
    return pl.pallas_call(
        functools.partial(_hgrn_body, layer=layer, chunk=chunk, n_chunks=block // chunk,
                          chunks_per_iter=chunks_per_iter),
        grid=(bsz, s // block),
        in_specs=[spec(0), spec(1), spec(2), spec(3),
                  pl.BlockSpec((DEPTH, HG_QK), lambda b, i: (0, 0))],
        out_specs=pl.BlockSpec((None, block, HG_W), lambda b, i: (b, i, 0)),
        out_shape=jax.ShapeDtypeStruct((bsz, s, HG_W), BF16),
        scratch_shapes=[pltpu.VMEM((HG_HEADS, HG_VDIM, HG_KDIM), F32),
                        pltpu.VMEM((slots, chunk, LANES), F32),
                        pltpu.VMEM((slots, chunk, LANES), F32)],
        compiler_params=_params(("arbitrary", "arbitrary"), 32),
        name=f"hgrn2_l{layer}",
    )(hg, hg, hg, hg, lb_param)


def _ret_log_gamma(h):
    return math.log(1.0 - 2.0 ** (-5.0 - h))


def _ret_body(q_ref, k_ref, v_ref, g_ref, cos_ref, sin_ref, o_ref,
              st_ref, dm_ref, fs_ref, te_ref, bd_ref, *, chunk, n_chunks, chunks_per_iter):
    c = chunk
    qk = RET_QK
    lane_head = lax.broadcasted_iota(jnp.int32, (c, qk), 1) // RET_KDIM

    def lane_log_gamma(head_idx):
        lg = jnp.full(head_idx.shape, _ret_log_gamma(0), F32)
        for h in range(1, RET_HEADS):
            lg = jnp.where(head_idx == h, _ret_log_gamma(h), lg)
        return lg

    @pl.when((pl.program_id(0) == 0) & (pl.program_id(1) == 0))
    def _():
        t = lax.broadcasted_iota(jnp.int32, (c, qk), 0).astype(F32)
        lg = lane_log_gamma(lane_head)
        fs_ref[...] = jnp.exp(lg * (t + 1.0))
        te_ref[...] = jnp.exp(lg * (c - 1.0 - t))
        dist = (lax.broadcasted_iota(jnp.int32, (c, c), 0) - lax.broadcasted_iota(jnp.int32, (c, c), 1))
        for h in range(RET_HEADS):
            dm_ref[h] = jnp.where(dist >= 0, jnp.exp(_ret_log_gamma(h) * jnp.maximum(dist, 0).astype(F32)), 0.0)
        rh = lax.broadcasted_iota(jnp.int32, (qk, RET_W), 0) // RET_KDIM
        ch = lax.broadcasted_iota(jnp.int32, (qk, RET_W), 1) // RET_VDIM
        bd_ref[...] = jnp.where(rh == ch, jnp.exp(lane_log_gamma(rh) * float(c)), 0.0)

    @pl.when(pl.program_id(1) == 0)
    def _():
        st_ref[...] = jnp.zeros_like(st_ref)

    in_first_half = (lax.broadcasted_iota(jnp.int32, (c, LANES), 1) % RET_KDIM) < (RET_KDIM // 2)

    def rope(t, cos, sin):
        parts = []
        for p in range(qk // LANES):
            th = t[:, p * LANES:(p + 1) * LANES]
            back = pltpu.roll(th, RET_KDIM // 2, axis=1)
            fwd = pltpu.roll(th, LANES - RET_KDIM // 2, axis=1)
            parts.append(jnp.where(in_first_half, fwd, back))
        return t * cos + jnp.concatenate(parts, axis=1) * sin

    heads = range(RET_HEADS)
    hs = [slice(h * RET_VDIM, (h + 1) * RET_VDIM) for h in heads]

    def one_iter(it, carry):
        units = range(chunks_per_iter)
        rows = [pl.ds(pl.multiple_of((it * chunks_per_iter + u) * c, c), c) for u in units]
        qr, kr, v = [], [], []
        for u in units:
            cos = cos_ref[rows[u], :]
            sin = sin_ref[rows[u], :]
            qr.append(rope(q_ref[rows[u], :].astype(F32), cos, sin))
            kr.append(rope(k_ref[rows[u], :].astype(F32), cos, sin) * (RET_KDIM ** -0.5))
            v.append(v_ref[rows[u], :])
        kr_bf = [k.astype(BF16) for k in kr]
        scores = [[_dot_nt(jnp.where(lane_head == h, qr[u], 0.0).astype(BF16), kr_bf[u]) for h in heads]
                  for u in units]
        upd = [_dot_tn((kr[u] * te_ref[...]).astype(BF16), v[u]) for u in units]
        bd = bd_ref[...]
        st = st_ref[...]
        inter = []
        for u in units:
            inter.append(_dot((qr[u] * fs_ref[...]).astype(BF16), st.astype(BF16)))
            st = st * bd + jnp.where(bd > 0.0, upd[u], 0.0)
        st_ref[...] = st
        for u in units:
            g = g_ref[rows[u], :].astype(F32)
            for h in heads:
                p = (scores[u][h] * dm_ref[h]).astype(BF16)
                oh = _dot(p, v[u][:, hs[h]]) + inter[u][:, hs[h]]
                o_ref[rows[u], hs[h]] = (_rms(oh) * _silu(g[:, hs[h]])).astype(o_ref.dtype)
        return carry

    lax.fori_loop(0, n_chunks // chunks_per_iter, one_iter, 0)


def _retention(rt, cos_t, sin_t, block, chunk, name, chunks_per_iter=2):
    bsz, s, _ = rt.shape
    return pl.pallas_call(
        functools.partial(_ret_body, chunk=chunk, n_chunks=block // chunk, chunks_per_iter=chunks_per_iter),
        grid=(bsz, s // block),
        in_specs=[
            pl.BlockSpec((None, block, RET_QK), lambda b, i: (b, i, 0)),
            pl.BlockSpec((None, block, RET_QK), lambda b, i: (b, i, 1)),
            pl.BlockSpec((None, block, RET_W), lambda b, i: (b, i, 1)),
            pl.BlockSpec((None, block, RET_W), lambda b, i: (b, i, 2)),
            pl.BlockSpec((block, RET_QK), lambda b, i: (i, 0)),
            pl.BlockSpec((block, RET_QK), lambda b, i: (i, 0)),
        ],
        out_specs=pl.BlockSpec((None, block, RET_W), lambda b, i: (b, i, 0)),
        out_shape=jax.ShapeDtypeStruct((bsz, s, RET_W), BF16),
        scratch_shapes=[
            pltpu.VMEM((RET_QK, RET_W), F32),
            pltpu.VMEM((RET_HEADS, chunk, chunk), F32),
            pltpu.VMEM((chunk, RET_QK), F32),
            pltpu.VMEM((chunk, RET_QK), F32),
            pltpu.VMEM((RET_QK, RET_W), F32),
        ],
        compiler_params=_params(("arbitrary", "arbitrary"), 32),
        name=name,
    )(rt, rt, rt, rt, cos_t, sin_t)


def _t5_bucket_np(distance):
    max_exact = N_BUCKETS // 2
    n = np.maximum(distance, 1).astype(np.float32)
    large = max_exact + (np.log(n / max_exact) / math.log(MAX_DISTANCE / max_exact)
                         * (N_BUCKETS - max_exact)).astype(np.int32)
    large = np.minimum(large, N_BUCKETS - 1)
    return np.where(distance < max_exact, distance, large)


def _bucket_table(dilation):
    a = np.arange(ATT_BLK)[:, None]
    kk = np.arange(2 * ATT_BLK)[None, :]
    j = a + ATT_BLK - kk
    valid = (j >= 0) & (j <= ATT_BLK)
    bucket = _t5_bucket_np(np.clip(j, 0, ATT_BLK) * dilation)
    return np.where(valid, bucket, -1).astype(np.int32)


def _in_attn_body(h_ref, *refs):
    n_pieces = ATT_COLS // IN_ATTN_PIECE
    w_refs, outs, res_ref = refs[:n_pieces], refs[n_pieces:-1], refs[-1]
    dilations = [d for _, d in DILATED_CONFIGS]
    tm = h_ref.shape[0]
    piece = IN_ATTN_PIECE
    blocks = piece // LANES
    h = h_ref[...]
    for p in range(ATT_COLS // piece):
        seg, off = divmod(p * piece, ATT_W)
        val = _dot(h, w_refs[p][...])
        if seg == 0:
            val = val * (ATT_HDIM ** -0.5 * LOG2E)
        for cb in range(blocks):
            res_ref[p * blocks + cb] = val[:, cb * LANES:(cb + 1) * LANES]
        for di, d in enumerate(dilations):
            o_ref = outs[3 * di + seg]
            if d == 1:
                o_ref[:, off:off + piece] = val.astype(BF16)
                continue
            for r in range(d):
                for cb in range(blocks):
                    lo = r * ATT_W + off + cb * LANES
                    rows = pl.ds(r, tm // d, stride=d)
                    o_ref[:, lo:lo + LANES] = res_ref.at[p * blocks + cb][rows, :].astype(BF16)


def _in_attn(h, w, col0, tm, name):
    t, dm = h.shape
    dilations = [d for _, d in DILATED_CONFIGS]
    piece = IN_ATTN_PIECE
    n_pieces = ATT_COLS // piece
    w_specs = [pl.BlockSpec((dm, piece), lambda i, p=p: (0, col0 // piece + p), pipeline_mode=pl.Buffered(1))
               for p in range(n_pieces)]
    out_shape, out_specs = [], []
    for d in dilations:
        for _ in range(3):
            out_shape.append(jax.ShapeDtypeStruct((t // d, d * ATT_W), BF16))
            out_specs.append(pl.BlockSpec((tm // d, d * ATT_W), lambda i: (i, 0)))
    outs = pl.pallas_call(
        _in_attn_body,
        grid=(t // tm,),
        in_specs=[pl.BlockSpec((tm, dm), lambda i: (i, 0))] + w_specs,
        out_specs=out_specs,
        out_shape=out_shape,
        scratch_shapes=[pltpu.VMEM((ATT_COLS // LANES, tm, LANES), F32)],
        compiler_params=_params(("arbitrary",), 56),
        name=name,
    )(h, *([w] * n_pieces))
    return [tuple(outs[3 * di:3 * di + 3]) for di in range(len(dilations))]


def _attn_body(rb_ref, idx_ref, q_ref, kc_ref, kp_ref, vc_ref, vp_ref, o_ref, st_ref, bias_ref, *, n_sub):
    blk = ATT_BLK
    heads = range(ATT_HEADS)

    @pl.when((pl.program_id(0) == 0) & (pl.program_id(1) == 0) & (pl.program_id(2) == 0))
    def _():
        idx = idx_ref[...]
        in_prev = lax.broadcasted_iota(jnp.int32, idx.shape, 1) < blk
        for h in heads:
            acc = jnp.full(idx.shape, NEG, F32)
            for n in range(N_BUCKETS):
                acc = jnp.where(idx == n, rb_ref[n, h] * LOG2E, acc)
            bias_ref[0, h] = acc
            bias_ref[1, h] = jnp.where(in_prev, NEG, acc)

    sel0 = jnp.where(pl.program_id(2) == 0, 1, 0)
    hs = [slice(h * ATT_HDIM, (h + 1) * ATT_HDIM) for h in heads]
    lane = lax.broadcasted_iota(jnp.int32, (blk, LANES), 1)

    def qk(t):
        rows = slice(t * blk, (t + 1) * blk)
        out = []
        for h in heads:
            q = q_ref[rows, hs[h]]
            if t == 0:
                out.append((_dot_nt(q, kp_ref[:, hs[h]]) + bias_ref[sel0, h, :, 0:blk],
                            _dot_nt(q, kc_ref[rows, hs[h]]) + bias_ref[sel0, h, :, blk:2 * blk]))
            else:
                l2 = _dot_nt(q, kc_ref[(t - 1) * blk:(t + 1) * blk, hs[h]]) + bias_ref[0, h]
                out.append((l2[:, 0:blk], l2[:, blk:2 * blk]))
        return out

    def softmax(t, logits):
        probs = []
        stats = jnp.zeros((blk, LANES), F32)
        for h in heads:
            lp, lc = logits[h]
            m = jnp.max(jnp.maximum(lp, lc), axis=-1, keepdims=True)
            pp = jnp.exp2(lp - m)
            pc = jnp.exp2(lc - m)
            den = jnp.sum(pp + pc, axis=-1, keepdims=True)
            probs.append((pp.astype(BF16), pc.astype(BF16), 1.0 / den))
            stats = jnp.where(lane == h, m, stats)
            stats = jnp.where(lane == ATT_HEADS + h, den, stats)
        st_ref[t * blk:(t + 1) * blk, :] = stats
        return probs

    def pv(t, probs):
        rows = slice(t * blk, (t + 1) * blk)
        for h in heads:
            pp, pc, inv = probs[h]
            vprev = vp_ref[:, hs[h]] if t == 0 else vc_ref[(t - 1) * blk:t * blk, hs[h]]
            o = _dot(pp, vprev) + _dot(pc, vc_ref[rows, hs[h]])
            o_ref[rows, hs[h]] = (o * inv).astype(o_ref.dtype)

    logits = qk(0)
    for t in range(n_sub):
        probs = softmax(t, logits)
        if t + 1 < n_sub:
            logits = qk(t + 1)
        pv(t, probs)


def _attn_branch(q, k, v, rel_bias, bsz, dilation, n_sub, name):
    rows = q.shape[0]
    l = rows // bsz
    blk = ATT_BLK
    lb = n_sub * blk
    nb = l // lb
    q, k, v = (a.reshape(bsz, l, dilation * ATT_W) for a in (q, k, v))
    idx = jnp.asarray(_bucket_table(dilation))
    cur = pl.BlockSpec((None, lb, ATT_W), lambda b, r, i: (b, i, r))
    prev = pl.BlockSpec((None, blk, ATT_W), lambda b, r, i: (b, jnp.maximum(i * n_sub - 1, 0), r))
    o, st = pl.pallas_call(
        functools.partial(_attn_body, n_sub=n_sub),
        grid=(bsz, dilation, nb),
        in_specs=[
            pl.BlockSpec(memory_space=pltpu.SMEM),
            pl.BlockSpec((blk, 2 * blk), lambda b, r, i: (0, 0)),
            cur, cur, prev, cur, prev,
        ],
        out_specs=[
            pl.BlockSpec((None, lb, ATT_W), lambda b, r, i: (b, i, r)),
            pl.BlockSpec((None, lb, LANES), lambda b, r, i: (b, i, r)),
        ],
        out_shape=[
            jax.ShapeDtypeStruct((bsz, l, dilation * ATT_W), BF16),
            jax.ShapeDtypeStruct((bsz, l, dilation * LANES), F32),
        ],
        scratch_shapes=[pltpu.VMEM((2, ATT_HEADS, blk, 2 * blk), F32)],
        compiler_params=_params(("arbitrary", "arbitrary", "arbitrary"), 32),
        name=name,
    )(rel_bias, idx, q, k, k, v, v)
    return o.reshape(rows, dilation * ATT_W), st.reshape(rows, dilation * LANES)


def _merge_heads(o_refs, s_refs, scratch, tm):
    scratch = list(scratch)
    heads, stats = [], []
    for (_, d), o_ref, s_ref in zip(DILATED_CONFIGS, o_refs, s_refs):
        if d == 1:
            heads.append(lambda h, o_ref=o_ref: o_ref[:, h * ATT_HDIM:(h + 1) * ATT_HDIM].astype(F32))
            stats.append(s_ref[...])
            continue
        po_ref, ps_ref = scratch.pop(0), scratch.pop(0)
        for r in range(d):
            rows = pl.ds(r, tm // d, stride=d)
            for h in range(ATT_HEADS):
                lanes = slice(r * ATT_W + h * ATT_HDIM, r * ATT_W + (h + 1) * ATT_HDIM)
                po_ref.at[h][rows, :] = o_ref[:, lanes].astype(F32)
            ps_ref[rows, :] = s_ref[:, r * LANES:(r + 1) * LANES]
        heads.append(lambda h, po_ref=po_ref: po_ref[h])
        stats.append(ps_ref[...])
    mx = functools.reduce(jnp.maximum, stats)
    ws = [pltpu.roll(s, LANES - ATT_HEADS, axis=1) * jnp.exp2(s - mx) for s in stats]
    inv = 1.0 / functools.reduce(lambda a, b: a + b, ws)
    ws = [w * inv for w in ws]

    def head(h):
        acc = ws[0][:, h:h + 1] * heads[0](h)
        for w, get in zip(ws[1:], heads[1:]):
            acc = acc + w[:, h:h + 1] * get(h)
        return acc

    return head


def _outproj_body(x_ref, a_ref, b_ref, *refs):
    n = len(DILATED_CONFIGS)
    o_refs, s_refs = refs[:n], refs[n:2 * n]
    w_ref, g_ref, o_ref, u_ref = refs[2 * n:2 * n + 4]
    tm = x_ref.shape[0]
    acc = x_ref[...]
    acc = acc + _dot(a_ref[...], w_ref[0:HG_W, :])
    acc = acc + _dot(b_ref[...], w_ref[HG_W:HG_W + RET_W, :])
    head = _merge_heads(o_refs, s_refs, refs[2 * n + 4:], tm)
    for h in range(0, ATT_HEADS, 2):
        c2 = jnp.concatenate([head(h), head(h + 1)], axis=1).astype(BF16)
        r0 = HG_W + RET_W + h * ATT_HDIM
        acc = acc + _dot(c2, w_ref[r0:r0 + 2 * ATT_HDIM, :])
    o_ref[...] = acc
    u_ref[...] = (_rms(acc) * g_ref[...]).astype(u_ref.dtype)


def _outproj(x, a, b, outs, stats, w, g, tm, name):
    t, d = x.shape
    row = lambda width: pl.BlockSpec((tm, width), lambda i: (i, 0))
    in_specs, scratch = [row(d), row(HG_W), row(RET_W)], []
    for width in (ATT_W, LANES):
        for _, dil in DILATED_CONFIGS:
            in_specs.append(pl.BlockSpec((tm // dil, dil * width), lambda i: (i, 0)))
    for _, dil in DILATED_CONFIGS:
        if dil > 1:
            scratch += [pltpu.VMEM((ATT_HEADS, tm, ATT_HDIM), F32), pltpu.VMEM((tm, LANES), F32)]
    in_specs += [pl.BlockSpec(w.shape, lambda i: (0, 0), pipeline_mode=pl.Buffered(1)),
                 pl.BlockSpec((1, d), lambda i: (0, 0))]
    return pl.pallas_call(
        _outproj_body,
        grid=(t // tm,),
        in_specs=in_specs,
        out_specs=[row(d), row(d)],
        out_shape=[jax.ShapeDtypeStruct((t, d), F32), jax.ShapeDtypeStruct((t, d), BF16)],
        scratch_shapes=scratch,
        compiler_params=_params(("arbitrary",), 56),
        name=name,
    )(x, a, b, *outs, *stats, w, g.reshape(1, d))


def _ffn1_body(u_ref, wg_ref, wu_ref, cw_ref, cb_ref, o_ref, carry_ref, *, tiles_per_seq):
    i = pl.program_id(0)
    j = pl.program_id(1)
    tm = u_ref.shape[0]
    u = u_ref[...]
    gp = _dot(u, wg_ref[...])
    up = _dot(u, wu_ref[...])
    w0 = cw_ref[0:1, :]
    w1 = cw_ref[1:2, :]
    w2 = cw_ref[2:3, :]
    cb = cb_ref[...]

    gate = w2 * gp + w1 * pltpu.roll(gp, 1, axis=0) + w0 * pltpu.roll(gp, 2, axis=0) + cb
    o_ref[...] = (_silu(gate) * up).astype(o_ref.dtype)

    prev = jnp.where(i % tiles_per_seq == 0, 0.0, carry_ref[j])
    carry_ref[j] = gp[tm - SUBLANES:, :]
    top = gp[0:SUBLANES, :]
    r = lax.broadcasted_iota(jnp.int32, top.shape, 0)
    p1 = prev[SUBLANES - 1:SUBLANES, :]
    p2 = prev[SUBLANES - 2:SUBLANES - 1, :]
    t1 = jnp.where(r == 0, p1, pltpu.roll(top, 1, axis=0))
    t2 = jnp.where(r == 0, p2, jnp.where(r == 1, p1, pltpu.roll(top, 2, axis=0)))
    gate_top = w2 * top + w1 * t1 + w0 * t2 + cb
    o_ref[0:SUBLANES, :] = (_silu(gate_top) * up[0:SUBLANES, :]).astype(o_ref.dtype)


def _ffn1(u, wg, wu, cw, cb, seq_len, tm, tn, name):
    t, d = u.shape
    n = wg.shape[1]
    nj = n // tn
    return pl.pallas_call(
        functools.partial(_ffn1_body, tiles_per_seq=seq_len // tm),
        grid=(t // tm, nj),
        in_specs=[
            pl.BlockSpec((tm, d), lambda i, j: (i, 0)),
            pl.BlockSpec((d, tn), lambda i, j: (0, j)),
            pl.BlockSpec((d, tn), lambda i, j: (0, j)),
            pl.BlockSpec((cw.shape[0], tn), lambda i, j: (0, j)),
            pl.BlockSpec((1, tn), lambda i, j: (0, j)),
        ],
        out_specs=pl.BlockSpec((tm, tn), lambda i, j: (i, j)),
        out_shape=jax.ShapeDtypeStruct((t, n), BF16),
        scratch_shapes=[pltpu.VMEM((nj, SUBLANES, tn), F32)],
        compiler_params=_params(("arbitrary", "arbitrary"), 48),
        name=name,
    )(u, wg, wu, cw, cb.reshape(1, n))


def _ffn2_body(x_ref, h_ref, w_ref, g_ref, *out_refs, last_layer):
    y_ref = out_refs[0]
    k = pl.program_id(1)
    part = _dot(h_ref[...], w_ref[...])

    @pl.when(k == 0)
    def _():
        y_ref[...] = x_ref[...] + part

    @pl.when(k > 0)
    def _():
        y_ref[...] += part

    @pl.when(k == pl.num_programs(1) - 1)
    def _():
        normed = _rms(y_ref[...]) * g_ref[...]
        if last_layer:
            y_ref[...] = normed
        else:
            out_refs[1][...] = normed.astype(out_refs[1].dtype)


def _ffn2(x, h, w, g, last_layer, tm, tk, name):
    t, d = x.shape
    kdim = h.shape[1]
    row = pl.BlockSpec((tm, d), lambda i, k: (i, 0))
    out_shape = [jax.ShapeDtypeStruct((t, d), F32)]
    if not last_layer:
        out_shape.append(jax.ShapeDtypeStruct((t, d), BF16))
    return pl.pallas_call(
        functools.partial(_ffn2_body, last_layer=last_layer),
        grid=(t // tm, kdim // tk),
        in_specs=[
            row,
            pl.BlockSpec((tm, tk), lambda i, k: (i, k)),
            pl.BlockSpec((tk, d), lambda i, k: (k, 0)),
            pl.BlockSpec((1, d), lambda i, k: (0, 0)),
        ],
        out_specs=[row] * len(out_shape),
        out_shape=out_shape,
        compiler_params=_params(("arbitrary", "arbitrary"), 56),
        name=name,
    )(x, h, w, g.reshape(1, d))


def _rope_tables(s):
    inv_freq = ROPE_BASE ** (-jnp.arange(0, RET_KDIM, 2, dtype=F32) / RET_KDIM)
    ang = jnp.arange(s, dtype=F32)[:, None] * inv_freq[None, :]
    cos, sin = jnp.cos(ang), jnp.sin(ang)
    cos_t = jnp.tile(jnp.concatenate([cos, cos], axis=1), (1, RET_HEADS))
    sin_t = jnp.tile(jnp.concatenate([-sin, sin], axis=1), (1, RET_HEADS))
    return cos_t, sin_t


def kernel(x, norm_mix, w_in, hg_lower_bound, w_out, norm_ffn, w_gate, conv_w, conv_b,
           w_up, w_down, rel_bias, norm_final):
    bsz, s, d = x.shape
    t = bsz * s
    pad = D_FF_PAD - D_FF
    cos_t, sin_t = _rope_tables(s)
    x2 = x.reshape(t, d)
    h = _norm(x2, norm_mix[0], 1024, "norm_in")
    for l in range(DEPTH):
        last = l == DEPTH - 1
        w_in_bf = w_in[l].astype(BF16)
        hg = _matmul(h, w_in_bf, 0, HG_COLS, F32, 1024, 1024, f"in_hgrn_l{l}")
        rt = _matmul(h, w_in_bf, HG_COLS, RET_COLS, BF16, 1024, 512, f"in_ret_l{l}")
        qkv = _in_attn(h, w_in_bf, HG_COLS + RET_COLS, 512, f"in_attn_l{l}")

        a = _hgrn(hg.reshape(bsz, s, HG_COLS), hg_lower_bound, l, block=512, chunk=32, chunks_per_iter=4)
        b = _retention(rt.reshape(bsz, s, RET_COLS), cos_t, sin_t, block=512, chunk=128, name=f"retention_l{l}")
        outs, stats = zip(*[_attn_branch(*qkv[di], rel_bias, bsz, dil, 4, f"attn_d{dil}_l{l}")
                            for di, (_, dil) in enumerate(DILATED_CONFIGS)])
        x2, u = _outproj(x2, a.reshape(t, HG_W), b.reshape(t, RET_W), outs, stats, w_out[l].astype(BF16),
                         norm_ffn[l], 512, f"out_proj_l{l}")

        wg = jnp.pad(w_gate[l].astype(BF16), ((0, 0), (0, pad)))
        wu = jnp.pad(w_up[l].astype(BF16), ((0, 0), (0, pad)))
        wd = jnp.pad(w_down[l].astype(BF16), ((0, pad), (0, 0)))
        cw = jnp.pad(conv_w[l], ((0, 0), (0, pad)))
        cb = jnp.pad(conv_b[l], ((0, pad),))
        hmid = _ffn1(u, wg, wu, cw, cb, s, 1024, 512, f"ffn_gate_up_l{l}")
        res = _ffn2(x2, hmid, wd, norm_final if last else norm_mix[l + 1], last, 512, 2816, f"ffn_down_l{l}")
        if last:
            x2 = res[0]
        else:
            x2, h = res
    return x2.reshape(bsz, s, d)
```

```python
import functools
import math

import numpy as np
import jax
import jax.numpy as jnp
from jax import lax
from jax.experimental import pallas as pl
from jax.experimental.pallas import tpu as pltpu

F32 = jnp.float32
BF16 = jnp.bfloat16

D_MODEL = 2048
DEPTH = 2
HG_HEADS = 4
HG_KDIM = 128
HG_VDIM = 128
RET_HEADS = 4
RET_KDIM = 64
RET_VDIM = 128
ROPE_BASE = 10000.0
ATT_HEADS = 8
ATT_HDIM = 128
DILATED_CONFIGS = ((128, 1), (512, 4), (2048, 16))
N_BUCKETS = 32
MAX_DISTANCE = 2048
D_FF = 5504
EPS = 1e-6

HG_QK = HG_HEADS * HG_KDIM
HG_W = HG_HEADS * HG_VDIM
RET_QK = RET_HEADS * RET_KDIM
RET_W = RET_HEADS * RET_VDIM
ATT_W = ATT_HEADS * ATT_HDIM
HG_COLS = 2 * HG_QK + 2 * HG_W
RET_COLS = 2 * RET_QK + 2 * RET_W
ATT_COLS = 3 * ATT_W

LANES = 128
SUBLANES = 8
D_FF_PAD = 5632
ATT_BLK = 128
IN_ATTN_PIECE = 512
NEG = -1e30
LOG2E = math.log2(math.e)
MIB = 1024 * 1024


def _dot(a, b):
    return jnp.dot(a, b, preferred_element_type=F32)


def _dot_nt(a, b):
    return lax.dot_general(a, b, (((1,), (1,)), ((), ())), preferred_element_type=F32)


def _dot_tn(a, b):
    return lax.dot_general(a, b, (((0,), (0,)), ((), ())), preferred_element_type=F32)


def _params(semantics, vmem_mib):
    return pltpu.CompilerParams(dimension_semantics=semantics, vmem_limit_bytes=vmem_mib * MIB)


def _rms(x):
    return x * lax.rsqrt(jnp.mean(x * x, axis=-1, keepdims=True) + EPS)


def _silu(x):
    return x * jax.nn.sigmoid(x)


def _norm_body(x_ref, g_ref, o_ref):
    o_ref[...] = (_rms(x_ref[...]) * g_ref[...]).astype(o_ref.dtype)


def _norm(x, g, tm, name):
    t, d = x.shape
    return pl.pallas_call(
        _norm_body,
        grid=(t // tm,),
        in_specs=[pl.BlockSpec((tm, d), lambda i: (i, 0)), pl.BlockSpec((1, d), lambda i: (0, 0))],
        out_specs=pl.BlockSpec((tm, d), lambda i: (i, 0)),
        out_shape=jax.ShapeDtypeStruct((t, d), BF16),
        compiler_params=_params(("arbitrary",), 40),
        name=name,
    )(x, g.reshape(1, d))


def _matmul_body(h_ref, w_ref, o_ref):
    o_ref[...] = _dot(h_ref[...], w_ref[...]).astype(o_ref.dtype)


def _matmul(h, w, layer, col0, n, out_dtype, tm, tn, name):
    t, d = h.shape
    j0 = col0 // tn
    return pl.pallas_call(
        _matmul_body,
        grid=(t // tm, n // tn),
        in_specs=[pl.BlockSpec((tm, d), lambda i, j: (i, 0)),
                  pl.BlockSpec((None, d, tn), lambda i, j: (layer, 0, j + j0))],
        out_specs=pl.BlockSpec((tm, tn), lambda i, j: (i, j)),
        out_shape=jax.ShapeDtypeStruct((t, n), out_dtype),
        compiler_params=_params(("arbitrary", "arbitrary"), 48),
        name=name,
    )(h, w)


def _hgrn_body(q_ref, f_ref, i_ref, g_ref, lbp_ref, o_ref, st_ref, b_ref, k_ref, *,
               layer, chunk, n_chunks, chunks_per_iter):
    c = chunk
    n_groups = c // SUBLANES

    @pl.when(pl.program_id(1) == 0)
    def _():
        st_ref[...] = jnp.zeros_like(st_ref)

    p = lbp_ref[...]
    e = jnp.exp(p - jnp.max(p, axis=0, keepdims=True))
    sm = e / jnp.sum(e, axis=0, keepdims=True)
    lb_all = jnp.zeros((1, HG_QK), F32)
    for m in range(1, layer + 1):
        lb_all = lb_all + sm[m:m + 1, :]

    tri = lax.broadcasted_iota(jnp.int32, (c, c), 0) >= lax.broadcasted_iota(jnp.int32, (c, c), 1)
    tri_bf = jnp.where(tri, 1.0, 0.0).astype(BF16)
    col = lax.broadcasted_iota(jnp.int32, (SUBLANES, c), 1)

    def stage_gates(rows, h):
        hs = slice(h * HG_KDIM, (h + 1) * HG_KDIM)
        lb = lb_all[:, hs]
        oml = 1.0 - lb
        xf = f_ref[rows, hs]
        sig = jax.nn.sigmoid(xf)
        log2f = jnp.log2(lb + oml * sig)
        kk = oml * (1.0 - sig)
        hi = log2f.astype(BF16)
        r1 = log2f - hi.astype(F32)
        mid = r1.astype(BF16)
        lo = (r1 - mid.astype(F32)).astype(BF16)
        cs = _dot(tri_bf, jnp.concatenate([hi, mid, lo], axis=1))
        b = cs[:, 0:LANES] + cs[:, LANES:2 * LANES] + cs[:, 2 * LANES:3 * LANES]
        return b, kk, _silu(q_ref[rows, hs]), i_ref[rows, hs].astype(BF16)

    def stage_cross(slot, b, kk, qs):
        out = []
        for g in range(1, n_groups):
            r0 = g * SUBLANES
            ref = b_ref[slot, pl.ds(r0 - 1, 1), :]
            qt = (qs[r0:r0 + SUBLANES, :] * jnp.exp2(b[r0:r0 + SUBLANES, :] - ref)).astype(BF16)
            kt = jnp.concatenate([kk[0:r0, :] * jnp.exp2(ref - b[0:r0, :]), jnp.zeros((c - r0, LANES), F32)], axis=0)
            out.append(_dot_nt(qt, kt.astype(BF16)))
        return out

    def stage_scores(slot, b, qs, cross):
        groups = []
        for g in range(n_groups):
            r0 = g * SUBLANES
            qg = qs[r0:r0 + SUBLANES, :]
            bg = b[r0:r0 + SUBLANES, :]
            ag = cross[g - 1] if g else jnp.zeros((SUBLANES, c), F32)
            for s in range(r0, r0 + SUBLANES):
                bs = b_ref[slot, pl.ds(s, 1), :]
                ks = k_ref[slot, pl.ds(s, 1), :]
                rs = jnp.sum(qg * jnp.exp2(bg - bs) * ks, axis=-1, keepdims=True)
                ag = jnp.where(col == s, rs, ag)
            groups.append(ag)
        return jnp.where(tri, jnp.concatenate(groups, axis=0), 0.0).astype(BF16)

    def one_iter(it, carry):
        streams = [(u, h) for u in range(chunks_per_iter) for h in range(HG_HEADS)]
        rows = [pl.ds(pl.multiple_of((it * chunks_per_iter + u) * c, c), c) for u in range(chunks_per_iter)]
        gates = [stage_gates(rows[u], h) for u, h in streams]
        for slot, (b, kk, _, _) in enumerate(gates):
            b_ref[slot] = b
            k_ref[slot] = kk
        cross = [stage_cross(slot, b, kk, qs) for slot, (b, kk, qs, _) in enumerate(gates)]
        scores = [stage_scores(slot, b, qs, cross[slot]) for slot, (b, _, qs, _) in enumerate(gates)]
        st = [st_ref[h] for h in range(HG_HEADS)]
        for slot, (u, h) in enumerate(streams):
            b, kk, qs, v = gates[slot]
            hs = slice(h * HG_KDIM, (h + 1) * HG_KDIM)
            b_last = b_ref[slot, pl.ds(c - 1, 1), :]
            o = _dot(scores[slot], v) + _dot_nt((qs * jnp.exp2(b)).astype(BF16), st[h].astype(BF16))
            kt = (kk * jnp.exp2(b_last - b)).astype(BF16)
            st[h] = st[h] * jnp.exp2(b_last) + _dot_tn(v, kt)
            o_ref[rows[u], hs] = (_rms(o) * _silu(g_ref[rows[u], hs])).astype(o_ref.dtype)
        for h in range(HG_HEADS):
            st_ref[h] = st[h]
        return carry

    lax.fori_loop(0, n_chunks // chunks_per_iter, one_iter, 0)


def _hgrn(hg, lb_param, layer, block, chunk, chunks_per_iter=1):
    bsz, s, _ = hg.shape
    slots = HG_HEADS * chunks_per_iter
    spec = lambda seg: pl.BlockSpec((None, block, HG_QK), lambda b, i, seg=seg: (b, i, seg))
    return pl.pallas_call(
        functools.partial(_hgrn_body, layer=layer, chunk=chunk, n_chunks=block // chunk,
                          chunks_per_iter=chunks_per_iter),
        grid=(bsz, s // block),
        in_specs=[spec(0), spec(1), spec(2), spec(3),
                  pl.BlockSpec((DEPTH, HG_QK), lambda b, i: (0, 0))],
        out_specs=pl.BlockSpec((None, block, HG_W), lambda b, i: (b, i, 0)),
        out_shape=jax.ShapeDtypeStruct((bsz, s, HG_W), BF16),
        scratch_shapes=[pltpu.VMEM((HG_HEADS, HG_VDIM, HG_KDIM), F32),
                        pltpu.VMEM((slots, chunk, LANES), F32),
                        pltpu.VMEM((slots, chunk, LANES), F32)],
        compiler_params=_params(("arbitrary", "arbitrary"), 32),
        name=f"hgrn2_l{layer}",
    )(hg, hg, hg, hg, lb_param)


def _ret_log_gamma(h):
    return math.log(1.0 - 2.0 ** (-5.0 - h))


def _ret_body(q_ref, k_ref, v_ref, g_ref, cos_ref, sin_ref, o_ref,
              st_ref, dm_ref, fs_ref, te_ref, bd_ref, *, chunk, n_chunks, chunks_per_iter):
    c = chunk
    qk = RET_QK
    lane_head = lax.broadcasted_iota(jnp.int32, (c, qk), 1) // RET_KDIM

    def lane_log_gamma(head_idx):
        lg = jnp.full(head_idx.shape, _ret_log_gamma(0), F32)
        for h in range(1, RET_HEADS):
            lg = jnp.where(head_idx == h, _ret_log_gamma(h), lg)
        return lg

    @pl.when((pl.program_id(0) == 0) & (pl.program_id(1) == 0))
    def _():
        t = lax.broadcasted_iota(jnp.int32, (c, qk), 0).astype(F32)
        lg = lane_log_gamma(lane_head)
        fs_ref[...] = jnp.exp(lg * (t + 1.0))
        te_ref[...] = jnp.exp(lg * (c - 1.0 - t))
        dist = (lax.broadcasted_iota(jnp.int32, (c, c), 0) - lax.broadcasted_iota(jnp.int32, (c, c), 1))
        for h in range(RET_HEADS):
            dm_ref[h] = jnp.where(dist >= 0, jnp.exp(_ret_log_gamma(h) * jnp.maximum(dist, 0).astype(F32)), 0.0)
        rh = lax.broadcasted_iota(jnp.int32, (qk, RET_W), 0) // RET_KDIM
        ch = lax.broadcasted_iota(jnp.int32, (qk, RET_W), 1) // RET_VDIM
        bd_ref[...] = jnp.where(rh == ch, jnp.exp(lane_log_gamma(rh) * float(c)), 0.0)

    @pl.when(pl.program_id(1) == 0)
    def _():
        st_ref[...] = jnp.zeros_like(st_ref)

    in_first_half = (lax.broadcasted_iota(jnp.int32, (c, LANES), 1) % RET_KDIM) < (RET_KDIM // 2)

    def rope(t, cos, sin):
        parts = []
        for p in range(qk // LANES):
            th = t[:, p * LANES:(p + 1) * LANES]
            back = pltpu.roll(th, RET_KDIM // 2, axis=1)
            fwd = pltpu.roll(th, LANES - RET_KDIM // 2, axis=1)
            parts.append(jnp.where(in_first_half, fwd, back))
        return t * cos + jnp.concatenate(parts, axis=1) * sin

    heads = range(RET_HEADS)
    hs = [slice(h * RET_VDIM, (h + 1) * RET_VDIM) for h in heads]

    def one_iter(it, carry):
        units = range(chunks_per_iter)
        rows = [pl.ds(pl.multiple_of((it * chunks_per_iter + u) * c, c), c) for u in units]
        qr, kr, v = [], [], []
        for u in units:
            cos = cos_ref[rows[u], :]
            sin = sin_ref[rows[u], :]
            qr.append(rope(q_ref[rows[u], :].astype(F32), cos, sin))
            kr.append(rope(k_ref[rows[u], :].astype(F32), cos, sin) * (RET_KDIM ** -0.5))
            v.append(v_ref[rows[u], :])
        kr_bf = [k.astype(BF16) for k in kr]
        scores = [[_dot_nt(jnp.where(lane_head == h, qr[u], 0.0).astype(BF16), kr_bf[u]) for h in heads]
                  for u in units]
        upd = [_dot_tn((kr[u] * te_ref[...]).astype(BF16), v[u]) for u in units]
        bd = bd_ref[...]
        st = st_ref[...]
        inter = []
        for u in units:
            inter.append(_dot((qr[u] * fs_ref[...]).astype(BF16), st.astype(BF16)))
            st = st * bd + jnp.where(bd > 0.0, upd[u], 0.0)
        st_ref[...] = st
        for u in units:
            g = g_ref[rows[u], :].astype(F32)
            for h in heads:
                p = (scores[u][h] * dm_ref[h]).astype(BF16)
                oh = _dot(p, v[u][:, hs[h]]) + inter[u][:, hs[h]]
                o_ref[rows[u], hs[h]] = (_rms(oh) * _silu(g[:, hs[h]])).astype(o_ref.dtype)
        return carry

    lax.fori_loop(0, n_chunks // chunks_per_iter, one_iter, 0)


def _retention(rt, cos_t, sin_t, block, chunk, name, chunks_per_iter=2):
    bsz, s, _ = rt.shape
    return pl.pallas_call(
        functools.partial(_ret_body, chunk=chunk, n_chunks=block // chunk, chunks_per_iter=chunks_per_iter),
        grid=(bsz, s // block),
        in_specs=[
            pl.BlockSpec((None, block, RET_QK), lambda b, i: (b, i, 0)),
            pl.BlockSpec((None, block, RET_QK), lambda b, i: (b, i, 1)),
            pl.BlockSpec((None, block, RET_W), lambda b, i: (b, i, 1)),
            pl.BlockSpec((None, block, RET_W), lambda b, i: (b, i, 2)),
            pl.BlockSpec((block, RET_QK), lambda b, i: (i, 0)),
            pl.BlockSpec((block, RET_QK), lambda b, i: (i, 0)),
        ],
        out_specs=pl.BlockSpec((None, block, RET_W), lambda b, i: (b, i, 0)),
        out_shape=jax.ShapeDtypeStruct((bsz, s, RET_W), BF16),
        scratch_shapes=[
            pltpu.VMEM((RET_QK, RET_W), F32),
            pltpu.VMEM((RET_HEADS, chunk, chunk), F32),
            pltpu.VMEM((chunk, RET_QK), F32),
            pltpu.VMEM((chunk, RET_QK), F32),
            pltpu.VMEM((RET_QK, RET_W), F32),
        ],
        compiler_params=_params(("arbitrary", "arbitrary"), 32),
        name=name,
    )(rt, rt, rt, rt, cos_t, sin_t)


def _t5_bucket_np(distance):
    max_exact = N_BUCKETS // 2
    n = np.maximum(distance, 1).astype(np.float32)
    large = max_exact + (np.log(n / max_exact) / math.log(MAX_DISTANCE / max_exact)
                         * (N_BUCKETS - max_exact)).astype(np.int32)
    large = np.minimum(large, N_BUCKETS - 1)
    return np.where(distance < max_exact, distance, large)


def _bucket_table(dilation):
    a = np.arange(ATT_BLK)[:, None]
    kk = np.arange(2 * ATT_BLK)[None, :]
    j = a + ATT_BLK - kk
    valid = (j >= 0) & (j <= ATT_BLK)
    bucket = _t5_bucket_np(np.clip(j, 0, ATT_BLK) * dilation)
    return np.where(valid, bucket, -1).astype(np.int32)


def _in_attn_body(h_ref, *refs):
    n_pieces = ATT_COLS // IN_ATTN_PIECE
    w_refs, outs, res_ref = refs[:n_pieces], refs[n_pieces:-1], refs[-1]
    dilations = [d for _, d in DILATED_CONFIGS]
    tm = h_ref.shape[0]
    piece = IN_ATTN_PIECE
    blocks = piece // LANES
    h = h_ref[...]
    for p in range(ATT_COLS // piece):
        seg, off = divmod(p * piece, ATT_W)
        val = _dot(h, w_refs[p][...])
        if seg == 0:
            val = val * (ATT_HDIM ** -0.5 * LOG2E)
        for cb in range(blocks):
            res_ref[p * blocks + cb] = val[:, cb * LANES:(cb + 1) * LANES]
        for di, d in enumerate(dilations):
            o_ref = outs[3 * di + seg]
            if d == 1:
                o_ref[:, off:off + piece] = val.astype(BF16)
                continue
            for r in range(d):
                for cb in range(blocks):
                    lo = r * ATT_W + off + cb * LANES
                    rows = pl.ds(r, tm // d, stride=d)
                    o_ref[:, lo:lo + LANES] = res_ref.at[p * blocks + cb][rows, :].astype(BF16)


def _in_attn(h, w, layer, col0, tm, name):
    t, dm = h.shape
    dilations = [d for _, d in DILATED_CONFIGS]
    piece = IN_ATTN_PIECE
    n_pieces = ATT_COLS // piece
    w_specs = [pl.BlockSpec((None, dm, piece), lambda i, p=p: (layer, 0, col0 // piece + p),
                            pipeline_mode=pl.Buffered(1)) for p in range(n_pieces)]
    out_shape, out_specs = [], []
    for d in dilations:
        for _ in range(3):
            out_shape.append(jax.ShapeDtypeStruct((t // d, d * ATT_W), BF16))
            out_specs.append(pl.BlockSpec((tm // d, d * ATT_W), lambda i: (i, 0)))
    outs = pl.pallas_call(
        _in_attn_body,
        grid=(t // tm,),
        in_specs=[pl.BlockSpec((tm, dm), lambda i: (i, 0))] + w_specs,
        out_specs=out_specs,
        out_shape=out_shape,
        scratch_shapes=[pltpu.VMEM((ATT_COLS // LANES, tm, LANES), F32)],
        compiler_params=_params(("arbitrary",), 56),
        name=name,
    )(h, *([w] * n_pieces))
    return [tuple(outs[3 * di:3 * di + 3]) for di in range(len(dilations))]


def _attn_body(rb_ref, idx_ref, q_ref, kc_ref, kp_ref, vc_ref, vp_ref, o_ref, st_ref, bias_ref, *, n_sub):
    blk = ATT_BLK
    heads = range(ATT_HEADS)

    @pl.when((pl.program_id(0) == 0) & (pl.program_id(1) == 0) & (pl.program_id(2) == 0))
    def _():
        idx = idx_ref[...]
        in_prev = lax.broadcasted_iota(jnp.int32, idx.shape, 1) < blk
        for h in heads:
            acc = jnp.full(idx.shape, NEG, F32)
            for n in range(N_BUCKETS):
                acc = jnp.where(idx == n, rb_ref[n, h] * LOG2E, acc)
            bias_ref[0, h] = acc
            bias_ref[1, h] = jnp.where(in_prev, NEG, acc)

    sel0 = jnp.where(pl.program_id(2) == 0, 1, 0)
    hs = [slice(h * ATT_HDIM, (h + 1) * ATT_HDIM) for h in heads]
    lane = lax.broadcasted_iota(jnp.int32, (blk, LANES), 1)

    def qk(t):
        rows = slice(t * blk, (t + 1) * blk)
        out = []
        for h in heads:
            q = q_ref[rows, hs[h]]
            if t == 0:
                out.append((_dot_nt(q, kp_ref[:, hs[h]]) + bias_ref[sel0, h, :, 0:blk],
                            _dot_nt(q, kc_ref[rows, hs[h]]) + bias_ref[sel0, h, :, blk:2 * blk]))
            else:
                l2 = _dot_nt(q, kc_ref[(t - 1) * blk:(t + 1) * blk, hs[h]]) + bias_ref[0, h]
                out.append((l2[:, 0:blk], l2[:, blk:2 * blk]))
        return out

    def softmax(t, logits):
        probs = []
        stats = jnp.zeros((blk, LANES), F32)
        for h in heads:
            lp, lc = logits[h]
            m = jnp.max(jnp.maximum(lp, lc), axis=-1, keepdims=True)
            pp = jnp.exp2(lp - m)
            pc = jnp.exp2(lc - m)
            den = jnp.sum(pp + pc, axis=-1, keepdims=True)
            probs.append((pp.astype(BF16), pc.astype(BF16), 1.0 / den))
            stats = jnp.where(lane == h, m, stats)
            stats = jnp.where(lane == ATT_HEADS + h, den, stats)
        st_ref[t * blk:(t + 1) * blk, :] = stats
        return probs

    def pv(t, probs):
        rows = slice(t * blk, (t + 1) * blk)
        for h in heads:
            pp, pc, inv = probs[h]
            vprev = vp_ref[:, hs[h]] if t == 0 else vc_ref[(t - 1) * blk:t * blk, hs[h]]
            o = _dot(pp, vprev) + _dot(pc, vc_ref[rows, hs[h]])
            o_ref[rows, hs[h]] = (o * inv).astype(o_ref.dtype)

    logits = qk(0)
    for t in range(n_sub):
        probs = softmax(t, logits)
        if t + 1 < n_sub:
            logits = qk(t + 1)
        pv(t, probs)


def _attn_branch(q, k, v, rel_bias, bsz, dilation, n_sub, name):
    rows = q.shape[0]
    l = rows // bsz
    blk = ATT_BLK
    lb = n_sub * blk
    nb = l // lb
    q, k, v = (a.reshape(bsz, l, dilation * ATT_W) for a in (q, k, v))
    idx = jnp.asarray(_bucket_table(dilation))
    cur = pl.BlockSpec((None, lb, ATT_W), lambda b, r, i: (b, i, r))
    prev = pl.BlockSpec((None, blk, ATT_W), lambda b, r, i: (b, jnp.maximum(i * n_sub - 1, 0), r))
    o, st = pl.pallas_call(
        functools.partial(_attn_body, n_sub=n_sub),
        grid=(bsz, dilation, nb),
        in_specs=[
            pl.BlockSpec(memory_space=pltpu.SMEM),
            pl.BlockSpec((blk, 2 * blk), lambda b, r, i: (0, 0)),
            cur, cur, prev, cur, prev,
        ],
        out_specs=[
            pl.BlockSpec((None, lb, ATT_W), lambda b, r, i: (b, i, r)),
            pl.BlockSpec((None, lb, LANES), lambda b, r, i: (b, i, r)),
        ],
        out_shape=[
            jax.ShapeDtypeStruct((bsz, l, dilation * ATT_W), BF16),
            jax.ShapeDtypeStruct((bsz, l, dilation * LANES), F32),
        ],
        scratch_shapes=[pltpu.VMEM((2, ATT_HEADS, blk, 2 * blk), F32)],
        compiler_params=_params(("arbitrary", "arbitrary", "arbitrary"), 32),
        name=name,
    )(rel_bias, idx, q, k, k, v, v)
    return o.reshape(rows, dilation * ATT_W), st.reshape(rows, dilation * LANES)


def _merge_heads(o_refs, s_refs, scratch, tm):
    scratch = list(scratch)
    heads, stats = [], []
    for (_, d), o_ref, s_ref in zip(DILATED_CONFIGS, o_refs, s_refs):
        if d == 1:
            heads.append(lambda h, o_ref=o_ref: o_ref[:, h * ATT_HDIM:(h + 1) * ATT_HDIM].astype(F32))
            stats.append(s_ref[...])
            continue
        po_ref, ps_ref = scratch.pop(0), scratch.pop(0)
        for r in range(d):
            rows = pl.ds(r, tm // d, stride=d)
            for h in range(ATT_HEADS):
                lanes = slice(r * ATT_W + h * ATT_HDIM, r * ATT_W + (h + 1) * ATT_HDIM)
                po_ref.at[h][rows, :] = o_ref[:, lanes].astype(F32)
            ps_ref[rows, :] = s_ref[:, r * LANES:(r + 1) * LANES]
        heads.append(lambda h, po_ref=po_ref: po_ref[h])
        stats.append(ps_ref[...])
    mx = functools.reduce(jnp.maximum, stats)
    ws = [pltpu.roll(s, LANES - ATT_HEADS, axis=1) * jnp.exp2(s - mx) for s in stats]
    inv = 1.0 / functools.reduce(lambda a, b: a + b, ws)
    ws = [w * inv for w in ws]

    def head(h):
        acc = ws[0][:, h:h + 1] * heads[0](h)
        for w, get in zip(ws[1:], heads[1:]):
            acc = acc + w[:, h:h + 1] * get(h)
        return acc

    return head


def _outproj_body(x_ref, a_ref, b_ref, *refs):
    n = len(DILATED_CONFIGS)
    o_refs, s_refs = refs[:n], refs[n:2 * n]
    w_ref, g_ref, o_ref, u_ref = refs[2 * n:2 * n + 4]
    tm = x_ref.shape[0]
    acc = x_ref[...]
    acc = acc + _dot(a_ref[...], w_ref[0:HG_W, :])
    acc = acc + _dot(b_ref[...], w_ref[HG_W:HG_W + RET_W, :])
    head = _merge_heads(o_refs, s_refs, refs[2 * n + 4:], tm)
    for h in range(0, ATT_HEADS, 2):
        c2 = jnp.concatenate([head(h), head(h + 1)], axis=1).astype(BF16)
        r0 = HG_W + RET_W + h * ATT_HDIM
        acc = acc + _dot(c2, w_ref[r0:r0 + 2 * ATT_HDIM, :])
    o_ref[...] = acc
    u_ref[...] = (_rms(acc) * g_ref[...]).astype(u_ref.dtype)


def _outproj(x, a, b, outs, stats, w, layer, g, tm, name):
    t, d = x.shape
    row = lambda width: pl.BlockSpec((tm, width), lambda i: (i, 0))
    in_specs, scratch = [row(d), row(HG_W), row(RET_W)], []
    for width in (ATT_W, LANES):
        for _, dil in DILATED_CONFIGS:
            in_specs.append(pl.BlockSpec((tm // dil, dil * width), lambda i: (i, 0)))
    for _, dil in DILATED_CONFIGS:
        if dil > 1:
            scratch += [pltpu.VMEM((ATT_HEADS, tm, ATT_HDIM), F32), pltpu.VMEM((tm, LANES), F32)]
    in_specs += [pl.BlockSpec((None,) + w.shape[1:], lambda i: (layer, 0, 0), pipeline_mode=pl.Buffered(1)),
                 pl.BlockSpec((1, d), lambda i: (0, 0))]
    return pl.pallas_call(
        _outproj_body,
        grid=(t // tm,),
        in_specs=in_specs,
        out_specs=[row(d), row(d)],
        out_shape=[jax.ShapeDtypeStruct((t, d), F32), jax.ShapeDtypeStruct((t, d), BF16)],
        scratch_shapes=scratch,
        compiler_params=_params(("arbitrary",), 56),
        name=name,
    )(x, a, b, *outs, *stats, w, g.reshape(1, d))


def _ffn1_body(u_ref, wg_ref, wu_ref, cw_ref, cb_ref, o_ref, carry_ref, *, tiles_per_seq):
    i = pl.program_id(0)
    j = pl.program_id(1)
    tm = u_ref.shape[0]
    u = u_ref[...]
    gp = _dot(u, wg_ref[...])
    up = _dot(u, wu_ref[...])
    w0 = cw_ref[0:1, :]
    w1 = cw_ref[1:2, :]
    w2 = cw_ref[2:3, :]
    cb = cb_ref[...]

    gate = w2 * gp + w1 * pltpu.roll(gp, 1, axis=0) + w0 * pltpu.roll(gp, 2, axis=0) + cb
    o_ref[...] = (_silu(gate) * up).astype(o_ref.dtype)

    prev = jnp.where(i % tiles_per_seq == 0, 0.0, carry_ref[j])
    carry_ref[j] = gp[tm - SUBLANES:, :]
    top = gp[0:SUBLANES, :]
    r = lax.broadcasted_iota(jnp.int32, top.shape, 0)
    p1 = prev[SUBLANES - 1:SUBLANES, :]
    p2 = prev[SUBLANES - 2:SUBLANES - 1, :]
    t1 = jnp.where(r == 0, p1, pltpu.roll(top, 1, axis=0))
    t2 = jnp.where(r == 0, p2, jnp.where(r == 1, p1, pltpu.roll(top, 2, axis=0)))
    gate_top = w2 * top + w1 * t1 + w0 * t2 + cb
    o_ref[0:SUBLANES, :] = (_silu(gate_top) * up[0:SUBLANES, :]).astype(o_ref.dtype)


def _ffn1(u, wg, wu, cw, cb, layer, seq_len, tm, tn, name):
    t, d = u.shape
    n = wg.shape[2]
    nj = n // tn
    wspec = lambda rows: pl.BlockSpec((None, rows, tn), lambda i, j: (layer, 0, j))
    return pl.pallas_call(
        functools.partial(_ffn1_body, tiles_per_seq=seq_len // tm),
        grid=(t // tm, nj),
        in_specs=[pl.BlockSpec((tm, d), lambda i, j: (i, 0)), wspec(d), wspec(d), wspec(cw.shape[1]), wspec(1)],
        out_specs=pl.BlockSpec((tm, tn), lambda i, j: (i, j)),
        out_shape=jax.ShapeDtypeStruct((t, n), BF16),
        scratch_shapes=[pltpu.VMEM((nj, SUBLANES, tn), F32)],
        compiler_params=_params(("arbitrary", "arbitrary"), 48),
        name=name,
    )(u, wg, wu, cw, cb)


def _ffn2_body(x_ref, h_ref, w_ref, g_ref, *out_refs, last_layer):
    y = x_ref[...] + _dot(h_ref[...], w_ref[...])
    normed = _rms(y) * g_ref[...]
    if last_layer:
        out_refs[0][...] = normed
    else:
        out_refs[0][...] = y
        out_refs[1][...] = normed.astype(out_refs[1].dtype)


def _ffn2(x, h, w, layer, g, last_layer, tm, name):
    t, d = x.shape
    kdim = h.shape[1]
    row = pl.BlockSpec((tm, d), lambda i: (i, 0))
    out_shape = [jax.ShapeDtypeStruct((t, d), F32)]
    if not last_layer:
        out_shape.append(jax.ShapeDtypeStruct((t, d), BF16))
    return pl.pallas_call(
        functools.partial(_ffn2_body, last_layer=last_layer),
        grid=(t // tm,),
        in_specs=[
            row,
            pl.BlockSpec((tm, kdim), lambda i: (i, 0)),
            pl.BlockSpec((None, kdim, d), lambda i: (layer, 0, 0), pipeline_mode=pl.Buffered(1)),
            pl.BlockSpec((1, d), lambda i: (0, 0)),
        ],
        out_specs=[row] * len(out_shape),
        out_shape=out_shape,
        compiler_params=_params(("arbitrary",), 60),
        name=name,
    )(x, h, w, g.reshape(1, d))


def _rope_tables(s):
    inv_freq = ROPE_BASE ** (-jnp.arange(0, RET_KDIM, 2, dtype=F32) / RET_KDIM)
    ang = jnp.arange(s, dtype=F32)[:, None] * inv_freq[None, :]
    cos, sin = jnp.cos(ang), jnp.sin(ang)
    cos_t = jnp.tile(jnp.concatenate([cos, cos], axis=1), (1, RET_HEADS))
    sin_t = jnp.tile(jnp.concatenate([-sin, sin], axis=1), (1, RET_HEADS))
    return cos_t, sin_t


def kernel(x, norm_mix, w_in, hg_lower_bound, w_out, norm_ffn, w_gate, conv_w, conv_b,
           w_up, w_down, rel_bias, norm_final):
    bsz, s, d = x.shape
    t = bsz * s
    pad = D_FF_PAD - D_FF
    cos_t, sin_t = _rope_tables(s)
    x2 = x.reshape(t, d)
    w_in_bf = w_in.astype(BF16)
    w_out_bf = w_out.astype(BF16)
    wg = jnp.pad(w_gate.astype(BF16), ((0, 0), (0, 0), (0, pad)))
    wu = jnp.pad(w_up.astype(BF16), ((0, 0), (0, 0), (0, pad)))
    wd = jnp.pad(w_down.astype(BF16), ((0, 0), (0, pad), (0, 0)))
    cw = jnp.pad(conv_w, ((0, 0), (0, 0), (0, pad)))
    cb = jnp.pad(conv_b, ((0, 0), (0, pad))).reshape(DEPTH, 1, D_FF_PAD)
    h = _norm(x2, norm_mix[0], 1024, "norm_in")
    for l in range(DEPTH):
        last = l == DEPTH - 1
        hg = _matmul(h, w_in_bf, l, 0, HG_COLS, F32, 1024, 1024, f"in_hgrn_l{l}")
        rt = _matmul(h, w_in_bf, l, HG_COLS, RET_COLS, BF16, 1024, 512, f"in_ret_l{l}")
        qkv = _in_attn(h, w_in_bf, l, HG_COLS + RET_COLS, 512, f"in_attn_l{l}")

        a = _hgrn(hg.reshape(bsz, s, HG_COLS), hg_lower_bound, l, block=512, chunk=32, chunks_per_iter=4)
        b = _retention(rt.reshape(bsz, s, RET_COLS), cos_t, sin_t, block=512, chunk=128, name=f"retention_l{l}")
        outs, stats = zip(*[_attn_branch(*qkv[di], rel_bias, bsz, dil, 4, f"attn_d{dil}_l{l}")
                            for di, (_, dil) in enumerate(DILATED_CONFIGS)])
        x2, u = _outproj(x2, a.reshape(t, HG_W), b.reshape(t, RET_W), outs, stats, w_out_bf, l,
                         norm_ffn[l], 512, f"out_proj_l{l}")
        hmid = _ffn1(u, wg, wu, cw, cb, l, s, 1024, 512, f"ffn_gate_up_l{l}")
        res = _ffn2(x2, hmid, wd, l, norm_final if last else norm_mix[l + 1], last, 512, f"ffn_down_l{l}")
        if last:
            x2 = res[0]
        else:
            x2, h = res
    return x2.reshape(bsz, s, d)
```

```python
import functools
import math

import numpy as np
import jax
import jax.numpy as jnp
from jax import lax
from jax.experimental import pallas as pl
from jax.experimental.pallas import tpu as pltpu

F32 = jnp.float32
BF16 = jnp.bfloat16

D_MODEL = 2048
DEPTH = 2
HG_HEADS = 4
HG_KDIM = 128
HG_VDIM = 128
RET_HEADS = 4
RET_KDIM = 64
RET_VDIM = 128
ROPE_BASE = 10000.0
ATT_HEADS = 8
ATT_HDIM = 128
DILATED_CONFIGS = ((128, 1), (512, 4), (2048, 16))
N_BUCKETS = 32
MAX_DISTANCE = 2048
D_FF = 5504
EPS = 1e-6

HG_QK = HG_HEADS * HG_KDIM
HG_W = HG_HEADS * HG_VDIM
RET_QK = RET_HEADS * RET_KDIM
RET_W = RET_HEADS * RET_VDIM
ATT_W = ATT_HEADS * ATT_HDIM
HG_COLS = 2 * HG_QK + 2 * HG_W
RET_COLS = 2 * RET_QK + 2 * RET_W
ATT_COLS = 3 * ATT_W

LANES = 128
SUBLANES = 8
D_FF_PAD = 5632
ATT_BLK = 128
IN_ATTN_PIECE = 512
MERGE_STRIDE = 4
NEG = -1e30
LOG2E = math.log2(math.e)
MIB = 1024 * 1024


def _dot(a, b):
    return jnp.dot(a, b, preferred_element_type=F32)


def _dot_nt(a, b):
    return lax.dot_general(a, b, (((1,), (1,)), ((), ())), preferred_element_type=F32)


def _dot_tn(a, b):
    return lax.dot_general(a, b, (((0,), (0,)), ((), ())), preferred_element_type=F32)


def _params(semantics, vmem_mib):
    return pltpu.CompilerParams(dimension_semantics=semantics, vmem_limit_bytes=vmem_mib * MIB)


def _rms(x):
    return x * lax.rsqrt(jnp.mean(x * x, axis=-1, keepdims=True) + EPS)


def _silu(x):
    return x * jax.nn.sigmoid(x)


def _norm_body(x_ref, g_ref, o_ref):
    o_ref[...] = (_rms(x_ref[...]) * g_ref[...]).astype(o_ref.dtype)


def _norm(x, g, tm, name):
    t, d = x.shape
    return pl.pallas_call(
        _norm_body,
        grid=(t // tm,),
        in_specs=[pl.BlockSpec((tm, d), lambda i: (i, 0)), pl.BlockSpec((1, d), lambda i: (0, 0))],
        out_specs=pl.BlockSpec((tm, d), lambda i: (i, 0)),
        out_shape=jax.ShapeDtypeStruct((t, d), BF16),
        compiler_params=_params(("arbitrary",), 40),
        name=name,
    )(x, g.reshape(1, d))


def _matmul_body(h_ref, w_ref, o_ref):
    o_ref[...] = _dot(h_ref[...], w_ref[...]).astype(o_ref.dtype)


def _matmul(h, w, layer, col0, n, out_dtype, tm, tn, name):
    t, d = h.shape
    j0 = col0 // tn
    return pl.pallas_call(
        _matmul_body,
        grid=(t // tm, n // tn),
        in_specs=[pl.BlockSpec((tm, d), lambda i, j: (i, 0)),
                  pl.BlockSpec((None, d, tn), lambda i, j: (layer, 0, j + j0))],
        out_specs=pl.BlockSpec((tm, tn), lambda i, j: (i, j)),
        out_shape=jax.ShapeDtypeStruct((t, n), out_dtype),
        compiler_params=_params(("arbitrary", "arbitrary"), 48),
        name=name,
    )(h, w)


def _hgrn_body(q_ref, f_ref, i_ref, g_ref, lbp_ref, o_ref, st_ref, b_ref, k_ref, *,
               layer, chunk, n_chunks, chunks_per_iter):
    c = chunk
    n_groups = c // SUBLANES

    @pl.when(pl.program_id(1) == 0)
    def _():
        st_ref[...] = jnp.zeros_like(st_ref)

    p = lbp_ref[...]
    e = jnp.exp(p - jnp.max(p, axis=0, keepdims=True))
    sm = e / jnp.sum(e, axis=0, keepdims=True)
    lb_all = jnp.zeros((1, HG_QK), F32)
    for m in range(1, layer + 1):
        lb_all = lb_all + sm[m:m + 1, :]

    tri = lax.broadcasted_iota(jnp.int32, (c, c), 0) >= lax.broadcasted_iota(jnp.int32, (c, c), 1)
    tri_bf = jnp.where(tri, 1.0, 0.0).astype(BF16)
    col = lax.broadcasted_iota(jnp.int32, (SUBLANES, c), 1)

    def stage_gates(rows, h):
        hs = slice(h * HG_KDIM, (h + 1) * HG_KDIM)
        lb = lb_all[:, hs]
        oml = 1.0 - lb
        xf = f_ref[rows, hs]
        sig = jax.nn.sigmoid(xf)
        log2f = jnp.log2(lb + oml * sig)
        kk = oml * (1.0 - sig)
        hi = log2f.astype(BF16)
        r1 = log2f - hi.astype(F32)
        mid = r1.astype(BF16)
        lo = (r1 - mid.astype(F32)).astype(BF16)
        cs = _dot(tri_bf, jnp.concatenate([hi, mid, lo], axis=1))
        b = cs[:, 0:LANES] + cs[:, LANES:2 * LANES] + cs[:, 2 * LANES:3 * LANES]
        return b, kk, _silu(q_ref[rows, hs]), i_ref[rows, hs].astype(BF16)

    def stage_cross(slot, b, kk, qs):
        out = []
        for g in range(1, n_groups):
            r0 = g * SUBLANES
            ref = b_ref[slot, pl.ds(r0 - 1, 1), :]
            qt = (qs[r0:r0 + SUBLANES, :] * jnp.exp2(b[r0:r0 + SUBLANES, :] - ref)).astype(BF16)
            kt = jnp.concatenate([kk[0:r0, :] * jnp.exp2(ref - b[0:r0, :]), jnp.zeros((c - r0, LANES), F32)], axis=0)
            out.append(_dot_nt(qt, kt.astype(BF16)))
        return out

    def stage_scores(slot, b, qs, cross):
        groups = []
        for g in range(n_groups):
            r0 = g * SUBLANES
            qg = qs[r0:r0 + SUBLANES, :]
            bg = b[r0:r0 + SUBLANES, :]
            ag = cross[g - 1] if g else jnp.zeros((SUBLANES, c), F32)
            for s in range(r0, r0 + SUBLANES):
                bs = b_ref[slot, pl.ds(s, 1), :]
                ks = k_ref[slot, pl.ds(s, 1), :]
                rs = jnp.sum(qg * jnp.exp2(bg - bs) * ks, axis=-1, keepdims=True)
                ag = jnp.where(col == s, rs, ag)
            groups.append(ag)
        return jnp.where(tri, jnp.concatenate(groups, axis=0), 0.0).astype(BF16)

    def one_iter(it, carry):
        streams = [(u, h) for u in range(chunks_per_iter) for h in range(HG_HEADS)]
        rows = [pl.ds(pl.multiple_of((it * chunks_per_iter + u) * c, c), c) for u in range(chunks_per_iter)]
        gates = [stage_gates(rows[u], h) for u, h in streams]
        for slot, (b, kk, _, _) in enumerate(gates):
            b_ref[slot] = b
            k_ref[slot] = kk
        cross = [stage_cross(slot, b, kk, qs) for slot, (b, kk, qs, _) in enumerate(gates)]
        scores = [stage_scores(slot, b, qs, cross[slot]) for slot, (b, _, qs, _) in enumerate(gates)]
        st = [st_ref[h] for h in range(HG_HEADS)]
        for slot, (u, h) in enumerate(streams):
            b, kk, qs, v = gates[slot]
            hs = slice(h * HG_KDIM, (h + 1) * HG_KDIM)
            b_last = b_ref[slot, pl.ds(c - 1, 1), :]
            o = _dot(scores[slot], v) + _dot_nt((qs * jnp.exp2(b)).astype(BF16), st[h].astype(BF16))
            kt = (kk * jnp.exp2(b_last - b)).astype(BF16)
            st[h] = st[h] * jnp.exp2(b_last) + _dot_tn(v, kt)
            o_ref[rows[u], hs] = (_rms(o) * _silu(g_ref[rows[u], hs])).astype(o_ref.dtype)
        for h in range(HG_HEADS):
            st_ref[h] = st[h]
        return carry

    lax.fori_loop(0, n_chunks // chunks_per_iter, one_iter, 0)


def _hgrn(hg, lb_param, layer, block, chunk, chunks_per_iter=1):
    bsz, s, _ = hg.shape
    slots = HG_HEADS * chunks_per_iter
    spec = lambda seg: pl.BlockSpec((None, block, HG_QK), lambda b, i, seg=seg: (b, i, seg))
    return pl.pallas_call(
        functools.partial(_hgrn_body, layer=layer, chunk=chunk, n_chunks=block // chunk,
                          chunks_per_iter=chunks_per_iter),
        grid=(bsz, s // block),
        in_specs=[spec(0), spec(1), spec(2), spec(3),
                  pl.BlockSpec((DEPTH, HG_QK), lambda b, i: (0, 0))],
        out_specs=pl.BlockSpec((None, block, HG_W), lambda b, i: (b, i, 0)),
        out_shape=jax.ShapeDtypeStruct((bsz, s, HG_W), BF16),
        scratch_shapes=[pltpu.VMEM((HG_HEADS, HG_VDIM, HG_KDIM), F32),
                        pltpu.VMEM((slots, chunk, LANES), F32),
                        pltpu.VMEM((slots, chunk, LANES), F32)],
        compiler_params=_params(("arbitrary", "arbitrary"), 32),
        name=f"hgrn2_l{layer}",
    )(hg, hg, hg, hg, lb_param)


def _ret_log_gamma(h):
    return math.log(1.0 - 2.0 ** (-5.0 - h))


def _ret_body(q_ref, k_ref, v_ref, g_ref, cos_ref, sin_ref, o_ref,
              st_ref, dm_ref, fs_ref, te_ref, bd_ref, *, chunk, n_chunks, chunks_per_iter):
    c = chunk
    qk = RET_QK
    lane_head = lax.broadcasted_iota(jnp.int32, (c, qk), 1) // RET_KDIM

    def lane_log_gamma(head_idx):
        lg = jnp.full(head_idx.shape, _ret_log_gamma(0), F32)
        for h in range(1, RET_HEADS):
            lg = jnp.where(head_idx == h, _ret_log_gamma(h), lg)
        return lg

    @pl.when((pl.program_id(0) == 0) & (pl.program_id(1) == 0))
    def _():
        t = lax.broadcasted_iota(jnp.int32, (c, qk), 0).astype(F32)
        lg = lane_log_gamma(lane_head)
        fs_ref[...] = jnp.exp(lg * (t + 1.0))
        te_ref[...] = jnp.exp(lg * (c - 1.0 - t))
        dist = (lax.broadcasted_iota(jnp.int32, (c, c), 0) - lax.broadcasted_iota(jnp.int32, (c, c), 1))
        for h in range(RET_HEADS):
            dm_ref[h] = jnp.where(dist >= 0, jnp.exp(_ret_log_gamma(h) * jnp.maximum(dist, 0).astype(F32)), 0.0)
        rh = lax.broadcasted_iota(jnp.int32, (qk, RET_W), 0) // RET_KDIM
        ch = lax.broadcasted_iota(jnp.int32, (qk, RET_W), 1) // RET_VDIM
        bd_ref[...] = jnp.where(rh == ch, jnp.exp(lane_log_gamma(rh) * float(c)), 0.0)

    @pl.when(pl.program_id(1) == 0)
    def _():
        st_ref[...] = jnp.zeros_like(st_ref)

    in_first_half = (lax.broadcasted_iota(jnp.int32, (c, LANES), 1) % RET_KDIM) < (RET_KDIM // 2)

    def rope(t, cos, sin):
        parts = []
        for p in range(qk // LANES):
            th = t[:, p * LANES:(p + 1) * LANES]
            back = pltpu.roll(th, RET_KDIM // 2, axis=1)
            fwd = pltpu.roll(th, LANES - RET_KDIM // 2, axis=1)
            parts.append(jnp.where(in_first_half, fwd, back))
        return t * cos + jnp.concatenate(parts, axis=1) * sin

    heads = range(RET_HEADS)
    hs = [slice(h * RET_VDIM, (h + 1) * RET_VDIM) for h in heads]

    def one_iter(it, carry):
        units = range(chunks_per_iter)
        rows = [pl.ds(pl.multiple_of((it * chunks_per_iter + u) * c, c), c) for u in units]
        qr, kr, v = [], [], []
        for u in units:
            cos = cos_ref[rows[u], :]
            sin = sin_ref[rows[u], :]
            qr.append(rope(q_ref[rows[u], :].astype(F32), cos, sin))
            kr.append(rope(k_ref[rows[u], :].astype(F32), cos, sin) * (RET_KDIM ** -0.5))
            v.append(v_ref[rows[u], :])
        kr_bf = [k.astype(BF16) for k in kr]
        scores = [[_dot_nt(jnp.where(lane_head == h, qr[u], 0.0).astype(BF16), kr_bf[u]) for h in heads]
                  for u in units]
        upd = [_dot_tn((kr[u] * te_ref[...]).astype(BF16), v[u]) for u in units]
        bd = bd_ref[...]
        st = st_ref[...]
        inter = []
        for u in units:
            inter.append(_dot((qr[u] * fs_ref[...]).astype(BF16), st.astype(BF16)))
            st = st * bd + jnp.where(bd > 0.0, upd[u], 0.0)
        st_ref[...] = st
        for u in units:
            g = g_ref[rows[u], :].astype(F32)
            for h in heads:
                p = (scores[u][h] * dm_ref[h]).astype(BF16)
                oh = _dot(p, v[u][:, hs[h]]) + inter[u][:, hs[h]]
                o_ref[rows[u], hs[h]] = (_rms(oh) * _silu(g[:, hs[h]])).astype(o_ref.dtype)
        return carry

    lax.fori_loop(0, n_chunks // chunks_per_iter, one_iter, 0)


def _retention(rt, cos_t, sin_t, block, chunk, name, chunks_per_iter=2):
    bsz, s, _ = rt.shape
    return pl.pallas_call(
        functools.partial(_ret_body, chunk=chunk, n_chunks=block // chunk, chunks_per_iter=chunks_per_iter),
        grid=(bsz, s // block),
        in_specs=[
            pl.BlockSpec((None, block, RET_QK), lambda b, i: (b, i, 0)),
            pl.BlockSpec((None, block, RET_QK), lambda b, i: (b, i, 1)),
            pl.BlockSpec((None, block, RET_W), lambda b, i: (b, i, 1)),
            pl.BlockSpec((None, block, RET_W), lambda b, i: (b, i, 2)),
            pl.BlockSpec((block, RET_QK), lambda b, i: (i, 0)),
            pl.BlockSpec((block, RET_QK), lambda b, i: (i, 0)),
        ],
        out_specs=pl.BlockSpec((None, block, RET_W), lambda b, i: (b, i, 0)),
        out_shape=jax.ShapeDtypeStruct((bsz, s, RET_W), BF16),
        scratch_shapes=[
            pltpu.VMEM((RET_QK, RET_W), F32),
            pltpu.VMEM((RET_HEADS, chunk, chunk), F32),
            pltpu.VMEM((chunk, RET_QK), F32),
            pltpu.VMEM((chunk, RET_QK), F32),
            pltpu.VMEM((RET_QK, RET_W), F32),
        ],
        compiler_params=_params(("arbitrary", "arbitrary"), 32),
        name=name,
    )(rt, rt, rt, rt, cos_t, sin_t)


def _t5_bucket_np(distance):
    max_exact = N_BUCKETS // 2
    n = np.maximum(distance, 1).astype(np.float32)
    large = max_exact + (np.log(n / max_exact) / math.log(MAX_DISTANCE / max_exact)
                         * (N_BUCKETS - max_exact)).astype(np.int32)
    large = np.minimum(large, N_BUCKETS - 1)
    return np.where(distance < max_exact, distance, large)


def _bucket_table(dilation):
    a = np.arange(ATT_BLK)[:, None]
    kk = np.arange(2 * ATT_BLK)[None, :]
    j = a + ATT_BLK - kk
    valid = (j >= 0) & (j <= ATT_BLK)
    bucket = _t5_bucket_np(np.clip(j, 0, ATT_BLK) * dilation)
    return np.where(valid, bucket, -1).astype(np.int32)


def _in_attn_body(h_ref, *refs):
    n_pieces = ATT_COLS // IN_ATTN_PIECE
    w_refs, outs, (res_ref, mid_ref) = refs[:n_pieces], refs[n_pieces:-2], refs[-2:]
    (_, d1), (_, d4), (_, d16) = DILATED_CONFIGS
    step = d16 // d4
    assert d1 == 1 and step == d4
    tm = h_ref.shape[0]
    piece = IN_ATTN_PIECE
    blocks = piece // LANES
    h = h_ref[...]
    for p in range(ATT_COLS // piece):
        seg, off = divmod(p * piece, ATT_W)
        o1_ref, o4_ref, o16_ref = outs[seg], outs[3 + seg], outs[6 + seg]
        val = _dot(h, w_refs[p][...])
        if seg == 0:
            val = val * (ATT_HDIM ** -0.5 * LOG2E)
        o1_ref[:, off:off + piece] = val.astype(BF16)
        for cb in range(blocks):
            slab = p * blocks + cb
            res_ref[slab] = val[:, cb * LANES:(cb + 1) * LANES]
            for r4 in range(d4):
                part = res_ref.at[slab][pl.ds(r4, tm // d4, stride=d4), :]
                lo = r4 * ATT_W + off + cb * LANES
                o4_ref[:, lo:lo + LANES] = part.astype(BF16)
                mid_ref[slab * d4 + r4] = part
                for m in range(step):
                    lo = (r4 + d4 * m) * ATT_W + off + cb * LANES
                    rows = pl.ds(m, tm // d16, stride=step)
                    o16_ref[:, lo:lo + LANES] = mid_ref.at[slab * d4 + r4][rows, :].astype(BF16)


def _in_attn(h, w, layer, col0, tm, name):
    t, dm = h.shape
    dilations = [d for _, d in DILATED_CONFIGS]
    piece = IN_ATTN_PIECE
    n_pieces = ATT_COLS // piece
    w_specs = [pl.BlockSpec((None, dm, piece), lambda i, p=p: (layer, 0, col0 // piece + p),
                            pipeline_mode=pl.Buffered(1)) for p in range(n_pieces)]
    out_shape, out_specs = [], []
    for d in dilations:
        for _ in range(3):
            out_shape.append(jax.ShapeDtypeStruct((t // d, d * ATT_W), BF16))
            out_specs.append(pl.BlockSpec((tm // d, d * ATT_W), lambda i: (i, 0)))
    outs = pl.pallas_call(
        _in_attn_body,
        grid=(t // tm,),
        in_specs=[pl.BlockSpec((tm, dm), lambda i: (i, 0))] + w_specs,
        out_specs=out_specs,
        out_shape=out_shape,
        scratch_shapes=[pltpu.VMEM((ATT_COLS // LANES, tm, LANES), F32),
                        pltpu.VMEM((ATT_COLS // LANES * dilations[1], tm // dilations[1], LANES), F32)],
        compiler_params=_params(("arbitrary",), 56),
        name=name,
    )(h, *([w] * n_pieces))
    return [tuple(outs[3 * di:3 * di + 3]) for di in range(len(dilations))]


def _attn_body(rb_ref, idx_ref, q_ref, kc_ref, kp_ref, vc_ref, vp_ref, o_ref, st_ref, bias_ref, *, n_sub):
    blk = ATT_BLK
    heads = range(ATT_HEADS)

    @pl.when((pl.program_id(0) == 0) & (pl.program_id(1) == 0) & (pl.program_id(2) == 0))
    def _():
        idx = idx_ref[...]
        in_prev = lax.broadcasted_iota(jnp.int32, idx.shape, 1) < blk
        for h in heads:
            acc = jnp.full(idx.shape, NEG, F32)
            for n in range(N_BUCKETS):
                acc = jnp.where(idx == n, rb_ref[n, h] * LOG2E, acc)
            bias_ref[0, h] = acc
            bias_ref[1, h] = jnp.where(in_prev, NEG, acc)

    sel0 = jnp.where(pl.program_id(2) == 0, 1, 0)
    hs = [slice(h * ATT_HDIM, (h + 1) * ATT_HDIM) for h in heads]
    lane = lax.broadcasted_iota(jnp.int32, (blk, LANES), 1)

    def qk(t):
        rows = slice(t * blk, (t + 1) * blk)
        out = []
        for h in heads:
            q = q_ref[rows, hs[h]]
            if t == 0:
                out.append((_dot_nt(q, kp_ref[:, hs[h]]) + bias_ref[sel0, h, :, 0:blk],
                            _dot_nt(q, kc_ref[rows, hs[h]]) + bias_ref[sel0, h, :, blk:2 * blk]))
            else:
                l2 = _dot_nt(q, kc_ref[(t - 1) * blk:(t + 1) * blk, hs[h]]) + bias_ref[0, h]
                out.append((l2[:, 0:blk], l2[:, blk:2 * blk]))
        return out

    def softmax(t, logits):
        probs = []
        stats = jnp.zeros((blk, LANES), F32)
        for h in heads:
            lp, lc = logits[h]
            m = jnp.max(jnp.maximum(lp, lc), axis=-1, keepdims=True)
            pp = jnp.exp2(lp - m)
            pc = jnp.exp2(lc - m)
            den = jnp.sum(pp + pc, axis=-1, keepdims=True)
            probs.append((pp.astype(BF16), pc.astype(BF16), 1.0 / den))
            stats = jnp.where(lane == h, m, stats)
            stats = jnp.where(lane == ATT_HEADS + h, den, stats)
        st_ref[t * blk:(t + 1) * blk, :] = stats
        return probs

    def pv(t, probs):
        rows = slice(t * blk, (t + 1) * blk)
        for h in heads:
            pp, pc, inv = probs[h]
            vprev = vp_ref[:, hs[h]] if t == 0 else vc_ref[(t - 1) * blk:t * blk, hs[h]]
            o = _dot(pp, vprev) + _dot(pc, vc_ref[rows, hs[h]])
            o_ref[rows, hs[h]] = (o * inv).astype(o_ref.dtype)

    logits = qk(0)
    for t in range(n_sub):
        probs = softmax(t, logits)
        if t + 1 < n_sub:
            logits = qk(t + 1)
        pv(t, probs)


def _attn_branch(q, k, v, rel_bias, bsz, dilation, n_sub, name):
    rows = q.shape[0]
    l = rows // bsz
    blk = ATT_BLK
    lb = n_sub * blk
    nb = l // lb
    q, k, v = (a.reshape(bsz, l, dilation * ATT_W) for a in (q, k, v))
    idx = jnp.asarray(_bucket_table(dilation))
    cur = pl.BlockSpec((None, lb, ATT_W), lambda b, r, i: (b, i, r))
    prev = pl.BlockSpec((None, blk, ATT_W), lambda b, r, i: (b, jnp.maximum(i * n_sub - 1, 0), r))
    o, st = pl.pallas_call(
        functools.partial(_attn_body, n_sub=n_sub),
        grid=(bsz, dilation, nb),
        in_specs=[
            pl.BlockSpec(memory_space=pltpu.SMEM),
            pl.BlockSpec((blk, 2 * blk), lambda b, r, i: (0, 0)),
            cur, cur, prev, cur, prev,
        ],
        out_specs=[
            pl.BlockSpec((None, lb, ATT_W), lambda b, r, i: (b, i, r)),
            pl.BlockSpec((None, lb, LANES), lambda b, r, i: (b, i, r)),
        ],
        out_shape=[
            jax.ShapeDtypeStruct((bsz, l, dilation * ATT_W), BF16),
            jax.ShapeDtypeStruct((bsz, l, dilation * LANES), F32),
        ],
        scratch_shapes=[pltpu.VMEM((2, ATT_HEADS, blk, 2 * blk), F32)],
        compiler_params=_params(("arbitrary", "arbitrary", "arbitrary"), 32),
        name=name,
    )(rel_bias, idx, q, k, k, v, v)
    return o.reshape(rows, dilation * ATT_W), st.reshape(rows, dilation * LANES)


def _merge_heads(o_refs, s_refs, scratch, tm):
    scratch = list(scratch)
    heads, stats = [], []
    for (_, d), o_ref, s_ref in zip(DILATED_CONFIGS, o_refs, s_refs):
        if d == 1:
            heads.append(lambda h, o_ref=o_ref: o_ref[:, h * ATT_HDIM:(h + 1) * ATT_HDIM].astype(F32))
            stats.append(s_ref[...])
            continue
        po_ref, ps_ref = scratch.pop(0), scratch.pop(0)
        hop = MERGE_STRIDE
        if d > hop:
            mo_ref, ms_ref = scratch.pop(0), scratch.pop(0)
        for r in range(d):
            blocks = [o_ref[:, r * ATT_W + h * ATT_HDIM:r * ATT_W + (h + 1) * ATT_HDIM].astype(F32)
                      for h in range(ATT_HEADS)] + [s_ref[:, r * LANES:(r + 1) * LANES]]
            if d <= hop:
                rows = pl.ds(r, tm // d, stride=d)
                for h in range(ATT_HEADS):
                    po_ref.at[h][rows, :] = blocks[h]
                ps_ref[rows, :] = blocks[-1]
            else:
                r_lo, m = r % hop, r // hop
                rows = pl.ds(m, tm // d, stride=d // hop)
                for h in range(ATT_HEADS):
                    mo_ref.at[r_lo * ATT_HEADS + h][rows, :] = blocks[h]
                ms_ref.at[r_lo][rows, :] = blocks[-1]
        if d > hop:
            for r_lo in range(hop):
                rows = pl.ds(r_lo, tm // hop, stride=hop)
                for h in range(ATT_HEADS):
                    po_ref.at[h][rows, :] = mo_ref[r_lo * ATT_HEADS + h]
                ps_ref[rows, :] = ms_ref[r_lo]
        heads.append(lambda h, po_ref=po_ref: po_ref[h])
        stats.append(ps_ref[...])
    mx = functools.reduce(jnp.maximum, stats)
    ws = [pltpu.roll(s, LANES - ATT_HEADS, axis=1) * jnp.exp2(s - mx) for s in stats]
    inv = 1.0 / functools.reduce(lambda a, b: a + b, ws)
    ws = [w * inv for w in ws]

    def head(h):
        acc = ws[0][:, h:h + 1] * heads[0](h)
        for w, get in zip(ws[1:], heads[1:]):
            acc = acc + w[:, h:h + 1] * get(h)
        return acc

    return head


def _outproj_body(x_ref, a_ref, b_ref, *refs):
    n = len(DILATED_CONFIGS)
    o_refs, s_refs = refs[:n], refs[n:2 * n]
    w_ref, g_ref, o_ref, u_ref = refs[2 * n:2 * n + 4]
    tm = x_ref.shape[0]
    acc = x_ref[...]
    acc = acc + _dot(a_ref[...], w_ref[0:HG_W, :])
    acc = acc + _dot(b_ref[...], w_ref[HG_W:HG_W + RET_W, :])
    head = _merge_heads(o_refs, s_refs, refs[2 * n + 4:], tm)
    for h in range(0, ATT_HEADS, 2):
        c2 = jnp.concatenate([head(h), head(h + 1)], axis=1).astype(BF16)
        r0 = HG_W + RET_W + h * ATT_HDIM
        acc = acc + _dot(c2, w_ref[r0:r0 + 2 * ATT_HDIM, :])
    o_ref[...] = acc
    u_ref[...] = (_rms(acc) * g_ref[...]).astype(u_ref.dtype)


def _outproj(x, a, b, outs, stats, w, layer, g, tm, name):
    t, d = x.shape
    row = lambda width: pl.BlockSpec((tm, width), lambda i: (i, 0))
    in_specs, scratch = [row(d), row(HG_W), row(RET_W)], []
    for width in (ATT_W, LANES):
        for _, dil in DILATED_CONFIGS:
            in_specs.append(pl.BlockSpec((tm // dil, dil * width), lambda i: (i, 0)))
    for _, dil in DILATED_CONFIGS:
        if dil > 1:
            scratch += [pltpu.VMEM((ATT_HEADS, tm, ATT_HDIM), F32), pltpu.VMEM((tm, LANES), F32)]
        if dil > MERGE_STRIDE:
            scratch += [pltpu.VMEM((MERGE_STRIDE * ATT_HEADS, tm // MERGE_STRIDE, ATT_HDIM), F32),
                        pltpu.VMEM((MERGE_STRIDE, tm // MERGE_STRIDE, LANES), F32)]
    in_specs += [pl.BlockSpec((None,) + w.shape[1:], lambda i: (layer, 0, 0), pipeline_mode=pl.Buffered(1)),
                 pl.BlockSpec((1, d), lambda i: (0, 0))]
    return pl.pallas_call(
        _outproj_body,
        grid=(t // tm,),
        in_specs=in_specs,
        out_specs=[row(d), row(d)],
        out_shape=[jax.ShapeDtypeStruct((t, d), F32), jax.ShapeDtypeStruct((t, d), BF16)],
        scratch_shapes=scratch,
        compiler_params=_params(("arbitrary",), 56),
        name=name,
    )(x, a, b, *outs, *stats, w, g.reshape(1, d))


def _ffn1_body(u_ref, wg_ref, wu_ref, cw_ref, cb_ref, o_ref, carry_ref, *, tiles_per_seq):
    i = pl.program_id(0)
    j = pl.program_id(1)
    tm = u_ref.shape[0]
    u = u_ref[...]
    gp = _dot(u, wg_ref[...])
    up = _dot(u, wu_ref[...])
    w0 = cw_ref[0:1, :]
    w1 = cw_ref[1:2, :]
    w2 = cw_ref[2:3, :]
    cb = cb_ref[...]

    gate = w2 * gp + w1 * pltpu.roll(gp, 1, axis=0) + w0 * pltpu.roll(gp, 2, axis=0) + cb
    o_ref[...] = (_silu(gate) * up).astype(o_ref.dtype)

    prev = jnp.where(i % tiles_per_seq == 0, 0.0, carry_ref[j])
    carry_ref[j] = gp[tm - SUBLANES:, :]
    top = gp[0:SUBLANES, :]
    r = lax.broadcasted_iota(jnp.int32, top.shape, 0)
    p1 = prev[SUBLANES - 1:SUBLANES, :]
    p2 = prev[SUBLANES - 2:SUBLANES - 1, :]
    t1 = jnp.where(r == 0, p1, pltpu.roll(top, 1, axis=0))
    t2 = jnp.where(r == 0, p2, jnp.where(r == 1, p1, pltpu.roll(top, 2, axis=0)))
    gate_top = w2 * top + w1 * t1 + w0 * t2 + cb
    o_ref[0:SUBLANES, :] = (_silu(gate_top) * up[0:SUBLANES, :]).astype(o_ref.dtype)


def _ffn1(u, wg, wu, cw, cb, layer, seq_len, tm, tn, name):
    t, d = u.shape
    n = wg.shape[2]
    nj = n // tn
    wspec = lambda rows: pl.BlockSpec((None, rows, tn), lambda i, j: (layer, 0, j))
    return pl.pallas_call(
        functools.partial(_ffn1_body, tiles_per_seq=seq_len // tm),
        grid=(t // tm, nj),
        in_specs=[pl.BlockSpec((tm, d), lambda i, j: (i, 0)), wspec(d), wspec(d), wspec(cw.shape[1]), wspec(1)],
        out_specs=pl.BlockSpec((tm, tn), lambda i, j: (i, j)),
        out_shape=jax.ShapeDtypeStruct((t, n), BF16),
        scratch_shapes=[pltpu.VMEM((nj, SUBLANES, tn), F32)],
        compiler_params=_params(("arbitrary", "arbitrary"), 48),
        name=name,
    )(u, wg, wu, cw, cb)


def _ffn2_body(x_ref, h_ref, w_ref, g_ref, *out_refs, last_layer):
    y = x_ref[...] + _dot(h_ref[...], w_ref[...])
    normed = _rms(y) * g_ref[...]
    if last_layer:
        out_refs[0][...] = normed
    else:
        out_refs[0][...] = y
        out_refs[1][...] = normed.astype(out_refs[1].dtype)


def _ffn2(x, h, w, layer, g, last_layer, tm, name):
    t, d = x.shape
    kdim = h.shape[1]
    row = pl.BlockSpec((tm, d), lambda i: (i, 0))
    out_shape = [jax.ShapeDtypeStruct((t, d), F32)]
    if not last_layer:
        out_shape.append(jax.ShapeDtypeStruct((t, d), BF16))
    return pl.pallas_call(
        functools.partial(_ffn2_body, last_layer=last_layer),
        grid=(t // tm,),
        in_specs=[
            row,
            pl.BlockSpec((tm, kdim), lambda i: (i, 0)),
            pl.BlockSpec((None, kdim, d), lambda i: (layer, 0, 0), pipeline_mode=pl.Buffered(1)),
            pl.BlockSpec((1, d), lambda i: (0, 0)),
        ],
        out_specs=[row] * len(out_shape),
        out_shape=out_shape,
        compiler_params=_params(("arbitrary",), 60),
        name=name,
    )(x, h, w, g.reshape(1, d))


def _rope_tables(s):
    inv_freq = ROPE_BASE ** (-jnp.arange(0, RET_KDIM, 2, dtype=F32) / RET_KDIM)
    ang = jnp.arange(s, dtype=F32)[:, None] * inv_freq[None, :]
    cos, sin = jnp.cos(ang), jnp.sin(ang)
    cos_t = jnp.tile(jnp.concatenate([cos, cos], axis=1), (1, RET_HEADS))
    sin_t = jnp.tile(jnp.concatenate([-sin, sin], axis=1), (1, RET_HEADS))
    return cos_t, sin_t


def kernel(x, norm_mix, w_in, hg_lower_bound, w_out, norm_ffn, w_gate, conv_w, conv_b,
           w_up, w_down, rel_bias, norm_final):
    bsz, s, d = x.shape
    t = bsz * s
    pad = D_FF_PAD - D_FF
    cos_t, sin_t = _rope_tables(s)
    x2 = x.reshape(t, d)
    w_in_bf = w_in.astype(BF16)
    w_out_bf = w_out.astype(BF16)
    wg = jnp.concatenate([w_gate.astype(BF16), jnp.zeros((DEPTH, d, pad), BF16)], axis=2)
    wu = jnp.concatenate([w_up.astype(BF16), jnp.zeros((DEPTH, d, pad), BF16)], axis=2)
    wd = jnp.concatenate([w_down.astype(BF16), jnp.zeros((DEPTH, pad, d), BF16)], axis=1)
    cw = jnp.pad(conv_w, ((0, 0), (0, 0), (0, pad)))
    cb = jnp.pad(conv_b, ((0, 0), (0, pad))).reshape(DEPTH, 1, D_FF_PAD)
    h = _norm(x2, norm_mix[0], 1024, "norm_in")
    for l in range(DEPTH):
        last = l == DEPTH - 1
        hg = _matmul(h, w_in_bf, l, 0, HG_COLS, F32, 1024, 1024, f"in_hgrn_l{l}")
        rt = _matmul(h, w_in_bf, l, HG_COLS, RET_COLS, BF16, 1024, 512, f"in_ret_l{l}")
        qkv = _in_attn(h, w_in_bf, l, HG_COLS + RET_COLS, 512, f"in_attn_l{l}")

        a = _hgrn(hg.reshape(bsz, s, HG_COLS), hg_lower_bound, l, block=512, chunk=32, chunks_per_iter=4)
        b = _retention(rt.reshape(bsz, s, RET_COLS), cos_t, sin_t, block=512, chunk=128, name=f"retention_l{l}")
        outs, stats = zip(*[_attn_branch(*qkv[di], rel_bias, bsz, dil, 4, f"attn_d{dil}_l{l}")
                            for di, (_, dil) in enumerate(DILATED_CONFIGS)])
        x2, u = _outproj(x2, a.reshape(t, HG_W), b.reshape(t, RET_W), outs, stats, w_out_bf, l,
                         norm_ffn[l], 512, f"out_proj_l{l}")
        hmid = _ffn1(u, wg, wu, cw, cb, l, s, 1024, 512, f"ffn_gate_up_l{l}")
        res = _ffn2(x2, hmid, wd, l, norm_final if last else norm_mix[l + 1], last, 512, f"ffn_down_l{l}")
        if last:
            x2 = res[0]
        else:
            x2, h = res
    return x2.reshape(bsz, s, d)
```

```python
import functools
import math

import numpy as np
import jax
import jax.numpy as jnp
from jax import lax
from jax.experimental import pallas as pl
from jax.experimental.pallas import tpu as pltpu

F32 = jnp.float32
BF16 = jnp.bfloat16

D_MODEL = 2048
DEPTH = 2
HG_HEADS = 4
HG_KDIM = 128
HG_VDIM = 128
RET_HEADS = 4
RET_KDIM = 64
RET_VDIM = 128
ROPE_BASE = 10000.0
ATT_HEADS = 8
ATT_HDIM = 128
DILATED_CONFIGS = ((128, 1), (512, 4), (2048, 16))
N_BUCKETS = 32
MAX_DISTANCE = 2048
D_FF = 5504
EPS = 1e-6

HG_QK = HG_HEADS * HG_KDIM
HG_W = HG_HEADS * HG_VDIM
RET_QK = RET_HEADS * RET_KDIM
RET_W = RET_HEADS * RET_VDIM
ATT_W = ATT_HEADS * ATT_HDIM
HG_COLS = 2 * HG_QK + 2 * HG_W
RET_COLS = 2 * RET_QK + 2 * RET_W
ATT_COLS = 3 * ATT_W

LANES = 128
SUBLANES = 8
D_FF_PAD = 5632
ATT_BLK = 128
IN_ATTN_PIECE = 512
MERGE_STRIDE = 4
NEG = -1e30
LOG2E = math.log2(math.e)
MIB = 1024 * 1024


def _dot(a, b):
    return jnp.dot(a, b, preferred_element_type=F32)


def _dot_nt(a, b):
    return lax.dot_general(a, b, (((1,), (1,)), ((), ())), preferred_element_type=F32)


def _dot_tn(a, b):
    return lax.dot_general(a, b, (((0,), (0,)), ((), ())), preferred_element_type=F32)


def _params(semantics, vmem_mib):
    return pltpu.CompilerParams(dimension_semantics=semantics, vmem_limit_bytes=vmem_mib * MIB)


def _rms(x):
    return x * lax.rsqrt(jnp.mean(x * x, axis=-1, keepdims=True) + EPS)


def _silu(x):
    return x * jax.nn.sigmoid(x)


def _norm_body(x_ref, g_ref, o_ref):
    o_ref[...] = (_rms(x_ref[...]) * g_ref[...]).astype(o_ref.dtype)


def _norm(x, g, tm, name):
    t, d = x.shape
    return pl.pallas_call(
        _norm_body,
        grid=(t // tm,),
        in_specs=[pl.BlockSpec((tm, d), lambda i: (i, 0)), pl.BlockSpec((1, d), lambda i: (0, 0))],
        out_specs=pl.BlockSpec((tm, d), lambda i: (i, 0)),
        out_shape=jax.ShapeDtypeStruct((t, d), BF16),
        compiler_params=_params(("arbitrary",), 40),
        name=name,
    )(x, g.reshape(1, d))


def _matmul_body(h_ref, w_ref, o_ref):
    o_ref[...] = _dot(h_ref[...], w_ref[...]).astype(o_ref.dtype)


def _matmul(h, w, layer, col0, n, out_dtype, tm, tn, name):
    t, d = h.shape
    j0 = col0 // tn
    return pl.pallas_call(
        _matmul_body,
        grid=(t // tm, n // tn),
        in_specs=[pl.BlockSpec((tm, d), lambda i, j: (i, 0)),
                  pl.BlockSpec((None, d, tn), lambda i, j: (layer, 0, j + j0))],
        out_specs=pl.BlockSpec((tm, tn), lambda i, j: (i, j)),
        out_shape=jax.ShapeDtypeStruct((t, n), out_dtype),
        compiler_params=_params(("arbitrary", "arbitrary"), 48),
        name=name,
    )(h, w)


def _hgrn_body(q_ref, f_ref, i_ref, g_ref, lbp_ref, o_ref, st_ref, b_ref, k_ref, *,
               layer, chunk, n_chunks, chunks_per_iter):
    c = chunk
    n_groups = c // SUBLANES

    @pl.when(pl.program_id(1) == 0)
    def _():
        st_ref[...] = jnp.zeros_like(st_ref)

    p = lbp_ref[...]
    e = jnp.exp(p - jnp.max(p, axis=0, keepdims=True))
    sm = e / jnp.sum(e, axis=0, keepdims=True)
    lb_all = jnp.zeros((1, HG_QK), F32)
    for m in range(1, layer + 1):
        lb_all = lb_all + sm[m:m + 1, :]

    tri = lax.broadcasted_iota(jnp.int32, (c, c), 0) >= lax.broadcasted_iota(jnp.int32, (c, c), 1)
    tri_bf = jnp.where(tri, 1.0, 0.0).astype(BF16)
    col = lax.broadcasted_iota(jnp.int32, (SUBLANES, c), 1)

    def stage_gates(rows, h):
        hs = slice(h * HG_KDIM, (h + 1) * HG_KDIM)
        lb = lb_all[:, hs]
        oml = 1.0 - lb
        xf = f_ref[rows, hs]
        sig = jax.nn.sigmoid(xf)
        log2f = jnp.log2(lb + oml * sig)
        kk = oml * (1.0 - sig)
        hi = log2f.astype(BF16)
        r1 = log2f - hi.astype(F32)
        mid = r1.astype(BF16)
        lo = (r1 - mid.astype(F32)).astype(BF16)
        cs = _dot(tri_bf, jnp.concatenate([hi, mid, lo], axis=1))
        b = cs[:, 0:LANES] + cs[:, LANES:2 * LANES] + cs[:, 2 * LANES:3 * LANES]
        return b, kk, _silu(q_ref[rows, hs]), i_ref[rows, hs].astype(BF16)

    def stage_cross(slot, b, kk, qs):
        out = []
        for g in range(1, n_groups):
            r0 = g * SUBLANES
            ref = b_ref[slot, pl.ds(r0 - 1, 1), :]
            qt = (qs[r0:r0 + SUBLANES, :] * jnp.exp2(b[r0:r0 + SUBLANES, :] - ref)).astype(BF16)
            kt = jnp.concatenate([kk[0:r0, :] * jnp.exp2(ref - b[0:r0, :]), jnp.zeros((c - r0, LANES), F32)], axis=0)
            out.append(_dot_nt(qt, kt.astype(BF16)))
        return out

    def stage_scores(slot, b, qs, cross):
        groups = []
        for g in range(n_groups):
            r0 = g * SUBLANES
            qg = qs[r0:r0 + SUBLANES, :]
            bg = b[r0:r0 + SUBLANES, :]
            ag = cross[g - 1] if g else jnp.zeros((SUBLANES, c), F32)
            for s in range(r0, r0 + SUBLANES):
                bs = b_ref[slot, pl.ds(s, 1), :]
                ks = k_ref[slot, pl.ds(s, 1), :]
                rs = jnp.sum(qg * jnp.exp2(bg - bs) * ks, axis=-1, keepdims=True)
                ag = jnp.where(col == s, rs, ag)
            groups.append(ag)
        return jnp.where(tri, jnp.concatenate(groups, axis=0), 0.0).astype(BF16)

    def one_iter(it, carry):
        streams = [(u, h) for u in range(chunks_per_iter) for h in range(HG_HEADS)]
        rows = [pl.ds(pl.multiple_of((it * chunks_per_iter + u) * c, c), c) for u in range(chunks_per_iter)]
        gates = [stage_gates(rows[u], h) for u, h in streams]
        for slot, (b, kk, _, _) in enumerate(gates):
            b_ref[slot] = b
            k_ref[slot] = kk
        cross = [stage_cross(slot, b, kk, qs) for slot, (b, kk, qs, _) in enumerate(gates)]
        scores = [stage_scores(slot, b, qs, cross[slot]) for slot, (b, _, qs, _) in enumerate(gates)]
        st = [st_ref[h] for h in range(HG_HEADS)]
        for slot, (u, h) in enumerate(streams):
            b, kk, qs, v = gates[slot]
            hs = slice(h * HG_KDIM, (h + 1) * HG_KDIM)
            b_last = b_ref[slot, pl.ds(c - 1, 1), :]
            o = _dot(scores[slot], v) + _dot_nt((qs * jnp.exp2(b)).astype(BF16), st[h].astype(BF16))
            kt = (kk * jnp.exp2(b_last - b)).astype(BF16)
            st[h] = st[h] * jnp.exp2(b_last) + _dot_tn(v, kt)
            o_ref[rows[u], hs] = (_rms(o) * _silu(g_ref[rows[u], hs])).astype(o_ref.dtype)
        for h in range(HG_HEADS):
            st_ref[h] = st[h]
        return carry

    lax.fori_loop(0, n_chunks // chunks_per_iter, one_iter, 0)


def _hgrn(hg, lb_param, layer, block, chunk, chunks_per_iter=1):
    bsz, s, _ = hg.shape
    slots = HG_HEADS * chunks_per_iter
    spec = lambda seg: pl.BlockSpec((None, block, HG_QK), lambda b, i, seg=seg: (b, i, seg))
    return pl.pallas_call(
        functools.partial(_hgrn_body, layer=layer, chunk=chunk, n_chunks=block // chunk,
                          chunks_per_iter=chunks_per_iter),
        grid=(bsz, s // block),
        in_specs=[spec(0), spec(1), spec(2), spec(3),
                  pl.BlockSpec((DEPTH, HG_QK), lambda b, i: (0, 0))],
        out_specs=pl.BlockSpec((None, block, HG_W), lambda b, i: (b, i, 0)),
        out_shape=jax.ShapeDtypeStruct((bsz, s, HG_W), BF16),
        scratch_shapes=[pltpu.VMEM((HG_HEADS, HG_VDIM, HG_KDIM), F32),
                        pltpu.VMEM((slots, chunk, LANES), F32),
                        pltpu.VMEM((slots, chunk, LANES), F32)],
        compiler_params=_params(("arbitrary", "arbitrary"), 32),
        name=f"hgrn2_l{layer}",
    )(hg, hg, hg, hg, lb_param)


def _ret_log_gamma(h):
    return math.log(1.0 - 2.0 ** (-5.0 - h))


def _ret_body(q_ref, k_ref, v_ref, g_ref, cos_ref, sin_ref, o_ref,
              st_ref, dm_ref, fs_ref, te_ref, bd_ref, *, chunk, n_chunks, chunks_per_iter):
    c = chunk
    qk = RET_QK
    lane_head = lax.broadcasted_iota(jnp.int32, (c, qk), 1) // RET_KDIM

    def lane_log_gamma(head_idx):
        lg = jnp.full(head_idx.shape, _ret_log_gamma(0), F32)
        for h in range(1, RET_HEADS):
            lg = jnp.where(head_idx == h, _ret_log_gamma(h), lg)
        return lg

    @pl.when((pl.program_id(0) == 0) & (pl.program_id(1) == 0))
    def _():
        t = lax.broadcasted_iota(jnp.int32, (c, qk), 0).astype(F32)
        lg = lane_log_gamma(lane_head)
        fs_ref[...] = jnp.exp(lg * (t + 1.0))
        te_ref[...] = jnp.exp(lg * (c - 1.0 - t))
        dist = (lax.broadcasted_iota(jnp.int32, (c, c), 0) - lax.broadcasted_iota(jnp.int32, (c, c), 1))
        for h in range(RET_HEADS):
            dm_ref[h] = jnp.where(dist >= 0, jnp.exp(_ret_log_gamma(h) * jnp.maximum(dist, 0).astype(F32)), 0.0)
        rh = lax.broadcasted_iota(jnp.int32, (qk, RET_W), 0) // RET_KDIM
        ch = lax.broadcasted_iota(jnp.int32, (qk, RET_W), 1) // RET_VDIM
        bd_ref[...] = jnp.where(rh == ch, jnp.exp(lane_log_gamma(rh) * float(c)), 0.0)

    @pl.when(pl.program_id(1) == 0)
    def _():
        st_ref[...] = jnp.zeros_like(st_ref)

    in_first_half = (lax.broadcasted_iota(jnp.int32, (c, LANES), 1) % RET_KDIM) < (RET_KDIM // 2)

    def rope(t, cos, sin):
        parts = []
        for p in range(qk // LANES):
            th = t[:, p * LANES:(p + 1) * LANES]
            back = pltpu.roll(th, RET_KDIM // 2, axis=1)
            fwd = pltpu.roll(th, LANES - RET_KDIM // 2, axis=1)
            parts.append(jnp.where(in_first_half, fwd, back))
        return t * cos + jnp.concatenate(parts, axis=1) * sin

    heads = range(RET_HEADS)
    hs = [slice(h * RET_VDIM, (h + 1) * RET_VDIM) for h in heads]

    def one_iter(it, carry):
        units = range(chunks_per_iter)
        rows = [pl.ds(pl.multiple_of((it * chunks_per_iter + u) * c, c), c) for u in units]
        qr, kr, v = [], [], []
        for u in units:
            cos = cos_ref[rows[u], :]
            sin = sin_ref[rows[u], :]
            qr.append(rope(q_ref[rows[u], :].astype(F32), cos, sin))
            kr.append(rope(k_ref[rows[u], :].astype(F32), cos, sin) * (RET_KDIM ** -0.5))
            v.append(v_ref[rows[u], :])
        kr_bf = [k.astype(BF16) for k in kr]
        scores = [[_dot_nt(jnp.where(lane_head == h, qr[u], 0.0).astype(BF16), kr_bf[u]) for h in heads]
                  for u in units]
        upd = [_dot_tn((kr[u] * te_ref[...]).astype(BF16), v[u]) for u in units]
        bd = bd_ref[...]
        st = st_ref[...]
        inter = []
        for u in units:
            inter.append(_dot((qr[u] * fs_ref[...]).astype(BF16), st.astype(BF16)))
            st = st * bd + jnp.where(bd > 0.0, upd[u], 0.0)
        st_ref[...] = st
        for u in units:
            g = g_ref[rows[u], :].astype(F32)
            for h in heads:
                p = (scores[u][h] * dm_ref[h]).astype(BF16)
                oh = _dot(p, v[u][:, hs[h]]) + inter[u][:, hs[h]]
                o_ref[rows[u], hs[h]] = (_rms(oh) * _silu(g[:, hs[h]])).astype(o_ref.dtype)
        return carry

    lax.fori_loop(0, n_chunks // chunks_per_iter, one_iter, 0)


def _retention(rt, cos_t, sin_t, block, chunk, name, chunks_per_iter=2):
    bsz, s, _ = rt.shape
    return pl.pallas_call(
        functools.partial(_ret_body, chunk=chunk, n_chunks=block // chunk, chunks_per_iter=chunks_per_iter),
        grid=(bsz, s // block),
        in_specs=[
            pl.BlockSpec((None, block, RET_QK), lambda b, i: (b, i, 0)),
            pl.BlockSpec((None, block, RET_QK), lambda b, i: (b, i, 1)),
            pl.BlockSpec((None, block, RET_W), lambda b, i: (b, i, 1)),
            pl.BlockSpec((None, block, RET_W), lambda b, i: (b, i, 2)),
            pl.BlockSpec((block, RET_QK), lambda b, i: (i, 0)),
            pl.BlockSpec((block, RET_QK), lambda b, i: (i, 0)),
        ],
        out_specs=pl.BlockSpec((None, block, RET_W), lambda b, i: (b, i, 0)),
        out_shape=jax.ShapeDtypeStruct((bsz, s, RET_W), BF16),
        scratch_shapes=[
            pltpu.VMEM((RET_QK, RET_W), F32),
            pltpu.VMEM((RET_HEADS, chunk, chunk), F32),
            pltpu.VMEM((chunk, RET_QK), F32),
            pltpu.VMEM((chunk, RET_QK), F32),
            pltpu.VMEM((RET_QK, RET_W), F32),
        ],
        compiler_params=_params(("arbitrary", "arbitrary"), 32),
        name=name,
    )(rt, rt, rt, rt, cos_t, sin_t)


def _t5_bucket_np(distance):
    max_exact = N_BUCKETS // 2
    n = np.maximum(distance, 1).astype(np.float32)
    large = max_exact + (np.log(n / max_exact) / math.log(MAX_DISTANCE / max_exact)
                         * (N_BUCKETS - max_exact)).astype(np.int32)
    large = np.minimum(large, N_BUCKETS - 1)
    return np.where(distance < max_exact, distance, large)


def _bucket_table(dilation):
    a = np.arange(ATT_BLK)[:, None]
    kk = np.arange(2 * ATT_BLK)[None, :]
    j = a + ATT_BLK - kk
    valid = (j >= 0) & (j <= ATT_BLK)
    bucket = _t5_bucket_np(np.clip(j, 0, ATT_BLK) * dilation)
    return np.where(valid, bucket, -1).astype(np.int32)


def _in_attn_body(h_ref, *refs):
    n_pieces = ATT_COLS // IN_ATTN_PIECE
    w_refs, outs, (res_ref, mid_ref) = refs[:n_pieces], refs[n_pieces:-2], refs[-2:]
    (_, d1), (_, d4), (_, d16) = DILATED_CONFIGS
    step = d16 // d4
    assert d1 == 1 and step == d4
    tm = h_ref.shape[0]
    piece = IN_ATTN_PIECE
    blocks = piece // LANES
    h = h_ref[...]
    for p in range(ATT_COLS // piece):
        seg, off = divmod(p * piece, ATT_W)
        o1_ref, o4_ref, o16_ref = outs[seg], outs[3 + seg], outs[6 + seg]
        val = _dot(h, w_refs[p][...])
        if seg == 0:
            val = val * (ATT_HDIM ** -0.5 * LOG2E)
        o1_ref[:, off:off + piece] = val.astype(BF16)
        for cb in range(blocks):
            slab = p * blocks + cb
            res_ref[slab] = val[:, cb * LANES:(cb + 1) * LANES]
            for r4 in range(d4):
                part = res_ref.at[slab][pl.ds(r4, tm // d4, stride=d4), :]
                lo = r4 * ATT_W + off + cb * LANES
                o4_ref[:, lo:lo + LANES] = part.astype(BF16)
                mid_ref[slab * d4 + r4] = part
                for m in range(step):
                    lo = (r4 + d4 * m) * ATT_W + off + cb * LANES
                    rows = pl.ds(m, tm // d16, stride=step)
                    o16_ref[:, lo:lo + LANES] = mid_ref.at[slab * d4 + r4][rows, :].astype(BF16)


def _in_attn(h, w, layer, col0, tm, name):
    t, dm = h.shape
    dilations = [d for _, d in DILATED_CONFIGS]
    piece = IN_ATTN_PIECE
    n_pieces = ATT_COLS // piece
    w_specs = [pl.BlockSpec((None, dm, piece), lambda i, p=p: (layer, 0, col0 // piece + p),
                            pipeline_mode=pl.Buffered(1)) for p in range(n_pieces)]
    out_shape, out_specs = [], []
    for d in dilations:
        for _ in range(3):
            out_shape.append(jax.ShapeDtypeStruct((t // d, d * ATT_W), BF16))
            out_specs.append(pl.BlockSpec((tm // d, d * ATT_W), lambda i: (i, 0)))
    outs = pl.pallas_call(
        _in_attn_body,
        grid=(t // tm,),
        in_specs=[pl.BlockSpec((tm, dm), lambda i: (i, 0))] + w_specs,
        out_specs=out_specs,
        out_shape=out_shape,
        scratch_shapes=[pltpu.VMEM((ATT_COLS // LANES, tm, LANES), F32),
                        pltpu.VMEM((ATT_COLS // LANES * dilations[1], tm // dilations[1], LANES), F32)],
        compiler_params=_params(("arbitrary",), 56),
        name=name,
    )(h, *([w] * n_pieces))
    return [tuple(outs[3 * di:3 * di + 3]) for di in range(len(dilations))]


def _attn_body(rb_ref, idx_ref, q_ref, kc_ref, kp_ref, vc_ref, vp_ref, o_ref, st_ref, bias_ref, *, n_sub):
    blk = ATT_BLK
    heads = range(ATT_HEADS)

    @pl.when((pl.program_id(0) == 0) & (pl.program_id(1) == 0) & (pl.program_id(2) == 0))
    def _():
        idx = idx_ref[...]
        in_prev = lax.broadcasted_iota(jnp.int32, idx.shape, 1) < blk
        for h in heads:
            acc = jnp.full(idx.shape, NEG, F32)
            for n in range(N_BUCKETS):
                acc = jnp.where(idx == n, rb_ref[n, h] * LOG2E, acc)
            bias_ref[0, h] = acc
            bias_ref[1, h] = jnp.where(in_prev, NEG, acc)

    sel0 = jnp.where(pl.program_id(2) == 0, 1, 0)
    hs = [slice(h * ATT_HDIM, (h + 1) * ATT_HDIM) for h in heads]
    lane = lax.broadcasted_iota(jnp.int32, (blk, LANES), 1)

    def qk(t):
        rows = slice(t * blk, (t + 1) * blk)
        out = []
        for h in heads:
            q = q_ref[rows, hs[h]]
            if t == 0:
                out.append((_dot_nt(q, kp_ref[:, hs[h]]) + bias_ref[sel0, h, :, 0:blk],
                            _dot_nt(q, kc_ref[rows, hs[h]]) + bias_ref[sel0, h, :, blk:2 * blk]))
            else:
                l2 = _dot_nt(q, kc_ref[(t - 1) * blk:(t + 1) * blk, hs[h]]) + bias_ref[0, h]
                out.append((l2[:, 0:blk], l2[:, blk:2 * blk]))
        return out

    ones = jnp.ones((blk, ATT_HDIM), BF16)

    def softmax(t, logits):
        probs = []
        for h in heads:
            lp, lc = logits[h]
            m = jnp.max(jnp.maximum(lp, lc), axis=-1, keepdims=True)
            probs.append((jnp.exp2(lp - m).astype(BF16), jnp.exp2(lc - m).astype(BF16), m))
        return probs

    def pv(t, probs):
        rows = slice(t * blk, (t + 1) * blk)
        stats = jnp.zeros((blk, LANES), F32)
        for h in heads:
            pp, pc, m = probs[h]
            vprev = vp_ref[:, hs[h]] if t == 0 else vc_ref[(t - 1) * blk:t * blk, hs[h]]
            od = (_dot(pp, jnp.concatenate([vprev, ones], axis=1))
                  + _dot(pc, jnp.concatenate([vc_ref[rows, hs[h]], ones], axis=1)))
            den = od[:, ATT_HDIM:]
            o_ref[rows, hs[h]] = (od[:, 0:ATT_HDIM] / den).astype(o_ref.dtype)
            stats = jnp.where(lane == h, m, stats)
            stats = jnp.where(lane == ATT_HEADS + h, den, stats)
        st_ref[rows, :] = stats

    logits = qk(0)
    for t in range(n_sub):
        probs = softmax(t, logits)
        if t + 1 < n_sub:
            logits = qk(t + 1)
        pv(t, probs)


def _attn_branch(q, k, v, rel_bias, bsz, dilation, n_sub, name):
    rows = q.shape[0]
    l = rows // bsz
    blk = ATT_BLK
    lb = n_sub * blk
    nb = l // lb
    q, k, v = (a.reshape(bsz, l, dilation * ATT_W) for a in (q, k, v))
    idx = jnp.asarray(_bucket_table(dilation))
    cur = pl.BlockSpec((None, lb, ATT_W), lambda b, r, i: (b, i, r))
    prev = pl.BlockSpec((None, blk, ATT_W), lambda b, r, i: (b, jnp.maximum(i * n_sub - 1, 0), r))
    o, st = pl.pallas_call(
        functools.partial(_attn_body, n_sub=n_sub),
        grid=(bsz, dilation, nb),
        in_specs=[
            pl.BlockSpec(memory_space=pltpu.SMEM),
            pl.BlockSpec((blk, 2 * blk), lambda b, r, i: (0, 0)),
            cur, cur, prev, cur, prev,
        ],
        out_specs=[
            pl.BlockSpec((None, lb, ATT_W), lambda b, r, i: (b, i, r)),
            pl.BlockSpec((None, lb, LANES), lambda b, r, i: (b, i, r)),
        ],
        out_shape=[
            jax.ShapeDtypeStruct((bsz, l, dilation * ATT_W), BF16),
            jax.ShapeDtypeStruct((bsz, l, dilation * LANES), F32),
        ],
        scratch_shapes=[pltpu.VMEM((2, ATT_HEADS, blk, 2 * blk), F32)],
        compiler_params=_params(("arbitrary", "arbitrary", "arbitrary"), 32),
        name=name,
    )(rel_bias, idx, q, k, k, v, v)
    return o.reshape(rows, dilation * ATT_W), st.reshape(rows, dilation * LANES)


def _merge_heads(o_refs, s_refs, scratch, tm):
    scratch = list(scratch)
    heads, stats = [], []
    for (_, d), o_ref, s_ref in zip(DILATED_CONFIGS, o_refs, s_refs):
        if d == 1:
            heads.append(lambda h, o_ref=o_ref: o_ref[:, h * ATT_HDIM:(h + 1) * ATT_HDIM].astype(F32))
            stats.append(s_ref[...])
            continue
        po_ref, ps_ref = scratch.pop(0), scratch.pop(0)
        hop = MERGE_STRIDE
        if d > hop:
            mo_ref, ms_ref = scratch.pop(0), scratch.pop(0)
        for r in range(d):
            blocks = [o_ref[:, r * ATT_W + h * ATT_HDIM:r * ATT_W + (h + 1) * ATT_HDIM].astype(F32)
                      for h in range(ATT_HEADS)] + [s_ref[:, r * LANES:(r + 1) * LANES]]
            if d <= hop:
                rows = pl.ds(r, tm // d, stride=d)
                for h in range(ATT_HEADS):
                    po_ref.at[h][rows, :] = blocks[h]
                ps_ref[rows, :] = blocks[-1]
            else:
                r_lo, m = r % hop, r // hop
                rows = pl.ds(m, tm // d, stride=d // hop)
                for h in range(ATT_HEADS):
                    mo_ref.at[r_lo * ATT_HEADS + h][rows, :] = blocks[h]
                ms_ref.at[r_lo][rows, :] = blocks[-1]
        if d > hop:
            for r_lo in range(hop):
                rows = pl.ds(r_lo, tm // hop, stride=hop)
                for h in range(ATT_HEADS):
                    po_ref.at[h][rows, :] = mo_ref[r_lo * ATT_HEADS + h]
                ps_ref[rows, :] = ms_ref[r_lo]
        heads.append(lambda h, po_ref=po_ref: po_ref[h])
        stats.append(ps_ref[...])
    mx = functools.reduce(jnp.maximum, stats)
    ws = [pltpu.roll(s, LANES - ATT_HEADS, axis=1) * jnp.exp2(s - mx) for s in stats]
    inv = 1.0 / functools.reduce(lambda a, b: a + b, ws)
    ws = [w * inv for w in ws]

    def head(h):
        acc = ws[0][:, h:h + 1] * heads[0](h)
        for w, get in zip(ws[1:], heads[1:]):
            acc = acc + w[:, h:h + 1] * get(h)
        return acc

    return head


def _outproj_body(x_ref, a_ref, b_ref, *refs):
    n = len(DILATED_CONFIGS)
    o_refs, s_refs = refs[:n], refs[n:2 * n]
    w_ref, g_ref, o_ref, u_ref = refs[2 * n:2 * n + 4]
    tm = x_ref.shape[0]
    acc = x_ref[...]
    acc = acc + _dot(a_ref[...], w_ref[0:HG_W, :])
    acc = acc + _dot(b_ref[...], w_ref[HG_W:HG_W + RET_W, :])
    head = _merge_heads(o_refs, s_refs, refs[2 * n + 4:], tm)
    for h in range(0, ATT_HEADS, 2):
        c2 = jnp.concatenate([head(h), head(h + 1)], axis=1).astype(BF16)
        r0 = HG_W + RET_W + h * ATT_HDIM
        acc = acc + _dot(c2, w_ref[r0:r0 + 2 * ATT_HDIM, :])
    o_ref[...] = acc
    u_ref[...] = (_rms(acc) * g_ref[...]).astype(u_ref.dtype)


def _outproj(x, a, b, outs, stats, w, layer, g, tm, name):
    t, d = x.shape
    row = lambda width: pl.BlockSpec((tm, width), lambda i: (i, 0))
    in_specs, scratch = [row(d), row(HG_W), row(RET_W)], []
    for width in (ATT_W, LANES):
        for _, dil in DILATED_CONFIGS:
            in_specs.append(pl.BlockSpec((tm // dil, dil * width), lambda i: (i, 0)))
    for _, dil in DILATED_CONFIGS:
        if dil > 1:
            scratch += [pltpu.VMEM((ATT_HEADS, tm, ATT_HDIM), F32), pltpu.VMEM((tm, LANES), F32)]
        if dil > MERGE_STRIDE:
            scratch += [pltpu.VMEM((MERGE_STRIDE * ATT_HEADS, tm // MERGE_STRIDE, ATT_HDIM), F32),
                        pltpu.VMEM((MERGE_STRIDE, tm // MERGE_STRIDE, LANES), F32)]
    in_specs += [pl.BlockSpec((None,) + w.shape[1:], lambda i: (layer, 0, 0), pipeline_mode=pl.Buffered(1)),
                 pl.BlockSpec((1, d), lambda i: (0, 0))]
    return pl.pallas_call(
        _outproj_body,
        grid=(t // tm,),
        in_specs=in_specs,
        out_specs=[row(d), row(d)],
        out_shape=[jax.ShapeDtypeStruct((t, d), F32), jax.ShapeDtypeStruct((t, d), BF16)],
        scratch_shapes=scratch,
        compiler_params=_params(("arbitrary",), 56),
        name=name,
    )(x, a, b, *outs, *stats, w, g.reshape(1, d))


def _ffn1_body(u_ref, wg_ref, wu_ref, cw_ref, cb_ref, o_ref, carry_ref, *, tiles_per_seq):
    i = pl.program_id(0)
    j = pl.program_id(1)
    tm = u_ref.shape[0]
    u = u_ref[...]
    gp = _dot(u, wg_ref[...])
    up = _dot(u, wu_ref[...])
    w0 = cw_ref[0:1, :]
    w1 = cw_ref[1:2, :]
    w2 = cw_ref[2:3, :]
    cb = cb_ref[...]

    gate = w2 * gp + w1 * pltpu.roll(gp, 1, axis=0) + w0 * pltpu.roll(gp, 2, axis=0) + cb
    o_ref[...] = (_silu(gate) * up).astype(o_ref.dtype)

    prev = jnp.where(i % tiles_per_seq == 0, 0.0, carry_ref[j])
    carry_ref[j] = gp[tm - SUBLANES:, :]
    top = gp[0:SUBLANES, :]
    r = lax.broadcasted_iota(jnp.int32, top.shape, 0)
    p1 = prev[SUBLANES - 1:SUBLANES, :]
    p2 = prev[SUBLANES - 2:SUBLANES - 1, :]
    t1 = jnp.where(r == 0, p1, pltpu.roll(top, 1, axis=0))
    t2 = jnp.where(r == 0, p2, jnp.where(r == 1, p1, pltpu.roll(top, 2, axis=0)))
    gate_top = w2 * top + w1 * t1 + w0 * t2 + cb
    o_ref[0:SUBLANES, :] = (_silu(gate_top) * up[0:SUBLANES, :]).astype(o_ref.dtype)


def _ffn1(u, wg, wu, cw, cb, layer, seq_len, tm, tn, name):
    t, d = u.shape
    n = wg.shape[2]
    nj = n // tn
    wspec = lambda rows: pl.BlockSpec((None, rows, tn), lambda i, j: (layer, 0, j))
    return pl.pallas_call(
        functools.partial(_ffn1_body, tiles_per_seq=seq_len // tm),
        grid=(t // tm, nj),
        in_specs=[pl.BlockSpec((tm, d), lambda i, j: (i, 0)), wspec(d), wspec(d), wspec(cw.shape[1]), wspec(1)],
        out_specs=pl.BlockSpec((tm, tn), lambda i, j: (i, j)),
        out_shape=jax.ShapeDtypeStruct((t, n), BF16),
        scratch_shapes=[pltpu.VMEM((nj, SUBLANES, tn), F32)],
        compiler_params=_params(("arbitrary", "arbitrary"), 48),
        name=name,
    )(u, wg, wu, cw, cb)


def _ffn2_body(x_ref, h_ref, w_ref, g_ref, *out_refs, last_layer):
    y = x_ref[...] + _dot(h_ref[...], w_ref[...])
    normed = _rms(y) * g_ref[...]
    if last_layer:
        out_refs[0][...] = normed
    else:
        out_refs[0][...] = y
        out_refs[1][...] = normed.astype(out_refs[1].dtype)


def _ffn2(x, h, w, layer, g, last_layer, tm, name):
    t, d = x.shape
    kdim = h.shape[1]
    row = pl.BlockSpec((tm, d), lambda i: (i, 0))
    out_shape = [jax.ShapeDtypeStruct((t, d), F32)]
    if not last_layer:
        out_shape.append(jax.ShapeDtypeStruct((t, d), BF16))
    return pl.pallas_call(
        functools.partial(_ffn2_body, last_layer=last_layer),
        grid=(t // tm,),
        in_specs=[
            row,
            pl.BlockSpec((tm, kdim), lambda i: (i, 0)),
            pl.BlockSpec((None, kdim, d), lambda i: (layer, 0, 0), pipeline_mode=pl.Buffered(1)),
            pl.BlockSpec((1, d), lambda i: (0, 0)),
        ],
        out_specs=[row] * len(out_shape),
        out_shape=out_shape,
        compiler_params=_params(("arbitrary",), 60),
        name=name,
    )(x, h, w, g.reshape(1, d))


def _rope_tables(s):
    inv_freq = ROPE_BASE ** (-jnp.arange(0, RET_KDIM, 2, dtype=F32) / RET_KDIM)
    ang = jnp.arange(s, dtype=F32)[:, None] * inv_freq[None, :]
    cos, sin = jnp.cos(ang), jnp.sin(ang)
    cos_t = jnp.tile(jnp.concatenate([cos, cos], axis=1), (1, RET_HEADS))
    sin_t = jnp.tile(jnp.concatenate([-sin, sin], axis=1), (1, RET_HEADS))
    return cos_t, sin_t


def kernel(x, norm_mix, w_in, hg_lower_bound, w_out, norm_ffn, w_gate, conv_w, conv_b,
           w_up, w_down, rel_bias, norm_final):
    bsz, s, d = x.shape
    t = bsz * s
    pad = D_FF_PAD - D_FF
    cos_t, sin_t = _rope_tables(s)
    x2 = x.reshape(t, d)
    w_in_bf = w_in.astype(BF16)
    w_out_bf = w_out.astype(BF16)
    wg = jnp.concatenate([w_gate.astype(BF16), jnp.zeros((DEPTH, d, pad), BF16)], axis=2)
    wu = jnp.concatenate([w_up.astype(BF16), jnp.zeros((DEPTH, d, pad), BF16)], axis=2)
    wd = jnp.concatenate([w_down.astype(BF16), jnp.zeros((DEPTH, pad, d), BF16)], axis=1)
    cw = jnp.pad(conv_w, ((0, 0), (0, 0), (0, pad)))
    cb = jnp.pad(conv_b, ((0, 0), (0, pad))).reshape(DEPTH, 1, D_FF_PAD)
    h = _norm(x2, norm_mix[0], 1024, "norm_in")
    for l in range(DEPTH):
        last = l == DEPTH - 1
        hg = _matmul(h, w_in_bf, l, 0, HG_COLS, F32, 1024, 1024, f"in_hgrn_l{l}")
        rt = _matmul(h, w_in_bf, l, HG_COLS, RET_COLS, BF16, 2048, 512, f"in_ret_l{l}")
        qkv = _in_attn(h, w_in_bf, l, HG_COLS + RET_COLS, 512, f"in_attn_l{l}")

        a = _hgrn(hg.reshape(bsz, s, HG_COLS), hg_lower_bound, l, block=1024, chunk=32, chunks_per_iter=4)
        b = _retention(rt.reshape(bsz, s, RET_COLS), cos_t, sin_t, block=1024, chunk=128, name=f"retention_l{l}")
        outs, stats = zip(*[_attn_branch(*qkv[di], rel_bias, bsz, dil, 8, f"attn_d{dil}_l{l}")
                            for di, (_, dil) in enumerate(DILATED_CONFIGS)])
        x2, u = _outproj(x2, a.reshape(t, HG_W), b.reshape(t, RET_W), outs, stats, w_out_bf, l,
                         norm_ffn[l], 512, f"out_proj_l{l}")
        hmid = _ffn1(u, wg, wu, cw, cb, l, s, 1024, 512, f"ffn_gate_up_l{l}")
        res = _ffn2(x2, hmid, wd, l, norm_final if last else norm_mix[l + 1], last, 512, f"ffn_down_l{l}")
        if last:
            x2 = res[0]
        else:
            x2, h = res
    return x2.reshape(bsz, s, d)
```

```python
import functools
import math

import numpy as np
import jax
import jax.numpy as jnp
from jax import lax
from jax.experimental import pallas as pl
from jax.experimental.pallas import tpu as pltpu

F32 = jnp.float32
BF16 = jnp.bfloat16

D_MODEL = 2048
DEPTH = 2
HG_HEADS = 4
HG_KDIM = 128
HG_VDIM = 128
RET_HEADS = 4
RET_KDIM = 64
RET_VDIM = 128
ROPE_BASE = 10000.0
ATT_HEADS = 8
ATT_HDIM = 128
DILATED_CONFIGS = ((128, 1), (512, 4), (2048, 16))
N_BUCKETS = 32
MAX_DISTANCE = 2048
D_FF = 5504
EPS = 1e-6

HG_QK = HG_HEADS * HG_KDIM
HG_W = HG_HEADS * HG_VDIM
RET_QK = RET_HEADS * RET_KDIM
RET_W = RET_HEADS * RET_VDIM
ATT_W = ATT_HEADS * ATT_HDIM
HG_COLS = 2 * HG_QK + 2 * HG_W
RET_COLS = 2 * RET_QK + 2 * RET_W
ATT_COLS = 3 * ATT_W

LANES = 128
SUBLANES = 8
D_FF_PAD = 5632
ATT_BLK = 128
IN_ATTN_PIECE = 512
MERGE_STRIDE = 4

TILE_IN_HGRN = (1024, 1024)
TILE_IN_RET = (2048, 512)
TM_IN_ATTN = 512
HGRN_BLOCKING = (1024, 32, 4)
RET_BLOCKING = (1024, 128)
ATTN_SUBTILES = 8
TM_OUT_PROJ = 512
TILE_FFN_UP = (1024, 512)
TM_FFN_DOWN = 512
CAST_TILE = 256
NEG = -1e30
LOG2E = math.log2(math.e)
MIB = 1024 * 1024


def _dot(a, b):
    return jnp.dot(a, b, preferred_element_type=F32)


def _dot_nt(a, b):
    return lax.dot_general(a, b, (((1,), (1,)), ((), ())), preferred_element_type=F32)


def _dot_tn(a, b):
    return lax.dot_general(a, b, (((0,), (0,)), ((), ())), preferred_element_type=F32)


def _params(semantics, vmem_mib):
    return pltpu.CompilerParams(dimension_semantics=semantics, vmem_limit_bytes=vmem_mib * MIB)


def _rms(x):
    return x * lax.rsqrt(jnp.mean(x * x, axis=-1, keepdims=True) + EPS)


def _silu(x):
    return x * jax.nn.sigmoid(x)


def _norm_matmul_body(x_ref, g_ref, w_ref, o_ref, h_ref):
    @pl.when(pl.program_id(1) == 0)
    def _():
        h_ref[...] = (_rms(x_ref[...]) * g_ref[...]).astype(h_ref.dtype)

    o_ref[...] = _dot(h_ref[...], w_ref[...]).astype(o_ref.dtype)


def _norm_matmul(x, g, w, layer, col0, n, out_dtype, tm, tn, name):
    t, d = x.shape
    j0 = col0 // tn
    return pl.pallas_call(
        _norm_matmul_body,
        grid=(t // tm, n // tn),
        in_specs=[pl.BlockSpec((tm, d), lambda i, j: (i, 0)),
                  pl.BlockSpec((1, d), lambda i, j: (0, 0)),
                  pl.BlockSpec((None, d, tn), lambda i, j: (layer, 0, j + j0))],
        out_specs=[pl.BlockSpec((tm, tn), lambda i, j: (i, j)), pl.BlockSpec((tm, d), lambda i, j: (i, 0))],
        out_shape=[jax.ShapeDtypeStruct((t, n), out_dtype), jax.ShapeDtypeStruct((t, d), BF16)],
        compiler_params=_params(("arbitrary", "arbitrary"), 48),
        name=name,
    )(x, g.reshape(1, d), w)


def _matmul_body(h_ref, w_ref, o_ref):
    o_ref[...] = _dot(h_ref[...], w_ref[...]).astype(o_ref.dtype)


def _matmul(h, w, layer, col0, n, out_dtype, tm, tn, name):
    t, d = h.shape
    j0 = col0 // tn
    return pl.pallas_call(
        _matmul_body,
        grid=(t // tm, n // tn),
        in_specs=[pl.BlockSpec((tm, d), lambda i, j: (i, 0)),
                  pl.BlockSpec((None, d, tn), lambda i, j: (layer, 0, j + j0))],
        out_specs=pl.BlockSpec((tm, tn), lambda i, j: (i, j)),
        out_shape=jax.ShapeDtypeStruct((t, n), out_dtype),
        compiler_params=_params(("arbitrary", "arbitrary"), 48),
        name=name,
    )(h, w)


def _hgrn_body(q_ref, f_ref, i_ref, g_ref, lbp_ref, o_ref, st_ref, b_ref, k_ref, *,
               layer, chunk, n_chunks, chunks_per_iter):
    c = chunk
    n_groups = c // SUBLANES

    @pl.when(pl.program_id(1) == 0)
    def _():
        st_ref[...] = jnp.zeros_like(st_ref)

    p = lbp_ref[...]
    e = jnp.exp(p - jnp.max(p, axis=0, keepdims=True))
    sm = e / jnp.sum(e, axis=0, keepdims=True)
    lb_all = jnp.zeros((1, HG_QK), F32)
    for m in range(1, layer + 1):
        lb_all = lb_all + sm[m:m + 1, :]

    tri = lax.broadcasted_iota(jnp.int32, (c, c), 0) >= lax.broadcasted_iota(jnp.int32, (c, c), 1)
    tri_bf = jnp.where(tri, 1.0, 0.0).astype(BF16)
    col = lax.broadcasted_iota(jnp.int32, (SUBLANES, c), 1)

    def stage_gates(rows, h):
        hs = slice(h * HG_KDIM, (h + 1) * HG_KDIM)
        lb = lb_all[:, hs]
        oml = 1.0 - lb
        xf = f_ref[rows, hs]
        sig = jax.nn.sigmoid(xf)
        log2f = jnp.log2(lb + oml * sig)
        kk = oml * (1.0 - sig)
        hi = log2f.astype(BF16)
        r1 = log2f - hi.astype(F32)
        mid = r1.astype(BF16)
        lo = (r1 - mid.astype(F32)).astype(BF16)
        cs = _dot(tri_bf, jnp.concatenate([hi, mid, lo], axis=1))
        b = cs[:, 0:LANES] + cs[:, LANES:2 * LANES] + cs[:, 2 * LANES:3 * LANES]
        return b, kk, _silu(q_ref[rows, hs]), i_ref[rows, hs].astype(BF16)

    def stage_cross(slot, b, kk, qs):
        out = []
        for g in range(1, n_groups):
            r0 = g * SUBLANES
            ref = b_ref[slot, pl.ds(r0 - 1, 1), :]
            qt = (qs[r0:r0 + SUBLANES, :] * jnp.exp2(b[r0:r0 + SUBLANES, :] - ref)).astype(BF16)
            kt = jnp.concatenate([kk[0:r0, :] * jnp.exp2(ref - b[0:r0, :]), jnp.zeros((c - r0, LANES), F32)], axis=0)
            out.append(_dot_nt(qt, kt.astype(BF16)))
        return out

    def stage_scores(slot, b, qs, cross):
        groups = []
        for g in range(n_groups):
            r0 = g * SUBLANES
            qg = qs[r0:r0 + SUBLANES, :]
            bg = b[r0:r0 + SUBLANES, :]
            ag = cross[g - 1] if g else jnp.zeros((SUBLANES, c), F32)
            for s in range(r0, r0 + SUBLANES):
                bs = b_ref[slot, pl.ds(s, 1), :]
                ks = k_ref[slot, pl.ds(s, 1), :]
                rs = jnp.sum(qg * jnp.exp2(bg - bs) * ks, axis=-1, keepdims=True)
                ag = jnp.where(col == s, rs, ag)
            groups.append(ag)
        return jnp.where(tri, jnp.concatenate(groups, axis=0), 0.0).astype(BF16)

    def one_iter(it, carry):
        streams = [(u, h) for u in range(chunks_per_iter) for h in range(HG_HEADS)]
        rows = [pl.ds(pl.multiple_of((it * chunks_per_iter + u) * c, c), c) for u in range(chunks_per_iter)]
        gates = [stage_gates(rows[u], h) for u, h in streams]
        for slot, (b, kk, _, _) in enumerate(gates):
            b_ref[slot] = b
            k_ref[slot] = kk
        cross = [stage_cross(slot, b, kk, qs) for slot, (b, kk, qs, _) in enumerate(gates)]
        scores = [stage_scores(slot, b, qs, cross[slot]) for slot, (b, _, qs, _) in enumerate(gates)]
        st = [st_ref[h] for h in range(HG_HEADS)]
        for slot, (u, h) in enumerate(streams):
            b, kk, qs, v = gates[slot]
            hs = slice(h * HG_KDIM, (h + 1) * HG_KDIM)
            b_last = b_ref[slot, pl.ds(c - 1, 1), :]
            o = _dot(scores[slot], v) + _dot_nt((qs * jnp.exp2(b)).astype(BF16), st[h].astype(BF16))
            kt = (kk * jnp.exp2(b_last - b)).astype(BF16)
            st[h] = st[h] * jnp.exp2(b_last) + _dot_tn(v, kt)
            o_ref[rows[u], hs] = (_rms(o) * _silu(g_ref[rows[u], hs])).astype(o_ref.dtype)
        for h in range(HG_HEADS):
            st_ref[h] = st[h]
        return carry

    lax.fori_loop(0, n_chunks // chunks_per_iter, one_iter, 0)


def _hgrn(hg, lb_param, layer, block, chunk, chunks_per_iter=1):
    bsz, s, _ = hg.shape
    slots = HG_HEADS * chunks_per_iter
    spec = lambda seg: pl.BlockSpec((None, block, HG_QK), lambda b, i, seg=seg: (b, i, seg))
    return pl.pallas_call(
        functools.partial(_hgrn_body, layer=layer, chunk=chunk, n_chunks=block // chunk,
                          chunks_per_iter=chunks_per_iter),
        grid=(bsz, s // block),
        in_specs=[spec(0), spec(1), spec(2), spec(3),
                  pl.BlockSpec((DEPTH, HG_QK), lambda b, i: (0, 0))],
        out_specs=pl.BlockSpec((None, block, HG_W), lambda b, i: (b, i, 0)),
        out_shape=jax.ShapeDtypeStruct((bsz, s, HG_W), BF16),
        scratch_shapes=[pltpu.VMEM((HG_HEADS, HG_VDIM, HG_KDIM), F32),
                        pltpu.VMEM((slots, chunk, LANES), F32),
                        pltpu.VMEM((slots, chunk, LANES), F32)],
        compiler_params=_params(("arbitrary", "arbitrary"), 32),
        name=f"hgrn2_l{layer}",
    )(hg, hg, hg, hg, lb_param)


def _ret_log_gamma(h):
    return math.log(1.0 - 2.0 ** (-5.0 - h))


def _ret_body(q_ref, k_ref, v_ref, g_ref, cos_ref, sin_ref, o_ref,
              st_ref, dm_ref, fs_ref, te_ref, bd_ref, *, chunk, n_chunks, chunks_per_iter):
    c = chunk
    qk = RET_QK
    lane_head = lax.broadcasted_iota(jnp.int32, (c, qk), 1) // RET_KDIM

    def lane_log_gamma(head_idx):
        lg = jnp.full(head_idx.shape, _ret_log_gamma(0), F32)
        for h in range(1, RET_HEADS):
            lg = jnp.where(head_idx == h, _ret_log_gamma(h), lg)
        return lg

    @pl.when((pl.program_id(0) == 0) & (pl.program_id(1) == 0))
    def _():
        t = lax.broadcasted_iota(jnp.int32, (c, qk), 0).astype(F32)
        lg = lane_log_gamma(lane_head)
        fs_ref[...] = jnp.exp(lg * (t + 1.0))
        te_ref[...] = jnp.exp(lg * (c - 1.0 - t))
        dist = (lax.broadcasted_iota(jnp.int32, (c, c), 0) - lax.broadcasted_iota(jnp.int32, (c, c), 1))
        for h in range(RET_HEADS):
            dm_ref[h] = jnp.where(dist >= 0, jnp.exp(_ret_log_gamma(h) * jnp.maximum(dist, 0).astype(F32)), 0.0)
        rh = lax.broadcasted_iota(jnp.int32, (qk, RET_W), 0) // RET_KDIM
        ch = lax.broadcasted_iota(jnp.int32, (qk, RET_W), 1) // RET_VDIM
        bd_ref[...] = jnp.where(rh == ch, jnp.exp(lane_log_gamma(rh) * float(c)), 0.0)

    @pl.when(pl.program_id(1) == 0)
    def _():
        st_ref[...] = jnp.zeros_like(st_ref)

    in_first_half = (lax.broadcasted_iota(jnp.int32, (c, LANES), 1) % RET_KDIM) < (RET_KDIM // 2)

    def rope(t, cos, sin):
        parts = []
        for p in range(qk // LANES):
            th = t[:, p * LANES:(p + 1) * LANES]
            back = pltpu.roll(th, RET_KDIM // 2, axis=1)
            fwd = pltpu.roll(th, LANES - RET_KDIM // 2, axis=1)
            parts.append(jnp.where(in_first_half, fwd, back))
        return t * cos + jnp.concatenate(parts, axis=1) * sin

    heads = range(RET_HEADS)
    hs = [slice(h * RET_VDIM, (h + 1) * RET_VDIM) for h in heads]

    def one_iter(it, carry):
        units = range(chunks_per_iter)
        rows = [pl.ds(pl.multiple_of((it * chunks_per_iter + u) * c, c), c) for u in units]
        qr, kr, v = [], [], []
        for u in units:
            cos = cos_ref[rows[u], :]
            sin = sin_ref[rows[u], :]
            qr.append(rope(q_ref[rows[u], :].astype(F32), cos, sin))
            kr.append(rope(k_ref[rows[u], :].astype(F32), cos, sin) * (RET_KDIM ** -0.5))
            v.append(v_ref[rows[u], :])
        kr_bf = [k.astype(BF16) for k in kr]
        scores = [[_dot_nt(jnp.where(lane_head == h, qr[u], 0.0).astype(BF16), kr_bf[u]) for h in heads]
                  for u in units]
        upd = [_dot_tn((kr[u] * te_ref[...]).astype(BF16), v[u]) for u in units]
        bd = bd_ref[...]
        st = st_ref[...]
        inter = []
        for u in units:
            inter.append(_dot((qr[u] * fs_ref[...]).astype(BF16), st.astype(BF16)))
            st = st * bd + jnp.where(bd > 0.0, upd[u], 0.0)
        st_ref[...] = st
        for u in units:
            g = g_ref[rows[u], :].astype(F32)
            for h in heads:
                p = (scores[u][h] * dm_ref[h]).astype(BF16)
                oh = _dot(p, v[u][:, hs[h]]) + inter[u][:, hs[h]]
                o_ref[rows[u], hs[h]] = (_rms(oh) * _silu(g[:, hs[h]])).astype(o_ref.dtype)
        return carry

    lax.fori_loop(0, n_chunks // chunks_per_iter, one_iter, 0)


def _retention(rt, cos_t, sin_t, block, chunk, name, chunks_per_iter=2):
    bsz, s, _ = rt.shape
    return pl.pallas_call(
        functools.partial(_ret_body, chunk=chunk, n_chunks=block // chunk, chunks_per_iter=chunks_per_iter),
        grid=(bsz, s // block),
        in_specs=[
            pl.BlockSpec((None, block, RET_QK), lambda b, i: (b, i, 0)),
            pl.BlockSpec((None, block, RET_QK), lambda b, i: (b, i, 1)),
            pl.BlockSpec((None, block, RET_W), lambda b, i: (b, i, 1)),
            pl.BlockSpec((None, block, RET_W), lambda b, i: (b, i, 2)),
            pl.BlockSpec((block, RET_QK), lambda b, i: (i, 0)),
            pl.BlockSpec((block, RET_QK), lambda b, i: (i, 0)),
        ],
        out_specs=pl.BlockSpec((None, block, RET_W), lambda b, i: (b, i, 0)),
        out_shape=jax.ShapeDtypeStruct((bsz, s, RET_W), BF16),
        scratch_shapes=[
            pltpu.VMEM((RET_QK, RET_W), F32),
            pltpu.VMEM((RET_HEADS, chunk, chunk), F32),
            pltpu.VMEM((chunk, RET_QK), F32),
            pltpu.VMEM((chunk, RET_QK), F32),
            pltpu.VMEM((RET_QK, RET_W), F32),
        ],
        compiler_params=_params(("arbitrary", "arbitrary"), 32),
        name=name,
    )(rt, rt, rt, rt, cos_t, sin_t)


def _t5_bucket_np(distance):
    max_exact = N_BUCKETS // 2
    n = np.maximum(distance, 1).astype(np.float32)
    large = max_exact + (np.log(n / max_exact) / math.log(MAX_DISTANCE / max_exact)
                         * (N_BUCKETS - max_exact)).astype(np.int32)
    large = np.minimum(large, N_BUCKETS - 1)
    return np.where(distance < max_exact, distance, large)


def _bucket_table(dilation):
    a = np.arange(ATT_BLK)[:, None]
    kk = np.arange(2 * ATT_BLK)[None, :]
    j = a + ATT_BLK - kk
    valid = (j >= 0) & (j <= ATT_BLK)
    bucket = _t5_bucket_np(np.clip(j, 0, ATT_BLK) * dilation)
    return np.where(valid, bucket, -1).astype(np.int32)


def _in_attn_body(h_ref, *refs):
    n_pieces = ATT_COLS // IN_ATTN_PIECE
    w_refs, outs, (res_ref, mid_ref) = refs[:n_pieces], refs[n_pieces:-2], refs[-2:]
    (_, d1), (_, d4), (_, d16) = DILATED_CONFIGS
    step = d16 // d4
    assert d1 == 1 and step == d4
    tm = h_ref.shape[0]
    piece = IN_ATTN_PIECE
    blocks = piece // LANES
    h = h_ref[...]
    for p in range(ATT_COLS // piece):
        seg, off = divmod(p * piece, ATT_W)
        o1_ref, o4_ref, o16_ref = outs[seg], outs[3 + seg], outs[6 + seg]
        val = _dot(h, w_refs[p][...])
        if seg == 0:
            val = val * (ATT_HDIM ** -0.5 * LOG2E)
        o1_ref[:, off:off + piece] = val.astype(BF16)
        for cb in range(blocks):
            slab = p * blocks + cb
            res_ref[slab] = val[:, cb * LANES:(cb + 1) * LANES]
            for r4 in range(d4):
                part = res_ref.at[slab][pl.ds(r4, tm // d4, stride=d4), :]
                lo = r4 * ATT_W + off + cb * LANES
                o4_ref[:, lo:lo + LANES] = part.astype(BF16)
                mid_ref[slab * d4 + r4] = part
                for m in range(step):
                    lo = (r4 + d4 * m) * ATT_W + off + cb * LANES
                    rows = pl.ds(m, tm // d16, stride=step)
                    o16_ref[:, lo:lo + LANES] = mid_ref.at[slab * d4 + r4][rows, :].astype(BF16)


def _in_attn(h, w, layer, col0, tm, name):
    t, dm = h.shape
    dilations = [d for _, d in DILATED_CONFIGS]
    piece = IN_ATTN_PIECE
    n_pieces = ATT_COLS // piece
    w_specs = [pl.BlockSpec((None, dm, piece), lambda i, p=p: (layer, 0, col0 // piece + p),
                            pipeline_mode=pl.Buffered(1)) for p in range(n_pieces)]
    out_shape, out_specs = [], []
    for d in dilations:
        for _ in range(3):
            out_shape.append(jax.ShapeDtypeStruct((t // d, d * ATT_W), BF16))
            out_specs.append(pl.BlockSpec((tm // d, d * ATT_W), lambda i: (i, 0)))
    outs = pl.pallas_call(
        _in_attn_body,
        grid=(t // tm,),
        in_specs=[pl.BlockSpec((tm, dm), lambda i: (i, 0))] + w_specs,
        out_specs=out_specs,
        out_shape=out_shape,
        scratch_shapes=[pltpu.VMEM((ATT_COLS // LANES, tm, LANES), F32),
                        pltpu.VMEM((ATT_COLS // LANES * dilations[1], tm // dilations[1], LANES), F32)],
        compiler_params=_params(("arbitrary",), 56),
        name=name,
    )(h, *([w] * n_pieces))
    return [tuple(outs[3 * di:3 * di + 3]) for di in range(len(dilations))]


def _attn_body(rb_ref, idx_ref, q_ref, kc_ref, kp_ref, vc_ref, vp_ref, o_ref, st_ref, bias_ref, *, n_sub):
    blk = ATT_BLK
    heads = range(ATT_HEADS)

    @pl.when((pl.program_id(0) == 0) & (pl.program_id(1) == 0) & (pl.program_id(2) == 0))
    def _():
        idx = idx_ref[...]
        in_prev = lax.broadcasted_iota(jnp.int32, idx.shape, 1) < blk
        for h in heads:
            acc = jnp.full(idx.shape, NEG, F32)
            for n in range(N_BUCKETS):
                acc = jnp.where(idx == n, rb_ref[n, h] * LOG2E, acc)
            bias_ref[0, h] = acc
            bias_ref[1, h] = jnp.where(in_prev, NEG, acc)

    sel0 = jnp.where(pl.program_id(2) == 0, 1, 0)
    hs = [slice(h * ATT_HDIM, (h + 1) * ATT_HDIM) for h in heads]
    lane = lax.broadcasted_iota(jnp.int32, (blk, LANES), 1)

    def qk(t):
        rows = slice(t * blk, (t + 1) * blk)
        out = []
        for h in heads:
            q = q_ref[rows, hs[h]]
            if t == 0:
                out.append((_dot_nt(q, kp_ref[:, hs[h]]) + bias_ref[sel0, h, :, 0:blk],
                            _dot_nt(q, kc_ref[rows, hs[h]]) + bias_ref[sel0, h, :, blk:2 * blk]))
            else:
                l2 = _dot_nt(q, kc_ref[(t - 1) * blk:(t + 1) * blk, hs[h]]) + bias_ref[0, h]
                out.append((l2[:, 0:blk], l2[:, blk:2 * blk]))
        return out

    ones = jnp.ones((blk, ATT_HDIM), BF16)

    def softmax(t, logits):
        probs = []
        for h in heads:
            lp, lc = logits[h]
            m = jnp.max(jnp.maximum(lp, lc), axis=-1, keepdims=True)
            probs.append((jnp.exp2(lp - m).astype(BF16), jnp.exp2(lc - m).astype(BF16), m))
        return probs

    def pv(t, probs):
        rows = slice(t * blk, (t + 1) * blk)
        stats = jnp.zeros((blk, LANES), F32)
        for h in heads:
            pp, pc, m = probs[h]
            vprev = vp_ref[:, hs[h]] if t == 0 else vc_ref[(t - 1) * blk:t * blk, hs[h]]
            od = (_dot(pp, jnp.concatenate([vprev, ones], axis=1))
                  + _dot(pc, jnp.concatenate([vc_ref[rows, hs[h]], ones], axis=1)))
            den = od[:, ATT_HDIM:]
            o_ref[rows, hs[h]] = (od[:, 0:ATT_HDIM] / den).astype(o_ref.dtype)
            stats = jnp.where(lane == h, m, stats)
            stats = jnp.where(lane == ATT_HEADS + h, den, stats)
        st_ref[rows, :] = stats

    logits = qk(0)
    for t in range(n_sub):
        probs = softmax(t, logits)
        if t + 1 < n_sub:
            logits = qk(t + 1)
        pv(t, probs)


def _attn_branch(q, k, v, rel_bias, bsz, dilation, n_sub, name):
    rows = q.shape[0]
    l = rows // bsz
    blk = ATT_BLK
    lb = n_sub * blk
    nb = l // lb
    q, k, v = (a.reshape(bsz, l, dilation * ATT_W) for a in (q, k, v))
    idx = jnp.asarray(_bucket_table(dilation))
    cur = pl.BlockSpec((None, lb, ATT_W), lambda b, r, i: (b, i, r))
    prev = pl.BlockSpec((None, blk, ATT_W), lambda b, r, i: (b, jnp.maximum(i * n_sub - 1, 0), r))
    o, st = pl.pallas_call(
        functools.partial(_attn_body, n_sub=n_sub),
        grid=(bsz, dilation, nb),
        in_specs=[
            pl.BlockSpec(memory_space=pltpu.SMEM),
            pl.BlockSpec((blk, 2 * blk), lambda b, r, i: (0, 0)),
            cur, cur, prev, cur, prev,
        ],
        out_specs=[
            pl.BlockSpec((None, lb, ATT_W), lambda b, r, i: (b, i, r)),
            pl.BlockSpec((None, lb, LANES), lambda b, r, i: (b, i, r)),
        ],
        out_shape=[
            jax.ShapeDtypeStruct((bsz, l, dilation * ATT_W), BF16),
            jax.ShapeDtypeStruct((bsz, l, dilation * LANES), F32),
        ],
        scratch_shapes=[pltpu.VMEM((2, ATT_HEADS, blk, 2 * blk), F32)],
        compiler_params=_params(("arbitrary", "arbitrary", "arbitrary"), 32),
        name=name,
    )(rel_bias, idx, q, k, k, v, v)
    return o.reshape(rows, dilation * ATT_W), st.reshape(rows, dilation * LANES)


def _merge_heads(o_refs, s_refs, scratch, tm):
    scratch = list(scratch)
    heads, stats = [], []
    for (_, d), o_ref, s_ref in zip(DILATED_CONFIGS, o_refs, s_refs):
        if d == 1:
            heads.append(lambda h, o_ref=o_ref: o_ref[:, h * ATT_HDIM:(h + 1) * ATT_HDIM].astype(F32))
            stats.append(s_ref[...])
            continue
        po_ref, ps_ref = scratch.pop(0), scratch.pop(0)
        hop = MERGE_STRIDE
        if d > hop:
            mo_ref, ms_ref = scratch.pop(0), scratch.pop(0)
        for r in range(d):
            blocks = [o_ref[:, r * ATT_W + h * ATT_HDIM:r * ATT_W + (h + 1) * ATT_HDIM].astype(F32)
                      for h in range(ATT_HEADS)] + [s_ref[:, r * LANES:(r + 1) * LANES]]
            if d <= hop:
                rows = pl.ds(r, tm // d, stride=d)
                for h in range(ATT_HEADS):
                    po_ref.at[h][rows, :] = blocks[h]
                ps_ref[rows, :] = blocks[-1]
            else:
                r_lo, m = r % hop, r // hop
                rows = pl.ds(m, tm // d, stride=d // hop)
                for h in range(ATT_HEADS):
                    mo_ref.at[r_lo * ATT_HEADS + h][rows, :] = blocks[h]
                ms_ref.at[r_lo][rows, :] = blocks[-1]
        if d > hop:
            for r_lo in range(hop):
                rows = pl.ds(r_lo, tm // hop, stride=hop)
                for h in range(ATT_HEADS):
                    po_ref.at[h][rows, :] = mo_ref[r_lo * ATT_HEADS + h]
                ps_ref[rows, :] = ms_ref[r_lo]
        heads.append(lambda h, po_ref=po_ref: po_ref[h])
        stats.append(ps_ref[...])
    mx = functools.reduce(jnp.maximum, stats)
    ws = [pltpu.roll(s, LANES - ATT_HEADS, axis=1) * jnp.exp2(s - mx) for s in stats]
    inv = 1.0 / functools.reduce(lambda a, b: a + b, ws)
    ws = [w * inv for w in ws]

    def head(h):
        acc = ws[0][:, h:h + 1] * heads[0](h)
        for w, get in zip(ws[1:], heads[1:]):
            acc = acc + w[:, h:h + 1] * get(h)
        return acc

    return head


def _outproj_body(x_ref, a_ref, b_ref, *refs):
    n = len(DILATED_CONFIGS)
    o_refs, s_refs = refs[:n], refs[n:2 * n]
    w_ref, g_ref, o_ref, u_ref = refs[2 * n:2 * n + 4]
    tm = x_ref.shape[0]
    acc = x_ref[...]
    acc = acc + _dot(a_ref[...], w_ref[0:HG_W, :])
    acc = acc + _dot(b_ref[...], w_ref[HG_W:HG_W + RET_W, :])
    head = _merge_heads(o_refs, s_refs, refs[2 * n + 4:], tm)
    for h in range(0, ATT_HEADS, 2):
        c2 = jnp.concatenate([head(h), head(h + 1)], axis=1).astype(BF16)
        r0 = HG_W + RET_W + h * ATT_HDIM
        acc = acc + _dot(c2, w_ref[r0:r0 + 2 * ATT_HDIM, :])
    o_ref[...] = acc
    u_ref[...] = (_rms(acc) * g_ref[...]).astype(u_ref.dtype)


def _outproj(x, a, b, outs, stats, w, layer, g, tm, name):
    t, d = x.shape
    row = lambda width: pl.BlockSpec((tm, width), lambda i: (i, 0))
    in_specs, scratch = [row(d), row(HG_W), row(RET_W)], []
    for width in (ATT_W, LANES):
        for _, dil in DILATED_CONFIGS:
            in_specs.append(pl.BlockSpec((tm // dil, dil * width), lambda i: (i, 0)))
    for _, dil in DILATED_CONFIGS:
        if dil > 1:
            scratch += [pltpu.VMEM((ATT_HEADS, tm, ATT_HDIM), F32), pltpu.VMEM((tm, LANES), F32)]
        if dil > MERGE_STRIDE:
            scratch += [pltpu.VMEM((MERGE_STRIDE * ATT_HEADS, tm // MERGE_STRIDE, ATT_HDIM), F32),
                        pltpu.VMEM((MERGE_STRIDE, tm // MERGE_STRIDE, LANES), F32)]
    in_specs += [pl.BlockSpec((None,) + w.shape[1:], lambda i: (layer, 0, 0), pipeline_mode=pl.Buffered(1)),
                 pl.BlockSpec((1, d), lambda i: (0, 0))]
    return pl.pallas_call(
        _outproj_body,
        grid=(t // tm,),
        in_specs=in_specs,
        out_specs=[row(d), row(d)],
        out_shape=[jax.ShapeDtypeStruct((t, d), F32), jax.ShapeDtypeStruct((t, d), BF16)],
        scratch_shapes=scratch,
        compiler_params=_params(("arbitrary",), 56),
        name=name,
    )(x, a, b, *outs, *stats, w, g.reshape(1, d))


def _ffn1_body(u_ref, wg_ref, wu_ref, cw_ref, cb_ref, o_ref, carry_ref, *, tiles_per_seq):
    i = pl.program_id(0)
    j = pl.program_id(1)
    tm = u_ref.shape[0]
    u = u_ref[...]
    gp = _dot(u, wg_ref[...])
    up = _dot(u, wu_ref[...])
    w0 = cw_ref[0:1, :]
    w1 = cw_ref[1:2, :]
    w2 = cw_ref[2:3, :]
    cb = cb_ref[...]

    gate = w2 * gp + w1 * pltpu.roll(gp, 1, axis=0) + w0 * pltpu.roll(gp, 2, axis=0) + cb
    o_ref[...] = (_silu(gate) * up).astype(o_ref.dtype)

    prev = jnp.where(i % tiles_per_seq == 0, 0.0, carry_ref[j])
    carry_ref[j] = gp[tm - SUBLANES:, :]
    top = gp[0:SUBLANES, :]
    r = lax.broadcasted_iota(jnp.int32, top.shape, 0)
    p1 = prev[SUBLANES - 1:SUBLANES, :]
    p2 = prev[SUBLANES - 2:SUBLANES - 1, :]
    t1 = jnp.where(r == 0, p1, pltpu.roll(top, 1, axis=0))
    t2 = jnp.where(r == 0, p2, jnp.where(r == 1, p1, pltpu.roll(top, 2, axis=0)))
    gate_top = w2 * top + w1 * t1 + w0 * t2 + cb
    o_ref[0:SUBLANES, :] = (_silu(gate_top) * up[0:SUBLANES, :]).astype(o_ref.dtype)


def _ffn1(u, wg, wu, cw, cb, layer, seq_len, tm, tn, name):
    t, d = u.shape
    n = wg.shape[2]
    nj = n // tn
    wspec = lambda rows: pl.BlockSpec((None, rows, tn), lambda i, j: (layer, 0, j))
    return pl.pallas_call(
        functools.partial(_ffn1_body, tiles_per_seq=seq_len // tm),
        grid=(t // tm, nj),
        in_specs=[pl.BlockSpec((tm, d), lambda i, j: (i, 0)), wspec(d), wspec(d), wspec(cw.shape[1]), wspec(1)],
        out_specs=pl.BlockSpec((tm, tn), lambda i, j: (i, j)),
        out_shape=jax.ShapeDtypeStruct((t, n), BF16),
        scratch_shapes=[pltpu.VMEM((nj, SUBLANES, tn), F32)],
        compiler_params=_params(("arbitrary", "arbitrary"), 48),
        name=name,
    )(u, wg, wu, cw, cb)


def _ffn2_body(x_ref, h_ref, w_ref, g_ref, *out_refs, last_layer):
    y = x_ref[...] + _dot(h_ref[...], w_ref[...])
    normed = _rms(y) * g_ref[...]
    if last_layer:
        out_refs[0][...] = normed
    else:
        out_refs[0][...] = y
        out_refs[1][...] = normed.astype(out_refs[1].dtype)


def _ffn2(x, h, w, layer, g, last_layer, tm, name):
    t, d = x.shape
    kdim = h.shape[1]
    row = pl.BlockSpec((tm, d), lambda i: (i, 0))
    out_shape = [jax.ShapeDtypeStruct((t, d), F32)]
    if not last_layer:
        out_shape.append(jax.ShapeDtypeStruct((t, d), BF16))
    return pl.pallas_call(
        functools.partial(_ffn2_body, last_layer=last_layer),
        grid=(t // tm,),
        in_specs=[
            row,
            pl.BlockSpec((tm, kdim), lambda i: (i, 0)),
            pl.BlockSpec((None, kdim, d), lambda i: (layer, 0, 0), pipeline_mode=pl.Buffered(1)),
            pl.BlockSpec((1, d), lambda i: (0, 0)),
        ],
        out_specs=[row] * len(out_shape),
        out_shape=out_shape,
        compiler_params=_params(("arbitrary",), 60),
        name=name,
    )(x, h, w, g.reshape(1, d))


def _cast_pad_body(x_ref, o_ref, *, axis):
    if axis == 0:
        o_ref[0:D_FF, :] = x_ref[...].astype(o_ref.dtype)
        o_ref[D_FF:, :] = jnp.zeros((D_FF_PAD - D_FF, o_ref.shape[1]), o_ref.dtype)
    else:
        o_ref[:, 0:D_FF] = x_ref[...].astype(o_ref.dtype)
        o_ref[:, D_FF:] = jnp.zeros((o_ref.shape[0], D_FF_PAD - D_FF), o_ref.dtype)


def _cast_pad(w, axis, tile, name):
    depth, a, b = w.shape
    if axis == 1:
        in_block, out_block, out_dims = (None, D_FF, tile), (None, D_FF_PAD, tile), (depth, D_FF_PAD, b)
        index, grid = (lambda l, i: (l, 0, i)), (depth, b // tile)
    else:
        in_block, out_block, out_dims = (None, tile, D_FF), (None, tile, D_FF_PAD), (depth, a, D_FF_PAD)
        index, grid = (lambda l, i: (l, i, 0)), (depth, a // tile)
    return pl.pallas_call(
        functools.partial(_cast_pad_body, axis=axis - 1),
        grid=grid,
        in_specs=[pl.BlockSpec(in_block, index)],
        out_specs=pl.BlockSpec(out_block, index),
        out_shape=jax.ShapeDtypeStruct(out_dims, BF16),
        compiler_params=_params(("arbitrary", "arbitrary"), 32),
        name=name,
    )(w)


def _rope_tables(s):
    inv_freq = ROPE_BASE ** (-jnp.arange(0, RET_KDIM, 2, dtype=F32) / RET_KDIM)
    ang = jnp.arange(s, dtype=F32)[:, None] * inv_freq[None, :]
    cos, sin = jnp.cos(ang), jnp.sin(ang)
    cos_t = jnp.tile(jnp.concatenate([cos, cos], axis=1), (1, RET_HEADS))
    sin_t = jnp.tile(jnp.concatenate([-sin, sin], axis=1), (1, RET_HEADS))
    return cos_t, sin_t


def kernel(x, norm_mix, w_in, hg_lower_bound, w_out, norm_ffn, w_gate, conv_w, conv_b,
           w_up, w_down, rel_bias, norm_final):
    bsz, s, d = x.shape
    t = bsz * s
    pad = D_FF_PAD - D_FF
    cos_t, sin_t = _rope_tables(s)
    x2 = x.reshape(t, d)
    w_in_bf = w_in.astype(BF16)
    w_out_bf = w_out.astype(BF16)
    wg = _cast_pad(w_gate, 2, CAST_TILE, "cast_w_gate")
    wu = _cast_pad(w_up, 2, CAST_TILE, "cast_w_up")
    wd = _cast_pad(w_down, 1, CAST_TILE, "cast_w_down")
    cw = jnp.pad(conv_w, ((0, 0), (0, 0), (0, pad)))
    cb = jnp.pad(conv_b, ((0, 0), (0, pad))).reshape(DEPTH, 1, D_FF_PAD)
    h = None
    for l in range(DEPTH):
        last = l == DEPTH - 1
        if l == 0:
            hg, h = _norm_matmul(x2, norm_mix[0], w_in_bf, l, 0, HG_COLS, F32, *TILE_IN_HGRN, f"in_hgrn_l{l}")
        else:
            hg = _matmul(h, w_in_bf, l, 0, HG_COLS, F32, *TILE_IN_HGRN, f"in_hgrn_l{l}")
        rt = _matmul(h, w_in_bf, l, HG_COLS, RET_COLS, BF16, *TILE_IN_RET, f"in_ret_l{l}")
        qkv = _in_attn(h, w_in_bf, l, HG_COLS + RET_COLS, TM_IN_ATTN, f"in_attn_l{l}")

        a = _hgrn(hg.reshape(bsz, s, HG_COLS), hg_lower_bound, l, *HGRN_BLOCKING)
        b = _retention(rt.reshape(bsz, s, RET_COLS), cos_t, sin_t, *RET_BLOCKING, name=f"retention_l{l}")
        outs, stats = zip(*[_attn_branch(*qkv[di], rel_bias, bsz, dil, ATTN_SUBTILES, f"attn_d{dil}_l{l}")
                            for di, (_, dil) in enumerate(DILATED_CONFIGS)])
        x2, u = _outproj(x2, a.reshape(t, HG_W), b.reshape(t, RET_W), outs, stats, w_out_bf, l,
                         norm_ffn[l], TM_OUT_PROJ, f"out_proj_l{l}")
        hmid = _ffn1(u, wg, wu, cw, cb, l, s, *TILE_FFN_UP, f"ffn_gate_up_l{l}")
        res = _ffn2(x2, hmid, wd, l, norm_final if last else norm_mix[l + 1], last, TM_FFN_DOWN, f"ffn_down_l{l}")
        if last:
            x2 = res[0]
        else:
            x2, h = res
    return x2.reshape(bsz, s, d)
```

```python
import functools
import math

import numpy as np
import jax
import jax.numpy as jnp
from jax import lax
from jax.experimental import pallas as pl
from jax.experimental.pallas import tpu as pltpu

F32 = jnp.float32
BF16 = jnp.bfloat16

D_MODEL = 2048
DEPTH = 2
HG_HEADS = 4
HG_KDIM = 128
HG_VDIM = 128
RET_HEADS = 4
RET_KDIM = 64
RET_VDIM = 128
ROPE_BASE = 10000.0
ATT_HEADS = 8
ATT_HDIM = 128
DILATED_CONFIGS = ((128, 1), (512, 4), (2048, 16))
N_BUCKETS = 32
MAX_DISTANCE = 2048
D_FF = 5504
EPS = 1e-6

HG_QK = HG_HEADS * HG_KDIM
HG_W = HG_HEADS * HG_VDIM
RET_QK = RET_HEADS * RET_KDIM
RET_W = RET_HEADS * RET_VDIM
ATT_W = ATT_HEADS * ATT_HDIM
HG_COLS = 2 * HG_QK + 2 * HG_W
RET_COLS = 2 * RET_QK + 2 * RET_W
ATT_COLS = 3 * ATT_W

LANES = 128
SUBLANES = 8
D_FF_PAD = 5632
ATT_BLK = 128
IN_ATTN_PIECE = 512
MERGE_STRIDE = 4

TILE_IN_HGRN = (1024, 1024)
TM_NORM_IN = 512
TILE_IN_RET = (2048, 512)
TM_IN_ATTN = 512
HGRN_BLOCKING = (1024, 32, 4)
RET_BLOCKING = (1024, 128)
ATTN_SUBTILES = 8
TM_OUT_PROJ = 512
TILE_FFN_UP = (1024, 512)
TM_FFN_DOWN = 512
CAST_TILE = 256
NEG = -1e30
LOG2E = math.log2(math.e)
MIB = 1024 * 1024


def _dot(a, b):
    return jnp.dot(a, b, preferred_element_type=F32)


def _dot_nt(a, b):
    return lax.dot_general(a, b, (((1,), (1,)), ((), ())), preferred_element_type=F32)


def _dot_tn(a, b):
    return lax.dot_general(a, b, (((0,), (0,)), ((), ())), preferred_element_type=F32)


def _params(semantics, vmem_mib):
    return pltpu.CompilerParams(dimension_semantics=semantics, vmem_limit_bytes=vmem_mib * MIB)


def _rms(x):
    return x * lax.rsqrt(jnp.mean(x * x, axis=-1, keepdims=True) + EPS)


def _silu(x):
    return x * jax.nn.sigmoid(x)


def _norm_matmul_body(x_ref, g_ref, w_ref, o_ref, h_ref):
    half = x_ref.shape[0] // 2
    for r in range(2):
        rows = slice(r * half, (r + 1) * half)
        h = (_rms(x_ref[rows, :]) * g_ref[...]).astype(h_ref.dtype)
        h_ref[rows, :] = h
        o_ref[rows, :] = _dot(h, w_ref[...]).astype(o_ref.dtype)


def _norm_matmul(x, g, w, layer, col0, n, out_dtype, tm, name):
    t, d = x.shape
    row = lambda width: pl.BlockSpec((tm, width), lambda i: (i, 0))
    return pl.pallas_call(
        _norm_matmul_body,
        grid=(t // tm,),
        in_specs=[row(d), pl.BlockSpec((1, d), lambda i: (0, 0)),
                  pl.BlockSpec((None, d, n), lambda i: (layer, 0, col0 // n), pipeline_mode=pl.Buffered(1))],
        out_specs=[row(n), row(d)],
        out_shape=[jax.ShapeDtypeStruct((t, n), out_dtype), jax.ShapeDtypeStruct((t, d), BF16)],
        compiler_params=_params(("arbitrary",), 48),
        name=name,
    )(x, g.reshape(1, d), w)


def _matmul_body(h_ref, w_ref, o_ref):
    o_ref[...] = _dot(h_ref[...], w_ref[...]).astype(o_ref.dtype)


def _matmul(h, w, layer, col0, n, out_dtype, tm, tn, name):
    t, d = h.shape
    j0 = col0 // tn
    return pl.pallas_call(
        _matmul_body,
        grid=(t // tm, n // tn),
        in_specs=[pl.BlockSpec((tm, d), lambda i, j: (i, 0)),
                  pl.BlockSpec((None, d, tn), lambda i, j: (layer, 0, j + j0))],
        out_specs=pl.BlockSpec((tm, tn), lambda i, j: (i, j)),
        out_shape=jax.ShapeDtypeStruct((t, n), out_dtype),
        compiler_params=_params(("arbitrary", "arbitrary"), 48),
        name=name,
    )(h, w)


def _hgrn_body(q_ref, f_ref, i_ref, g_ref, lbp_ref, o_ref, st_ref, b_ref, k_ref, *,
               layer, chunk, n_chunks, chunks_per_iter):
    c = chunk
    n_groups = c // SUBLANES

    @pl.when(pl.program_id(1) == 0)
    def _():
        st_ref[...] = jnp.zeros_like(st_ref)

    p = lbp_ref[...]
    e = jnp.exp(p - jnp.max(p, axis=0, keepdims=True))
    sm = e / jnp.sum(e, axis=0, keepdims=True)
    lb_all = jnp.zeros((1, HG_QK), F32)
    for m in range(1, layer + 1):
        lb_all = lb_all + sm[m:m + 1, :]

    tri = lax.broadcasted_iota(jnp.int32, (c, c), 0) >= lax.broadcasted_iota(jnp.int32, (c, c), 1)
    tri_bf = jnp.where(tri, 1.0, 0.0).astype(BF16)
    col = lax.broadcasted_iota(jnp.int32, (SUBLANES, c), 1)

    def stage_gates(rows, h):
        hs = slice(h * HG_KDIM, (h + 1) * HG_KDIM)
        lb = lb_all[:, hs]
        oml = 1.0 - lb
        xf = f_ref[rows, hs]
        sig = jax.nn.sigmoid(xf)
        log2f = jnp.log2(lb + oml * sig)
        kk = oml * (1.0 - sig)
        hi = log2f.astype(BF16)
        r1 = log2f - hi.astype(F32)
        mid = r1.astype(BF16)
        lo = (r1 - mid.astype(F32)).astype(BF16)
        cs = _dot(tri_bf, jnp.concatenate([hi, mid, lo], axis=1))
        b = cs[:, 0:LANES] + cs[:, LANES:2 * LANES] + cs[:, 2 * LANES:3 * LANES]
        return b, kk, _silu(q_ref[rows, hs]), i_ref[rows, hs].astype(BF16)

    def stage_cross(slot, b, kk, qs):
        out = []
        for g in range(1, n_groups):
            r0 = g * SUBLANES
            ref = b_ref[slot, pl.ds(r0 - 1, 1), :]
            qt = (qs[r0:r0 + SUBLANES, :] * jnp.exp2(b[r0:r0 + SUBLANES, :] - ref)).astype(BF16)
            kt = jnp.concatenate([kk[0:r0, :] * jnp.exp2(ref - b[0:r0, :]), jnp.zeros((c - r0, LANES), F32)], axis=0)
            out.append(_dot_nt(qt, kt.astype(BF16)))
        return out

    def stage_scores(slot, b, qs, cross):
        groups = []
        for g in range(n_groups):
            r0 = g * SUBLANES
            qg = qs[r0:r0 + SUBLANES, :]
            bg = b[r0:r0 + SUBLANES, :]
            ag = cross[g - 1] if g else jnp.zeros((SUBLANES, c), F32)
            for s in range(r0, r0 + SUBLANES):
                bs = b_ref[slot, pl.ds(s, 1), :]
                ks = k_ref[slot, pl.ds(s, 1), :]
                rs = jnp.sum(qg * jnp.exp2(bg - bs) * ks, axis=-1, keepdims=True)
                ag = jnp.where(col == s, rs, ag)
            groups.append(ag)
        return jnp.where(tri, jnp.concatenate(groups, axis=0), 0.0).astype(BF16)

    def one_iter(it, carry):
        streams = [(u, h) for u in range(chunks_per_iter) for h in range(HG_HEADS)]
        rows = [pl.ds(pl.multiple_of((it * chunks_per_iter + u) * c, c), c) for u in range(chunks_per_iter)]
        gates = [stage_gates(rows[u], h) for u, h in streams]
        for slot, (b, kk, _, _) in enumerate(gates):
            b_ref[slot] = b
            k_ref[slot] = kk
        cross = [stage_cross(slot, b, kk, qs) for slot, (b, kk, qs, _) in enumerate(gates)]
        scores = [stage_scores(slot, b, qs, cross[slot]) for slot, (b, _, qs, _) in enumerate(gates)]
        st = [st_ref[h] for h in range(HG_HEADS)]
        for slot, (u, h) in enumerate(streams):
            b, kk, qs, v = gates[slot]
            hs = slice(h * HG_KDIM, (h + 1) * HG_KDIM)
            b_last = b_ref[slot, pl.ds(c - 1, 1), :]
            o = _dot(scores[slot], v) + _dot_nt((qs * jnp.exp2(b)).astype(BF16), st[h].astype(BF16))
            kt = (kk * jnp.exp2(b_last - b)).astype(BF16)
            st[h] = st[h] * jnp.exp2(b_last) + _dot_tn(v, kt)
            o_ref[rows[u], hs] = (_rms(o) * _silu(g_ref[rows[u], hs])).astype(o_ref.dtype)
        for h in range(HG_HEADS):
            st_ref[h] = st[h]
        return carry

    lax.fori_loop(0, n_chunks // chunks_per_iter, one_iter, 0)


def _hgrn(hg, lb_param, layer, block, chunk, chunks_per_iter=1):
    bsz, s, _ = hg.shape
    slots = HG_HEADS * chunks_per_iter
    spec = lambda seg: pl.BlockSpec((None, block, HG_QK), lambda b, i, seg=seg: (b, i, seg))
    return pl.pallas_call(
        functools.partial(_hgrn_body, layer=layer, chunk=chunk, n_chunks=block // chunk,
                          chunks_per_iter=chunks_per_iter),
        grid=(bsz, s // block),
        in_specs=[spec(0), spec(1), spec(2), spec(3),
                  pl.BlockSpec((DEPTH, HG_QK), lambda b, i: (0, 0))],
        out_specs=pl.BlockSpec((None, block, HG_W), lambda b, i: (b, i, 0)),
        out_shape=jax.ShapeDtypeStruct((bsz, s, HG_W), BF16),
        scratch_shapes=[pltpu.VMEM((HG_HEADS, HG_VDIM, HG_KDIM), F32),
                        pltpu.VMEM((slots, chunk, LANES), F32),
                        pltpu.VMEM((slots, chunk, LANES), F32)],
        compiler_params=_params(("arbitrary", "arbitrary"), 32),
        name=f"hgrn2_l{layer}",
    )(hg, hg, hg, hg, lb_param)


def _ret_log_gamma(h):
    return math.log(1.0 - 2.0 ** (-5.0 - h))


def _ret_body(q_ref, k_ref, v_ref, g_ref, cos_ref, sin_ref, o_ref,
              st_ref, dm_ref, fs_ref, te_ref, bd_ref, *, chunk, n_chunks, chunks_per_iter):
    c = chunk
    qk = RET_QK
    lane_head = lax.broadcasted_iota(jnp.int32, (c, qk), 1) // RET_KDIM

    def lane_log_gamma(head_idx):
        lg = jnp.full(head_idx.shape, _ret_log_gamma(0), F32)
        for h in range(1, RET_HEADS):
            lg = jnp.where(head_idx == h, _ret_log_gamma(h), lg)
        return lg

    @pl.when((pl.program_id(0) == 0) & (pl.program_id(1) == 0))
    def _():
        t = lax.broadcasted_iota(jnp.int32, (c, qk), 0).astype(F32)
        lg = lane_log_gamma(lane_head)
        fs_ref[...] = jnp.exp(lg * (t + 1.0))
        te_ref[...] = jnp.exp(lg * (c - 1.0 - t))
        dist = (lax.broadcasted_iota(jnp.int32, (c, c), 0) - lax.broadcasted_iota(jnp.int32, (c, c), 1))
        for h in range(RET_HEADS):
            dm_ref[h] = jnp.where(dist >= 0, jnp.exp(_ret_log_gamma(h) * jnp.maximum(dist, 0).astype(F32)), 0.0)
        rh = lax.broadcasted_iota(jnp.int32, (qk, RET_W), 0) // RET_KDIM
        ch = lax.broadcasted_iota(jnp.int32, (qk, RET_W), 1) // RET_VDIM
        bd_ref[...] = jnp.where(rh == ch, jnp.exp(lane_log_gamma(rh) * float(c)), 0.0)

    @pl.when(pl.program_id(1) == 0)
    def _():
        st_ref[...] = jnp.zeros_like(st_ref)

    in_first_half = (lax.broadcasted_iota(jnp.int32, (c, LANES), 1) % RET_KDIM) < (RET_KDIM // 2)

    def rope(t, cos, sin):
        parts = []
        for p in range(qk // LANES):
            th = t[:, p * LANES:(p + 1) * LANES]
            back = pltpu.roll(th, RET_KDIM // 2, axis=1)
            fwd = pltpu.roll(th, LANES - RET_KDIM // 2, axis=1)
            parts.append(th * cos + jnp.where(in_first_half, fwd, back) * sin)
        return jnp.concatenate(parts, axis=1)

    heads = range(RET_HEADS)
    hs = [slice(h * RET_VDIM, (h + 1) * RET_VDIM) for h in heads]

    def one_iter(it, carry):
        units = range(chunks_per_iter)
        rows = [pl.ds(pl.multiple_of((it * chunks_per_iter + u) * c, c), c) for u in units]
        qr, kr, v = [], [], []
        for u in units:
            cos = cos_ref[rows[u], :]
            sin = sin_ref[rows[u], :]
            qr.append(rope(q_ref[rows[u], :].astype(F32), cos, sin))
            kr.append(rope(k_ref[rows[u], :].astype(F32), cos, sin) * (RET_KDIM ** -0.5))
            v.append(v_ref[rows[u], :])
        kr_bf = [k.astype(BF16) for k in kr]
        scores = [[_dot_nt(jnp.where(lane_head == h, qr[u], 0.0).astype(BF16), kr_bf[u]) for h in heads]
                  for u in units]
        upd = [_dot_tn((kr[u] * te_ref[...]).astype(BF16), v[u]) for u in units]
        bd = bd_ref[...]
        st = st_ref[...]
        inter = []
        for u in units:
            inter.append(_dot((qr[u] * fs_ref[...]).astype(BF16), st.astype(BF16)))
            st = st * bd + jnp.where(bd > 0.0, upd[u], 0.0)
        st_ref[...] = st
        for u in units:
            g = g_ref[rows[u], :].astype(F32)
            for h in heads:
                p = (scores[u][h] * dm_ref[h]).astype(BF16)
                oh = _dot(p, v[u][:, hs[h]]) + inter[u][:, hs[h]]
                o_ref[rows[u], hs[h]] = (_rms(oh) * _silu(g[:, hs[h]])).astype(o_ref.dtype)
        return carry

    lax.fori_loop(0, n_chunks // chunks_per_iter, one_iter, 0)


def _retention(rt, cos_t, sin_t, block, chunk, name, chunks_per_iter=2):
    bsz, s, _ = rt.shape
    return pl.pallas_call(
        functools.partial(_ret_body, chunk=chunk, n_chunks=block // chunk, chunks_per_iter=chunks_per_iter),
        grid=(bsz, s // block),
        in_specs=[
            pl.BlockSpec((None, block, RET_QK), lambda b, i: (b, i, 0)),
            pl.BlockSpec((None, block, RET_QK), lambda b, i: (b, i, 1)),
            pl.BlockSpec((None, block, RET_W), lambda b, i: (b, i, 1)),
            pl.BlockSpec((None, block, RET_W), lambda b, i: (b, i, 2)),
            pl.BlockSpec((block, LANES), lambda b, i: (i, 0)),
            pl.BlockSpec((block, LANES), lambda b, i: (i, 0)),
        ],
        out_specs=pl.BlockSpec((None, block, RET_W), lambda b, i: (b, i, 0)),
        out_shape=jax.ShapeDtypeStruct((bsz, s, RET_W), BF16),
        scratch_shapes=[
            pltpu.VMEM((RET_QK, RET_W), F32),
            pltpu.VMEM((RET_HEADS, chunk, chunk), F32),
            pltpu.VMEM((chunk, RET_QK), F32),
            pltpu.VMEM((chunk, RET_QK), F32),
            pltpu.VMEM((RET_QK, RET_W), F32),
        ],
        compiler_params=_params(("arbitrary", "arbitrary"), 32),
        name=name,
    )(rt, rt, rt, rt, cos_t, sin_t)


def _t5_bucket_np(distance):
    max_exact = N_BUCKETS // 2
    n = np.maximum(distance, 1).astype(np.float32)
    large = max_exact + (np.log(n / max_exact) / math.log(MAX_DISTANCE / max_exact)
                         * (N_BUCKETS - max_exact)).astype(np.int32)
    large = np.minimum(large, N_BUCKETS - 1)
    return np.where(distance < max_exact, distance, large)


def _bucket_table(dilation):
    a = np.arange(ATT_BLK)[:, None]
    kk = np.arange(2 * ATT_BLK)[None, :]
    j = a + ATT_BLK - kk
    valid = (j >= 0) & (j <= ATT_BLK)
    bucket = _t5_bucket_np(np.clip(j, 0, ATT_BLK) * dilation)
    return np.where(valid, bucket, -1).astype(np.int32)


def _in_attn_body(h_ref, *refs):
    n_pieces = ATT_COLS // IN_ATTN_PIECE
    w_refs, outs, (res_ref, mid_ref) = refs[:n_pieces], refs[n_pieces:-2], refs[-2:]
    (_, d1), (_, d4), (_, d16) = DILATED_CONFIGS
    step = d16 // d4
    assert d1 == 1 and step == d4
    tm = h_ref.shape[0]
    piece = IN_ATTN_PIECE
    blocks = piece // LANES
    h = h_ref[...]
    for p in range(ATT_COLS // piece):
        seg, off = divmod(p * piece, ATT_W)
        o1_ref, o4_ref, o16_ref = outs[seg], outs[3 + seg], outs[6 + seg]
        val = _dot(h, w_refs[p][...])
        if seg == 0:
            val = val * (ATT_HDIM ** -0.5 * LOG2E)
        o1_ref[:, off:off + piece] = val.astype(BF16)
        for cb in range(blocks):
            slab = p * blocks + cb
            res_ref[slab] = val[:, cb * LANES:(cb + 1) * LANES]
            for r4 in range(d4):
                part = res_ref.at[slab][pl.ds(r4, tm // d4, stride=d4), :]
                lo = r4 * ATT_W + off + cb * LANES
                o4_ref[:, lo:lo + LANES] = part.astype(BF16)
                mid_ref[slab * d4 + r4] = part
                for m in range(step):
                    lo = (r4 + d4 * m) * ATT_W + off + cb * LANES
                    rows = pl.ds(m, tm // d16, stride=step)
                    o16_ref[:, lo:lo + LANES] = mid_ref.at[slab * d4 + r4][rows, :].astype(BF16)


def _in_attn(h, w, layer, col0, tm, name):
    t, dm = h.shape
    dilations = [d for _, d in DILATED_CONFIGS]
    piece = IN_ATTN_PIECE
    n_pieces = ATT_COLS // piece
    w_specs = [pl.BlockSpec((None, dm, piece), lambda i, p=p: (layer, 0, col0 // piece + p),
                            pipeline_mode=pl.Buffered(1)) for p in range(n_pieces)]
    out_shape, out_specs = [], []
    for d in dilations:
        for _ in range(3):
            out_shape.append(jax.ShapeDtypeStruct((t // d, d * ATT_W), BF16))
            out_specs.append(pl.BlockSpec((tm // d, d * ATT_W), lambda i: (i, 0)))
    outs = pl.pallas_call(
        _in_attn_body,
        grid=(t // tm,),
        in_specs=[pl.BlockSpec((tm, dm), lambda i: (i, 0))] + w_specs,
        out_specs=out_specs,
        out_shape=out_shape,
        scratch_shapes=[pltpu.VMEM((ATT_COLS // LANES, tm, LANES), F32),
                        pltpu.VMEM((ATT_COLS // LANES * dilations[1], tm // dilations[1], LANES), F32)],
        compiler_params=_params(("arbitrary",), 56),
        name=name,
    )(h, *([w] * n_pieces))
    return [tuple(outs[3 * di:3 * di + 3]) for di in range(len(dilations))]


def _attn_body(rb_ref, idx_ref, q_ref, kc_ref, kp_ref, vc_ref, vp_ref, o_ref, st_ref, bias_ref, *, n_sub):
    blk = ATT_BLK
    heads = range(ATT_HEADS)

    @pl.when((pl.program_id(0) == 0) & (pl.program_id(1) == 0) & (pl.program_id(2) == 0))
    def _():
        idx = idx_ref[...]
        in_prev = lax.broadcasted_iota(jnp.int32, idx.shape, 1) < blk
        for h in heads:
            acc = jnp.full(idx.shape, NEG, F32)
            for n in range(N_BUCKETS):
                acc = jnp.where(idx == n, rb_ref[n, h] * LOG2E, acc)
            bias_ref[0, h] = acc
            bias_ref[1, h] = jnp.where(in_prev, NEG, acc)

    sel0 = jnp.where(pl.program_id(2) == 0, 1, 0)
    hs = [slice(h * ATT_HDIM, (h + 1) * ATT_HDIM) for h in heads]
    lane = lax.broadcasted_iota(jnp.int32, (blk, LANES), 1)

    def qk(t):
        rows = slice(t * blk, (t + 1) * blk)
        out = []
        for h in heads:
            q = q_ref[rows, hs[h]]
            if t == 0:
                out.append((_dot_nt(q, kp_ref[:, hs[h]]) + bias_ref[sel0, h, :, 0:blk],
                            _dot_nt(q, kc_ref[rows, hs[h]]) + bias_ref[sel0, h, :, blk:2 * blk]))
            else:
                l2 = _dot_nt(q, kc_ref[(t - 1) * blk:(t + 1) * blk, hs[h]]) + bias_ref[0, h]
                out.append((l2[:, 0:blk], l2[:, blk:2 * blk]))
        return out

    ones = jnp.ones((blk, ATT_HDIM), BF16)

    def softmax(t, logits):
        probs = []
        for h in heads:
            lp, lc = logits[h]
            m = jnp.max(jnp.maximum(lp, lc), axis=-1, keepdims=True)
            probs.append((jnp.exp2(lp - m).astype(BF16), jnp.exp2(lc - m).astype(BF16), m))
        return probs

    def pv(t, probs):
        rows = slice(t * blk, (t + 1) * blk)
        stats = jnp.zeros((blk, LANES), F32)
        for h in heads:
            pp, pc, m = probs[h]
            vprev = vp_ref[:, hs[h]] if t == 0 else vc_ref[(t - 1) * blk:t * blk, hs[h]]
            od = (_dot(pp, jnp.concatenate([vprev, ones], axis=1))
                  + _dot(pc, jnp.concatenate([vc_ref[rows, hs[h]], ones], axis=1)))
            den = od[:, ATT_HDIM:]
            o_ref[rows, hs[h]] = (od[:, 0:ATT_HDIM] / den).astype(o_ref.dtype)
            stats = jnp.where(lane == h, m, stats)
            stats = jnp.where(lane == ATT_HEADS + h, den, stats)
        st_ref[rows, :] = stats

    logits = qk(0)
    for t in range(n_sub):
        probs = softmax(t, logits)
        if t + 1 < n_sub:
            logits = qk(t + 1)
        pv(t, probs)


def _attn_branch(q, k, v, rel_bias, bsz, dilation, n_sub, name):
    rows = q.shape[0]
    l = rows // bsz
    blk = ATT_BLK
    lb = n_sub * blk
    nb = l // lb
    q, k, v = (a.reshape(bsz, l, dilation * ATT_W) for a in (q, k, v))
    idx = jnp.asarray(_bucket_table(dilation))
    cur = pl.BlockSpec((None, lb, ATT_W), lambda b, r, i: (b, i, r))
    prev = pl.BlockSpec((None, blk, ATT_W), lambda b, r, i: (b, jnp.maximum(i * n_sub - 1, 0), r))
    o, st = pl.pallas_call(
        functools.partial(_attn_body, n_sub=n_sub),
        grid=(bsz, dilation, nb),
        in_specs=[
            pl.BlockSpec(memory_space=pltpu.SMEM),
            pl.BlockSpec((blk, 2 * blk), lambda b, r, i: (0, 0)),
            cur, cur, prev, cur, prev,
        ],
        out_specs=[
            pl.BlockSpec((None, lb, ATT_W), lambda b, r, i: (b, i, r)),
            pl.BlockSpec((None, lb, LANES), lambda b, r, i: (b, i, r)),
        ],
        out_shape=[
            jax.ShapeDtypeStruct((bsz, l, dilation * ATT_W), BF16),
            jax.ShapeDtypeStruct((bsz, l, dilation * LANES), F32),
        ],
        scratch_shapes=[pltpu.VMEM((2, ATT_HEADS, blk, 2 * blk), F32)],
        compiler_params=_params(("arbitrary", "arbitrary", "arbitrary"), 32),
        name=name,
    )(rel_bias, idx, q, k, k, v, v)
    return o.reshape(rows, dilation * ATT_W), st.reshape(rows, dilation * LANES)


def _merge_heads(o_refs, s_refs, scratch, tm):
    scratch = list(scratch)
    heads, stats = [], []
    for (_, d), o_ref, s_ref in zip(DILATED_CONFIGS, o_refs, s_refs):
        if d == 1:
            heads.append(lambda h, o_ref=o_ref: o_ref[:, h * ATT_HDIM:(h + 1) * ATT_HDIM].astype(F32))
            stats.append(s_ref[...])
            continue
        po_ref, ps_ref = scratch.pop(0), scratch.pop(0)
        hop = MERGE_STRIDE
        if d > hop:
            mo_ref, ms_ref = scratch.pop(0), scratch.pop(0)
        for r in range(d):
            blocks = [o_ref[:, r * ATT_W + h * ATT_HDIM:r * ATT_W + (h + 1) * ATT_HDIM].astype(F32)
                      for h in range(ATT_HEADS)] + [s_ref[:, r * LANES:(r + 1) * LANES]]
            if d <= hop:
                rows = pl.ds(r, tm // d, stride=d)
                for h in range(ATT_HEADS):
                    po_ref.at[h][rows, :] = blocks[h]
                ps_ref[rows, :] = blocks[-1]
            else:
                r_lo, m = r % hop, r // hop
                rows = pl.ds(m, tm // d, stride=d // hop)
                for h in range(ATT_HEADS):
                    mo_ref.at[r_lo * ATT_HEADS + h][rows, :] = blocks[h]
                ms_ref.at[r_lo][rows, :] = blocks[-1]
        if d > hop:
            for r_lo in range(hop):
                rows = pl.ds(r_lo, tm // hop, stride=hop)
                for h in range(ATT_HEADS):
                    po_ref.at[h][rows, :] = mo_ref[r_lo * ATT_HEADS + h]
                ps_ref[rows, :] = ms_ref[r_lo]
        heads.append(lambda h, po_ref=po_ref: po_ref[h])
        stats.append(ps_ref[...])
    mx = functools.reduce(jnp.maximum, stats)
    ws = [pltpu.roll(s, LANES - ATT_HEADS, axis=1) * jnp.exp2(s - mx) for s in stats]
    inv = 1.0 / functools.reduce(lambda a, b: a + b, ws)
    ws = [w * inv for w in ws]

    def head(h):
        acc = ws[0][:, h:h + 1] * heads[0](h)
        for w, get in zip(ws[1:], heads[1:]):
            acc = acc + w[:, h:h + 1] * get(h)
        return acc

    return head


def _outproj_body(x_ref, a_ref, b_ref, *refs):
    n = len(DILATED_CONFIGS)
    o_refs, s_refs = refs[:n], refs[n:2 * n]
    w_ref, g_ref, o_ref, u_ref = refs[2 * n:2 * n + 4]
    tm = x_ref.shape[0]
    acc = x_ref[...]
    acc = acc + _dot(a_ref[...], w_ref[0:HG_W, :])
    acc = acc + _dot(b_ref[...], w_ref[HG_W:HG_W + RET_W, :])
    head = _merge_heads(o_refs, s_refs, refs[2 * n + 4:], tm)
    for h in range(0, ATT_HEADS, 2):
        c2 = jnp.concatenate([head(h), head(h + 1)], axis=1).astype(BF16)
        r0 = HG_W + RET_W + h * ATT_HDIM
        acc = acc + _dot(c2, w_ref[r0:r0 + 2 * ATT_HDIM, :])
    o_ref[...] = acc
    u_ref[...] = (_rms(acc) * g_ref[...]).astype(u_ref.dtype)


def _outproj(x, a, b, outs, stats, w, layer, g, tm, name):
    t, d = x.shape
    row = lambda width: pl.BlockSpec((tm, width), lambda i: (i, 0))
    in_specs, scratch = [row(d), row(HG_W), row(RET_W)], []
    for width in (ATT_W, LANES):
        for _, dil in DILATED_CONFIGS:
            in_specs.append(pl.BlockSpec((tm // dil, dil * width), lambda i: (i, 0)))
    for _, dil in DILATED_CONFIGS:
        if dil > 1:
            scratch += [pltpu.VMEM((ATT_HEADS, tm, ATT_HDIM), F32), pltpu.VMEM((tm, LANES), F32)]
        if dil > MERGE_STRIDE:
            scratch += [pltpu.VMEM((MERGE_STRIDE * ATT_HEADS, tm // MERGE_STRIDE, ATT_HDIM), F32),
                        pltpu.VMEM((MERGE_STRIDE, tm // MERGE_STRIDE, LANES), F32)]
    in_specs += [pl.BlockSpec((None,) + w.shape[1:], lambda i: (layer, 0, 0), pipeline_mode=pl.Buffered(1)),
                 pl.BlockSpec((1, d), lambda i: (0, 0))]
    return pl.pallas_call(
        _outproj_body,
        grid=(t // tm,),
        in_specs=in_specs,
        out_specs=[row(d), row(d)],
        out_shape=[jax.ShapeDtypeStruct((t, d), F32), jax.ShapeDtypeStruct((t, d), BF16)],
        scratch_shapes=scratch,
        compiler_params=_params(("arbitrary",), 56),
        name=name,
    )(x, a, b, *outs, *stats, w, g.reshape(1, d))


def _ffn1_body(u_ref, wg_ref, wu_ref, cw_ref, cb_ref, o_ref, carry_ref, *, tiles_per_seq):
    i = pl.program_id(0)
    j = pl.program_id(1)
    tm = u_ref.shape[0]
    u = u_ref[...]
    gp = _dot(u, wg_ref[...])
    up = _dot(u, wu_ref[...])
    w0 = cw_ref[0:1, :]
    w1 = cw_ref[1:2, :]
    w2 = cw_ref[2:3, :]
    cb = cb_ref[...]

    gate = w2 * gp + w1 * pltpu.roll(gp, 1, axis=0) + w0 * pltpu.roll(gp, 2, axis=0) + cb
    o_ref[...] = (_silu(gate) * up).astype(o_ref.dtype)

    prev = jnp.where(i % tiles_per_seq == 0, 0.0, carry_ref[j])
    carry_ref[j] = gp[tm - SUBLANES:, :]
    top = gp[0:SUBLANES, :]
    r = lax.broadcasted_iota(jnp.int32, top.shape, 0)
    p1 = prev[SUBLANES - 1:SUBLANES, :]
    p2 = prev[SUBLANES - 2:SUBLANES - 1, :]
    t1 = jnp.where(r == 0, p1, pltpu.roll(top, 1, axis=0))
    t2 = jnp.where(r == 0, p2, jnp.where(r == 1, p1, pltpu.roll(top, 2, axis=0)))
    gate_top = w2 * top + w1 * t1 + w0 * t2 + cb
    o_ref[0:SUBLANES, :] = (_silu(gate_top) * up[0:SUBLANES, :]).astype(o_ref.dtype)


def _ffn1(u, wg, wu, cw, cb, layer, seq_len, tm, tn, name):
    t, d = u.shape
    n = wg.shape[2]
    nj = n // tn
    wspec = lambda rows: pl.BlockSpec((None, rows, tn), lambda i, j: (layer, 0, j))
    return pl.pallas_call(
        functools.partial(_ffn1_body, tiles_per_seq=seq_len // tm),
        grid=(t // tm, nj),
        in_specs=[pl.BlockSpec((tm, d), lambda i, j: (i, 0)), wspec(d), wspec(d), wspec(cw.shape[1]), wspec(1)],
        out_specs=pl.BlockSpec((tm, tn), lambda i, j: (i, j)),
        out_shape=jax.ShapeDtypeStruct((t, n), BF16),
        scratch_shapes=[pltpu.VMEM((nj, SUBLANES, tn), F32)],
        compiler_params=_params(("arbitrary", "arbitrary"), 48),
        name=name,
    )(u, wg, wu, cw, cb)


def _ffn2_body(x_ref, h_ref, w_ref, g_ref, *out_refs, last_layer):
    y = x_ref[...] + _dot(h_ref[...], w_ref[...])
    normed = _rms(y) * g_ref[...]
    if last_layer:
        out_refs[0][...] = normed
    else:
        out_refs[0][...] = y
        out_refs[1][...] = normed.astype(out_refs[1].dtype)


def _ffn2(x, h, w, layer, g, last_layer, tm, name):
    t, d = x.shape
    kdim = h.shape[1]
    row = pl.BlockSpec((tm, d), lambda i: (i, 0))
    out_shape = [jax.ShapeDtypeStruct((t, d), F32)]
    if not last_layer:
        out_shape.append(jax.ShapeDtypeStruct((t, d), BF16))
    return pl.pallas_call(
        functools.partial(_ffn2_body, last_layer=last_layer),
        grid=(t // tm,),
        in_specs=[
            row,
            pl.BlockSpec((tm, kdim), lambda i: (i, 0)),
            pl.BlockSpec((None, kdim, d), lambda i: (layer, 0, 0), pipeline_mode=pl.Buffered(1)),
            pl.BlockSpec((1, d), lambda i: (0, 0)),
        ],
        out_specs=[row] * len(out_shape),
        out_shape=out_shape,
        compiler_params=_params(("arbitrary",), 60),
        name=name,
    )(x, h, w, g.reshape(1, d))


def _cast_pad_body(x_ref, o_ref, *, axis):
    if axis == 0:
        o_ref[0:D_FF, :] = x_ref[...].astype(o_ref.dtype)
        o_ref[D_FF:, :] = jnp.zeros((D_FF_PAD - D_FF, o_ref.shape[1]), o_ref.dtype)
    else:
        o_ref[:, 0:D_FF] = x_ref[...].astype(o_ref.dtype)
        o_ref[:, D_FF:] = jnp.zeros((o_ref.shape[0], D_FF_PAD - D_FF), o_ref.dtype)


def _cast_pad(w, axis, tile, name):
    depth, a, b = w.shape
    if axis == 1:
        in_block, out_block, out_dims = (None, D_FF, tile), (None, D_FF_PAD, tile), (depth, D_FF_PAD, b)
        index, grid = (lambda l, i: (l, 0, i)), (depth, b // tile)
    else:
        in_block, out_block, out_dims = (None, tile, D_FF), (None, tile, D_FF_PAD), (depth, a, D_FF_PAD)
        index, grid = (lambda l, i: (l, i, 0)), (depth, a // tile)
    return pl.pallas_call(
        functools.partial(_cast_pad_body, axis=axis - 1),
        grid=grid,
        in_specs=[pl.BlockSpec(in_block, index)],
        out_specs=pl.BlockSpec(out_block, index),
        out_shape=jax.ShapeDtypeStruct(out_dims, BF16),
        compiler_params=_params(("arbitrary", "arbitrary"), 32),
        name=name,
    )(w)


def _rope_tables(s):
    inv_freq = ROPE_BASE ** (-jnp.arange(0, RET_KDIM, 2, dtype=F32) / RET_KDIM)
    ang = jnp.arange(s, dtype=F32)[:, None] * inv_freq[None, :]
    cos, sin = jnp.cos(ang), jnp.sin(ang)
    cos_t = jnp.tile(jnp.concatenate([cos, cos], axis=1), (1, LANES // RET_KDIM))
    sin_t = jnp.tile(jnp.concatenate([-sin, sin], axis=1), (1, LANES // RET_KDIM))
    return cos_t, sin_t


def kernel(x, norm_mix, w_in, hg_lower_bound, w_out, norm_ffn, w_gate, conv_w, conv_b,
           w_up, w_down, rel_bias, norm_final):
    bsz, s, d = x.shape
    t = bsz * s
    pad = D_FF_PAD - D_FF
    cos_t, sin_t = _rope_tables(s)
    x2 = x.reshape(t, d)
    w_in_bf = w_in.astype(BF16)
    w_out_bf = w_out.astype(BF16)
    wg = _cast_pad(w_gate, 2, CAST_TILE, "cast_w_gate")
    wu = _cast_pad(w_up, 2, CAST_TILE, "cast_w_up")
    wd = _cast_pad(w_down, 1, CAST_TILE, "cast_w_down")
    cw = jnp.pad(conv_w, ((0, 0), (0, 0), (0, pad)))
    cb = jnp.pad(conv_b, ((0, 0), (0, pad))).reshape(DEPTH, 1, D_FF_PAD)
    h = None
    for l in range(DEPTH):
        last = l == DEPTH - 1
        if l == 0:
            hg, h = _norm_matmul(x2, norm_mix[0], w_in_bf, l, 0, HG_COLS, F32, TM_NORM_IN, f"in_hgrn_l{l}")
        else:
            hg = _matmul(h, w_in_bf, l, 0, HG_COLS, F32, *TILE_IN_HGRN, f"in_hgrn_l{l}")
        rt = _matmul(h, w_in_bf, l, HG_COLS, RET_COLS, BF16, *TILE_IN_RET, f"in_ret_l{l}")
        qkv = _in_attn(h, w_in_bf, l, HG_COLS + RET_COLS, TM_IN_ATTN, f"in_attn_l{l}")

        a = _hgrn(hg.reshape(bsz, s, HG_COLS), hg_lower_bound, l, *HGRN_BLOCKING)
        b = _retention(rt.reshape(bsz, s, RET_COLS), cos_t, sin_t, *RET_BLOCKING, name=f"retention_l{l}")
        outs, stats = zip(*[_attn_branch(*qkv[di], rel_bias, bsz, dil, ATTN_SUBTILES, f"attn_d{dil}_l{l}")
                            for di, (_, dil) in enumerate(DILATED_CONFIGS)])
        x2, u = _outproj(x2, a.reshape(t, HG_W), b.reshape(t, RET_W), outs, stats, w_out_bf, l,
                         norm_ffn[l], TM_OUT_PROJ, f"out_proj_l{l}")
        hmid = _ffn1(u, wg, wu, cw, cb, l, s, *TILE_FFN_UP, f"ffn_gate_up_l{l}")
        res = _ffn2(x2, hmid, wd, l, norm_final if last else norm_mix[l + 1], last, TM_FFN_DOWN, f"ffn_down_l{l}")
        if last:
            x2 = res[0]
        else:
            x2, h = res
    return x2.reshape(bsz, s, d)
```

```python
import functools
import math

import numpy as np
import jax
import jax.numpy as jnp
from jax import lax
from jax.experimental import pallas as pl
from jax.experimental.pallas import tpu as pltpu

F32 = jnp.float32
BF16 = jnp.bfloat16

D_MODEL = 2048
DEPTH = 2
HG_HEADS = 4
HG_KDIM = 128
HG_VDIM = 128
RET_HEADS = 4
RET_KDIM = 64
RET_VDIM = 128
ROPE_BASE = 10000.0
ATT_HEADS = 8
ATT_HDIM = 128
DILATED_CONFIGS = ((128, 1), (512, 4), (2048, 16))
N_BUCKETS = 32
MAX_DISTANCE = 2048
D_FF = 5504
EPS = 1e-6

HG_QK = HG_HEADS * HG_KDIM
HG_W = HG_HEADS * HG_VDIM
RET_QK = RET_HEADS * RET_KDIM
RET_W = RET_HEADS * RET_VDIM
ATT_W = ATT_HEADS * ATT_HDIM
HG_COLS = 2 * HG_QK + 2 * HG_W
RET_COLS = 2 * RET_QK + 2 * RET_W
ATT_COLS = 3 * ATT_W

LANES = 128
SUBLANES = 8
D_FF_PAD = 5632
ATT_BLK = 128
IN_ATTN_PIECE = 512
MERGE_STRIDE = 4

TILE_IN_HGRN = (1024, 1024)
TM_NORM_IN = 512
TILE_IN_RET = (2048, 512)
TM_IN_ATTN = 512
HGRN_BLOCKING = (1024, 32, 4)
RET_BLOCKING = (1024, 128)
ATTN_SUBTILES = 8
TM_OUT_PROJ = 512
OUT_PROJ_ROWS = 256
TILE_FFN_UP = (2048, 512)
FFN_UP_ROWS = 1024
TM_FFN_DOWN = 512
CAST_TILE = 256
NEG = -1e30
LOG2E = math.log2(math.e)
MIB = 1024 * 1024


def _dot(a, b):
    return jnp.dot(a, b, preferred_element_type=F32)


def _dot_nt(a, b):
    return lax.dot_general(a, b, (((1,), (1,)), ((), ())), preferred_element_type=F32)


def _dot_tn(a, b):
    return lax.dot_general(a, b, (((0,), (0,)), ((), ())), preferred_element_type=F32)


def _params(semantics, vmem_mib):
    return pltpu.CompilerParams(dimension_semantics=semantics, vmem_limit_bytes=vmem_mib * MIB)


def _rms(x):
    return x * lax.rsqrt(jnp.mean(x * x, axis=-1, keepdims=True) + EPS)


def _silu(x):
    return x * jax.nn.sigmoid(x)


def _norm_matmul_body(x_ref, g_ref, w_ref, o_ref, h_ref):
    half = x_ref.shape[0] // 2
    for r in range(2):
        rows = slice(r * half, (r + 1) * half)
        h = (_rms(x_ref[rows, :]) * g_ref[...]).astype(h_ref.dtype)
        h_ref[rows, :] = h
        o_ref[rows, :] = _dot(h, w_ref[...]).astype(o_ref.dtype)


def _norm_matmul(x, g, w, layer, col0, n, out_dtype, tm, name):
    t, d = x.shape
    row = lambda width: pl.BlockSpec((tm, width), lambda i: (i, 0))
    return pl.pallas_call(
        _norm_matmul_body,
        grid=(t // tm,),
        in_specs=[row(d), pl.BlockSpec((1, d), lambda i: (0, 0)),
                  pl.BlockSpec((None, d, n), lambda i: (layer, 0, col0 // n), pipeline_mode=pl.Buffered(1))],
        out_specs=[row(n), row(d)],
        out_shape=[jax.ShapeDtypeStruct((t, n), out_dtype), jax.ShapeDtypeStruct((t, d), BF16)],
        compiler_params=_params(("arbitrary",), 48),
        name=name,
    )(x, g.reshape(1, d), w)


def _matmul_body(h_ref, w_ref, o_ref):
    o_ref[...] = _dot(h_ref[...], w_ref[...]).astype(o_ref.dtype)


def _matmul(h, w, layer, col0, n, out_dtype, tm, tn, name):
    t, d = h.shape
    j0 = col0 // tn
    return pl.pallas_call(
        _matmul_body,
        grid=(t // tm, n // tn),
        in_specs=[pl.BlockSpec((tm, d), lambda i, j: (i, 0)),
                  pl.BlockSpec((None, d, tn), lambda i, j: (layer, 0, j + j0))],
        out_specs=pl.BlockSpec((tm, tn), lambda i, j: (i, j)),
        out_shape=jax.ShapeDtypeStruct((t, n), out_dtype),
        compiler_params=_params(("arbitrary", "arbitrary"), 48),
        name=name,
    )(h, w)


def _hgrn_body(q_ref, f_ref, i_ref, g_ref, lbp_ref, o_ref, st_ref, b_ref, k_ref, *,
               layer, chunk, n_chunks, chunks_per_iter):
    c = chunk
    n_groups = c // SUBLANES

    @pl.when(pl.program_id(1) == 0)
    def _():
        st_ref[...] = jnp.zeros_like(st_ref)

    p = lbp_ref[...]
    e = jnp.exp(p - jnp.max(p, axis=0, keepdims=True))
    sm = e / jnp.sum(e, axis=0, keepdims=True)
    lb_all = jnp.zeros((1, HG_QK), F32)
    for m in range(1, layer + 1):
        lb_all = lb_all + sm[m:m + 1, :]

    tri = lax.broadcasted_iota(jnp.int32, (c, c), 0) >= lax.broadcasted_iota(jnp.int32, (c, c), 1)
    tri_bf = jnp.where(tri, 1.0, 0.0).astype(BF16)
    col = lax.broadcasted_iota(jnp.int32, (SUBLANES, c), 1)

    def stage_gates(rows, h):
        hs = slice(h * HG_KDIM, (h + 1) * HG_KDIM)
        lb = lb_all[:, hs]
        oml = 1.0 - lb
        xf = f_ref[rows, hs]
        sig = jax.nn.sigmoid(xf)
        log2f = jnp.log2(lb + oml * sig)
        kk = oml * (1.0 - sig)
        hi = log2f.astype(BF16)
        r1 = log2f - hi.astype(F32)
        mid = r1.astype(BF16)
        lo = (r1 - mid.astype(F32)).astype(BF16)
        cs = _dot(tri_bf, jnp.concatenate([hi, mid, lo], axis=1))
        b = cs[:, 0:LANES] + cs[:, LANES:2 * LANES] + cs[:, 2 * LANES:3 * LANES]
        return b, kk, _silu(q_ref[rows, hs]), i_ref[rows, hs].astype(BF16)

    def stage_cross(slot, b, kk, qs):
        out = []
        for g in range(1, n_groups):
            r0 = g * SUBLANES
            ref = b_ref[slot, pl.ds(r0 - 1, 1), :]
            qt = (qs[r0:r0 + SUBLANES, :] * jnp.exp2(b[r0:r0 + SUBLANES, :] - ref)).astype(BF16)
            kt = jnp.concatenate([kk[0:r0, :] * jnp.exp2(ref - b[0:r0, :]), jnp.zeros((c - r0, LANES), F32)], axis=0)
            out.append(_dot_nt(qt, kt.astype(BF16)))
        return out

    def stage_scores(slot, b, qs, cross):
        groups = []
        for g in range(n_groups):
            r0 = g * SUBLANES
            qg = qs[r0:r0 + SUBLANES, :]
            bg = b[r0:r0 + SUBLANES, :]
            ag = cross[g - 1] if g else jnp.zeros((SUBLANES, c), F32)
            for s in range(r0, r0 + SUBLANES):
                bs = b_ref[slot, pl.ds(s, 1), :]
                ks = k_ref[slot, pl.ds(s, 1), :]
                rs = jnp.sum(qg * jnp.exp2(bg - bs) * ks, axis=-1, keepdims=True)
                ag = jnp.where(col == s, rs, ag)
            groups.append(ag)
        return jnp.where(tri, jnp.concatenate(groups, axis=0), 0.0).astype(BF16)

    def one_iter(it, carry):
        streams = [(u, h) for u in range(chunks_per_iter) for h in range(HG_HEADS)]
        rows = [pl.ds(pl.multiple_of((it * chunks_per_iter + u) * c, c), c) for u in range(chunks_per_iter)]
        gates = [stage_gates(rows[u], h) for u, h in streams]
        for slot, (b, kk, _, _) in enumerate(gates):
            b_ref[slot] = b
            k_ref[slot] = kk
        cross = [stage_cross(slot, b, kk, qs) for slot, (b, kk, qs, _) in enumerate(gates)]
        scores = [stage_scores(slot, b, qs, cross[slot]) for slot, (b, _, qs, _) in enumerate(gates)]
        st = [st_ref[h] for h in range(HG_HEADS)]
        for slot, (u, h) in enumerate(streams):
            b, kk, qs, v = gates[slot]
            hs = slice(h * HG_KDIM, (h + 1) * HG_KDIM)
            b_last = b_ref[slot, pl.ds(c - 1, 1), :]
            o = _dot(scores[slot], v) + _dot_nt((qs * jnp.exp2(b)).astype(BF16), st[h].astype(BF16))
            kt = (kk * jnp.exp2(b_last - b)).astype(BF16)
            st[h] = st[h] * jnp.exp2(b_last) + _dot_tn(v, kt)
            o_ref[rows[u], hs] = (_rms(o) * _silu(g_ref[rows[u], hs])).astype(o_ref.dtype)
        for h in range(HG_HEADS):
            st_ref[h] = st[h]
        return carry

    lax.fori_loop(0, n_chunks // chunks_per_iter, one_iter, 0)


def _hgrn(hg, lb_param, layer, block, chunk, chunks_per_iter=1):
    bsz, s, _ = hg.shape
    slots = HG_HEADS * chunks_per_iter
    spec = lambda seg: pl.BlockSpec((None, block, HG_QK), lambda b, i, seg=seg: (b, i, seg))
    return pl.pallas_call(
        functools.partial(_hgrn_body, layer=layer, chunk=chunk, n_chunks=block // chunk,
                          chunks_per_iter=chunks_per_iter),
        grid=(bsz, s // block),
        in_specs=[spec(0), spec(1), spec(2), spec(3),
                  pl.BlockSpec((DEPTH, HG_QK), lambda b, i: (0, 0))],
        out_specs=pl.BlockSpec((None, block, HG_W), lambda b, i: (b, i, 0)),
        out_shape=jax.ShapeDtypeStruct((bsz, s, HG_W), BF16),
        scratch_shapes=[pltpu.VMEM((HG_HEADS, HG_VDIM, HG_KDIM), F32),
                        pltpu.VMEM((slots, chunk, LANES), F32),
                        pltpu.VMEM((slots, chunk, LANES), F32)],
        compiler_params=_params(("arbitrary", "arbitrary"), 32),
        name=f"hgrn2_l{layer}",
    )(hg, hg, hg, hg, lb_param)


def _ret_log_gamma(h):
    return math.log(1.0 - 2.0 ** (-5.0 - h))


def _ret_body(q_ref, k_ref, v_ref, g_ref, cos_ref, sin_ref, o_ref,
              st_ref, dm_ref, fs_ref, te_ref, bd_ref, *, chunk, n_chunks, chunks_per_iter):
    c = chunk
    qk = RET_QK
    lane_head = lax.broadcasted_iota(jnp.int32, (c, qk), 1) // RET_KDIM

    def lane_log_gamma(head_idx):
        lg = jnp.full(head_idx.shape, _ret_log_gamma(0), F32)
        for h in range(1, RET_HEADS):
            lg = jnp.where(head_idx == h, _ret_log_gamma(h), lg)
        return lg

    @pl.when((pl.program_id(0) == 0) & (pl.program_id(1) == 0))
    def _():
        t = lax.broadcasted_iota(jnp.int32, (c, qk), 0).astype(F32)
        lg = lane_log_gamma(lane_head)
        fs_ref[...] = jnp.exp(lg * (t + 1.0))
        te_ref[...] = jnp.exp(lg * (c - 1.0 - t))
        dist = (lax.broadcasted_iota(jnp.int32, (c, c), 0) - lax.broadcasted_iota(jnp.int32, (c, c), 1))
        for h in range(RET_HEADS):
            dm_ref[h] = jnp.where(dist >= 0, jnp.exp(_ret_log_gamma(h) * jnp.maximum(dist, 0).astype(F32)), 0.0)
        rh = lax.broadcasted_iota(jnp.int32, (qk, RET_W), 0) // RET_KDIM
        ch = lax.broadcasted_iota(jnp.int32, (qk, RET_W), 1) // RET_VDIM
        bd_ref[...] = jnp.where(rh == ch, jnp.exp(lane_log_gamma(rh) * float(c)), 0.0)

    @pl.when(pl.program_id(1) == 0)
    def _():
        st_ref[...] = jnp.zeros_like(st_ref)

    in_first_half = (lax.broadcasted_iota(jnp.int32, (c, LANES), 1) % RET_KDIM) < (RET_KDIM // 2)

    def rope(t, cos, sin):
        parts = []
        for p in range(qk // LANES):
            th = t[:, p * LANES:(p + 1) * LANES]
            back = pltpu.roll(th, RET_KDIM // 2, axis=1)
            fwd = pltpu.roll(th, LANES - RET_KDIM // 2, axis=1)
            parts.append(th * cos + jnp.where(in_first_half, fwd, back) * sin)
        return jnp.concatenate(parts, axis=1)

    heads = range(RET_HEADS)
    hs = [slice(h * RET_VDIM, (h + 1) * RET_VDIM) for h in heads]

    def one_iter(it, carry):
        units = range(chunks_per_iter)
        rows = [pl.ds(pl.multiple_of((it * chunks_per_iter + u) * c, c), c) for u in units]
        qr, kr, v = [], [], []
        for u in units:
            cos = cos_ref[rows[u], :]
            sin = sin_ref[rows[u], :]
            qr.append(rope(q_ref[rows[u], :].astype(F32), cos, sin))
            kr.append(rope(k_ref[rows[u], :].astype(F32), cos, sin) * (RET_KDIM ** -0.5))
            v.append(v_ref[rows[u], :])
        kr_bf = [k.astype(BF16) for k in kr]
        scores = [[_dot_nt(jnp.where(lane_head == h, qr[u], 0.0).astype(BF16), kr_bf[u]) for h in heads]
                  for u in units]
        upd = [_dot_tn((kr[u] * te_ref[...]).astype(BF16), v[u]) for u in units]
        bd = bd_ref[...]
        st = st_ref[...]
        inter = []
        for u in units:
            inter.append(_dot((qr[u] * fs_ref[...]).astype(BF16), st.astype(BF16)))
            st = st * bd + jnp.where(bd > 0.0, upd[u], 0.0)
        st_ref[...] = st
        for u in units:
            g = g_ref[rows[u], :].astype(F32)
            for h in heads:
                p = (scores[u][h] * dm_ref[h]).astype(BF16)
                oh = _dot(p, v[u][:, hs[h]]) + inter[u][:, hs[h]]
                o_ref[rows[u], hs[h]] = (_rms(oh) * _silu(g[:, hs[h]])).astype(o_ref.dtype)
        return carry

    lax.fori_loop(0, n_chunks // chunks_per_iter, one_iter, 0)


def _retention(rt, cos_t, sin_t, block, chunk, name, chunks_per_iter=2):
    bsz, s, _ = rt.shape
    return pl.pallas_call(
        functools.partial(_ret_body, chunk=chunk, n_chunks=block // chunk, chunks_per_iter=chunks_per_iter),
        grid=(bsz, s // block),
        in_specs=[
            pl.BlockSpec((None, block, RET_QK), lambda b, i: (b, i, 0)),
            pl.BlockSpec((None, block, RET_QK), lambda b, i: (b, i, 1)),
            pl.BlockSpec((None, block, RET_W), lambda b, i: (b, i, 1)),
            pl.BlockSpec((None, block, RET_W), lambda b, i: (b, i, 2)),
            pl.BlockSpec((block, LANES), lambda b, i: (i, 0)),
            pl.BlockSpec((block, LANES), lambda b, i: (i, 0)),
        ],
        out_specs=pl.BlockSpec((None, block, RET_W), lambda b, i: (b, i, 0)),
        out_shape=jax.ShapeDtypeStruct((bsz, s, RET_W), BF16),
        scratch_shapes=[
            pltpu.VMEM((RET_QK, RET_W), F32),
            pltpu.VMEM((RET_HEADS, chunk, chunk), F32),
            pltpu.VMEM((chunk, RET_QK), F32),
            pltpu.VMEM((chunk, RET_QK), F32),
            pltpu.VMEM((RET_QK, RET_W), F32),
        ],
        compiler_params=_params(("arbitrary", "arbitrary"), 32),
        name=name,
    )(rt, rt, rt, rt, cos_t, sin_t)


def _t5_bucket_np(distance):
    max_exact = N_BUCKETS // 2
    n = np.maximum(distance, 1).astype(np.float32)
    large = max_exact + (np.log(n / max_exact) / math.log(MAX_DISTANCE / max_exact)
                         * (N_BUCKETS - max_exact)).astype(np.int32)
    large = np.minimum(large, N_BUCKETS - 1)
    return np.where(distance < max_exact, distance, large)


def _bucket_table(dilation):
    a = np.arange(ATT_BLK)[:, None]
    kk = np.arange(2 * ATT_BLK)[None, :]
    j = a + ATT_BLK - kk
    valid = (j >= 0) & (j <= ATT_BLK)
    bucket = _t5_bucket_np(np.clip(j, 0, ATT_BLK) * dilation)
    return np.where(valid, bucket, -1).astype(np.int32)


def _in_attn_body(h_ref, *refs):
    n_pieces = ATT_COLS // IN_ATTN_PIECE
    w_refs, outs, (res_ref, mid_ref) = refs[:n_pieces], refs[n_pieces:-2], refs[-2:]
    (_, d1), (_, d4), (_, d16) = DILATED_CONFIGS
    step = d16 // d4
    assert d1 == 1 and step == d4
    tm = h_ref.shape[0]
    piece = IN_ATTN_PIECE
    blocks = piece // LANES
    h = h_ref[...]
    for p in range(ATT_COLS // piece):
        seg, off = divmod(p * piece, ATT_W)
        o1_ref, o4_ref, o16_ref = outs[seg], outs[3 + seg], outs[6 + seg]
        val = _dot(h, w_refs[p][...])
        if seg == 0:
            val = val * (ATT_HDIM ** -0.5 * LOG2E)
        o1_ref[:, off:off + piece] = val.astype(BF16)
        for cb in range(blocks):
            slab = p * blocks + cb
            res_ref[slab] = val[:, cb * LANES:(cb + 1) * LANES]
            for r4 in range(d4):
                part = res_ref.at[slab][pl.ds(r4, tm // d4, stride=d4), :]
                lo = r4 * ATT_W + off + cb * LANES
                o4_ref[:, lo:lo + LANES] = part.astype(BF16)
                mid_ref[slab * d4 + r4] = part
                for m in range(step):
                    lo = (r4 + d4 * m) * ATT_W + off + cb * LANES
                    rows = pl.ds(m, tm // d16, stride=step)
                    o16_ref[:, lo:lo + LANES] = mid_ref.at[slab * d4 + r4][rows, :].astype(BF16)


def _in_attn(h, w, layer, col0, tm, name):
    t, dm = h.shape
    dilations = [d for _, d in DILATED_CONFIGS]
    piece = IN_ATTN_PIECE
    n_pieces = ATT_COLS // piece
    w_specs = [pl.BlockSpec((None, dm, piece), lambda i, p=p: (layer, 0, col0 // piece + p),
                            pipeline_mode=pl.Buffered(1)) for p in range(n_pieces)]
    out_shape, out_specs = [], []
    for d in dilations:
        for _ in range(3):
            out_shape.append(jax.ShapeDtypeStruct((t // d, d * ATT_W), BF16))
            out_specs.append(pl.BlockSpec((tm // d, d * ATT_W), lambda i: (i, 0)))
    outs = pl.pallas_call(
        _in_attn_body,
        grid=(t // tm,),
        in_specs=[pl.BlockSpec((tm, dm), lambda i: (i, 0))] + w_specs,
        out_specs=out_specs,
        out_shape=out_shape,
        scratch_shapes=[pltpu.VMEM((ATT_COLS // LANES, tm, LANES), F32),
                        pltpu.VMEM((ATT_COLS // LANES * dilations[1], tm // dilations[1], LANES), F32)],
        compiler_params=_params(("arbitrary",), 56),
        name=name,
    )(h, *([w] * n_pieces))
    return [tuple(outs[3 * di:3 * di + 3]) for di in range(len(dilations))]


def _attn_body(rb_ref, idx_ref, q_ref, kc_ref, kp_ref, vc_ref, vp_ref, o_ref, st_ref, bias_ref, *, n_sub):
    blk = ATT_BLK
    heads = range(ATT_HEADS)

    @pl.when((pl.program_id(0) == 0) & (pl.program_id(1) == 0) & (pl.program_id(2) == 0))
    def _():
        idx = idx_ref[...]
        in_prev = lax.broadcasted_iota(jnp.int32, idx.shape, 1) < blk
        for h in heads:
            acc = jnp.full(idx.shape, NEG, F32)
            for n in range(N_BUCKETS):
                acc = jnp.where(idx == n, rb_ref[n, h] * LOG2E, acc)
            bias_ref[0, h] = acc
            bias_ref[1, h] = jnp.where(in_prev, NEG, acc)

    sel0 = jnp.where(pl.program_id(2) == 0, 1, 0)
    hs = [slice(h * ATT_HDIM, (h + 1) * ATT_HDIM) for h in heads]
    lane = lax.broadcasted_iota(jnp.int32, (blk, LANES), 1)

    def qk(t):
        rows = slice(t * blk, (t + 1) * blk)
        out = []
        for h in heads:
            q = q_ref[rows, hs[h]]
            if t == 0:
                out.append((_dot_nt(q, kp_ref[:, hs[h]]) + bias_ref[sel0, h, :, 0:blk],
                            _dot_nt(q, kc_ref[rows, hs[h]]) + bias_ref[sel0, h, :, blk:2 * blk]))
            else:
                l2 = _dot_nt(q, kc_ref[(t - 1) * blk:(t + 1) * blk, hs[h]]) + bias_ref[0, h]
                out.append((l2[:, 0:blk], l2[:, blk:2 * blk]))
        return out

    ones = jnp.ones((blk, ATT_HDIM), BF16)

    def softmax(t, logits):
        probs = []
        for h in heads:
            lp, lc = logits[h]
            m = jnp.max(jnp.maximum(lp, lc), axis=-1, keepdims=True)
            probs.append((jnp.exp2(lp - m).astype(BF16), jnp.exp2(lc - m).astype(BF16), m))
        return probs

    def pv(t, probs):
        rows = slice(t * blk, (t + 1) * blk)
        stats = jnp.zeros((blk, LANES), F32)
        for h in heads:
            pp, pc, m = probs[h]
            vprev = vp_ref[:, hs[h]] if t == 0 else vc_ref[(t - 1) * blk:t * blk, hs[h]]
            od = (_dot(pp, jnp.concatenate([vprev, ones], axis=1))
                  + _dot(pc, jnp.concatenate([vc_ref[rows, hs[h]], ones], axis=1)))
            den = od[:, ATT_HDIM:]
            o_ref[rows, hs[h]] = (od[:, 0:ATT_HDIM] / den).astype(o_ref.dtype)
            stats = jnp.where(lane == h, m, stats)
            stats = jnp.where(lane == ATT_HEADS + h, den, stats)
        st_ref[rows, :] = stats

    logits = qk(0)
    for t in range(n_sub):
        probs = softmax(t, logits)
        if t + 1 < n_sub:
            logits = qk(t + 1)
        pv(t, probs)


def _attn_branch(q, k, v, rel_bias, bsz, dilation, n_sub, name):
    rows = q.shape[0]
    l = rows // bsz
    blk = ATT_BLK
    lb = n_sub * blk
    nb = l // lb
    q, k, v = (a.reshape(bsz, l, dilation * ATT_W) for a in (q, k, v))
    idx = jnp.asarray(_bucket_table(dilation))
    cur = pl.BlockSpec((None, lb, ATT_W), lambda b, r, i: (b, i, r))
    prev = pl.BlockSpec((None, blk, ATT_W), lambda b, r, i: (b, jnp.maximum(i * n_sub - 1, 0), r))
    o, st = pl.pallas_call(
        functools.partial(_attn_body, n_sub=n_sub),
        grid=(bsz, dilation, nb),
        in_specs=[
            pl.BlockSpec(memory_space=pltpu.SMEM),
            pl.BlockSpec((blk, 2 * blk), lambda b, r, i: (0, 0)),
            cur, cur, prev, cur, prev,
        ],
        out_specs=[
            pl.BlockSpec((None, lb, ATT_W), lambda b, r, i: (b, i, r)),
            pl.BlockSpec((None, lb, LANES), lambda b, r, i: (b, i, r)),
        ],
        out_shape=[
            jax.ShapeDtypeStruct((bsz, l, dilation * ATT_W), BF16),
            jax.ShapeDtypeStruct((bsz, l, dilation * LANES), F32),
        ],
        scratch_shapes=[pltpu.VMEM((2, ATT_HEADS, blk, 2 * blk), F32)],
        compiler_params=_params(("arbitrary", "arbitrary", "arbitrary"), 32),
        name=name,
    )(rel_bias, idx, q, k, k, v, v)
    return o.reshape(rows, dilation * ATT_W), st.reshape(rows, dilation * LANES)


def _merge_heads(o_refs, s_refs, scratch, tm):
    scratch = list(scratch)
    heads, stats = [], []
    for (_, d), o_ref, s_ref in zip(DILATED_CONFIGS, o_refs, s_refs):
        if d == 1:
            heads.append(lambda h, rows, o_ref=o_ref: o_ref[rows, h * ATT_HDIM:(h + 1) * ATT_HDIM].astype(F32))
            stats.append(s_ref[...])
            continue
        po_ref, ps_ref = scratch.pop(0), scratch.pop(0)
        hop = MERGE_STRIDE
        if d > hop:
            mo_ref, ms_ref = scratch.pop(0), scratch.pop(0)
        for r in range(d):
            blocks = [o_ref[:, r * ATT_W + h * ATT_HDIM:r * ATT_W + (h + 1) * ATT_HDIM].astype(F32)
                      for h in range(ATT_HEADS)] + [s_ref[:, r * LANES:(r + 1) * LANES]]
            if d <= hop:
                rows = pl.ds(r, tm // d, stride=d)
                for h in range(ATT_HEADS):
                    po_ref.at[h][rows, :] = blocks[h]
                ps_ref[rows, :] = blocks[-1]
            else:
                r_lo, m = r % hop, r // hop
                rows = pl.ds(m, tm // d, stride=d // hop)
                for h in range(ATT_HEADS):
                    mo_ref.at[r_lo * ATT_HEADS + h][rows, :] = blocks[h]
                ms_ref.at[r_lo][rows, :] = blocks[-1]
        if d > hop:
            for r_lo in range(hop):
                rows = pl.ds(r_lo, tm // hop, stride=hop)
                for h in range(ATT_HEADS):
                    po_ref.at[h][rows, :] = mo_ref[r_lo * ATT_HEADS + h]
                ps_ref[rows, :] = ms_ref[r_lo]
        heads.append(lambda h, rows, po_ref=po_ref: po_ref[h, rows, :])
        stats.append(ps_ref[...])
    mx = functools.reduce(jnp.maximum, stats)
    ws = [pltpu.roll(s, LANES - ATT_HEADS, axis=1) * jnp.exp2(s - mx) for s in stats]
    inv = 1.0 / functools.reduce(lambda a, b: a + b, ws)
    ws = [w * inv for w in ws]

    def head(h, rows):
        acc = ws[0][rows, h:h + 1] * heads[0](h, rows)
        for w, get in zip(ws[1:], heads[1:]):
            acc = acc + w[rows, h:h + 1] * get(h, rows)
        return acc

    return head


def _outproj_body(x_ref, a_ref, b_ref, *refs):
    n = len(DILATED_CONFIGS)
    o_refs, s_refs = refs[:n], refs[n:2 * n]
    w_ref, g_ref, o_ref, u_ref = refs[2 * n:2 * n + 4]
    tm = x_ref.shape[0]
    head = _merge_heads(o_refs, s_refs, refs[2 * n + 4:], tm)
    for p in range(tm // OUT_PROJ_ROWS):
        rows = slice(p * OUT_PROJ_ROWS, (p + 1) * OUT_PROJ_ROWS)
        acc = x_ref[rows, :]
        acc = acc + _dot(a_ref[rows, :], w_ref[0:HG_W, :])
        acc = acc + _dot(b_ref[rows, :], w_ref[HG_W:HG_W + RET_W, :])
        for h in range(0, ATT_HEADS, 2):
            c2 = jnp.concatenate([head(h, rows), head(h + 1, rows)], axis=1).astype(BF16)
            r0 = HG_W + RET_W + h * ATT_HDIM
            acc = acc + _dot(c2, w_ref[r0:r0 + 2 * ATT_HDIM, :])
        o_ref[rows, :] = acc
        u_ref[rows, :] = (_rms(acc) * g_ref[...]).astype(u_ref.dtype)


def _outproj(x, a, b, outs, stats, w, layer, g, tm, name):
    t, d = x.shape
    row = lambda width: pl.BlockSpec((tm, width), lambda i: (i, 0))
    in_specs, scratch = [row(d), row(HG_W), row(RET_W)], []
    for width in (ATT_W, LANES):
        for _, dil in DILATED_CONFIGS:
            in_specs.append(pl.BlockSpec((tm // dil, dil * width), lambda i: (i, 0)))
    for _, dil in DILATED_CONFIGS:
        if dil > 1:
            scratch += [pltpu.VMEM((ATT_HEADS, tm, ATT_HDIM), F32), pltpu.VMEM((tm, LANES), F32)]
        if dil > MERGE_STRIDE:
            scratch += [pltpu.VMEM((MERGE_STRIDE * ATT_HEADS, tm // MERGE_STRIDE, ATT_HDIM), F32),
                        pltpu.VMEM((MERGE_STRIDE, tm // MERGE_STRIDE, LANES), F32)]
    in_specs += [pl.BlockSpec((None,) + w.shape[1:], lambda i: (layer, 0, 0), pipeline_mode=pl.Buffered(1)),
                 pl.BlockSpec((1, d), lambda i: (0, 0))]
    return pl.pallas_call(
        _outproj_body,
        grid=(t // tm,),
        in_specs=in_specs,
        out_specs=[row(d), row(d)],
        out_shape=[jax.ShapeDtypeStruct((t, d), F32), jax.ShapeDtypeStruct((t, d), BF16)],
        scratch_shapes=scratch,
        compiler_params=_params(("arbitrary",), 56),
        name=name,
    )(x, a, b, *outs, *stats, w, g.reshape(1, d))


def _ffn1_body(u_ref, wg_ref, wu_ref, cw_ref, cb_ref, o_ref, carry_ref, *, tiles_per_seq):
    i = pl.program_id(0)
    j = pl.program_id(1)
    tm = u_ref.shape[0]
    w0 = cw_ref[0:1, :]
    w1 = cw_ref[1:2, :]
    w2 = cw_ref[2:3, :]
    cb = cb_ref[...]
    prev = jnp.where(i % tiles_per_seq == 0, 0.0, carry_ref[j])
    for p in range(tm // FFN_UP_ROWS):
        rows = slice(p * FFN_UP_ROWS, (p + 1) * FFN_UP_ROWS)
        u = u_ref[rows, :]
        gp = _dot(u, wg_ref[...])
        up = _dot(u, wu_ref[...])
        gate = w2 * gp + w1 * pltpu.roll(gp, 1, axis=0) + w0 * pltpu.roll(gp, 2, axis=0) + cb
        o_ref[rows, :] = (_silu(gate) * up).astype(o_ref.dtype)
        top = gp[0:SUBLANES, :]
        r = lax.broadcasted_iota(jnp.int32, top.shape, 0)
        p1 = prev[SUBLANES - 1:SUBLANES, :]
        p2 = prev[SUBLANES - 2:SUBLANES - 1, :]
        t1 = jnp.where(r == 0, p1, pltpu.roll(top, 1, axis=0))
        t2 = jnp.where(r == 0, p2, jnp.where(r == 1, p1, pltpu.roll(top, 2, axis=0)))
        gate_top = w2 * top + w1 * t1 + w0 * t2 + cb
        o_ref[p * FFN_UP_ROWS:p * FFN_UP_ROWS + SUBLANES, :] = (
            _silu(gate_top) * up[0:SUBLANES, :]).astype(o_ref.dtype)
        prev = gp[FFN_UP_ROWS - SUBLANES:, :]
    carry_ref[j] = prev


def _ffn1(u, wg, wu, cw, cb, layer, seq_len, tm, tn, name):
    t, d = u.shape
    n = wg.shape[2]
    nj = n // tn
    wspec = lambda rows: pl.BlockSpec((None, rows, tn), lambda i, j: (layer, 0, j))
    return pl.pallas_call(
        functools.partial(_ffn1_body, tiles_per_seq=seq_len // tm),
        grid=(t // tm, nj),
        in_specs=[pl.BlockSpec((tm, d), lambda i, j: (i, 0)), wspec(d), wspec(d), wspec(cw.shape[1]), wspec(1)],
        out_specs=pl.BlockSpec((tm, tn), lambda i, j: (i, j)),
        out_shape=jax.ShapeDtypeStruct((t, n), BF16),
        scratch_shapes=[pltpu.VMEM((nj, SUBLANES, tn), F32)],
        compiler_params=_params(("arbitrary", "arbitrary"), 48),
        name=name,
    )(u, wg, wu, cw, cb)


def _ffn2_body(x_ref, h_ref, w_ref, g_ref, *out_refs, last_layer):
    y = x_ref[...] + _dot(h_ref[...], w_ref[...])
    normed = _rms(y) * g_ref[...]
    if last_layer:
        out_refs[0][...] = normed
    else:
        out_refs[0][...] = y
        out_refs[1][...] = normed.astype(out_refs[1].dtype)


def _ffn2(x, h, w, layer, g, last_layer, tm, name):
    t, d = x.shape
    kdim = h.shape[1]
    row = pl.BlockSpec((tm, d), lambda i: (i, 0))
    out_shape = [jax.ShapeDtypeStruct((t, d), F32)]
    if not last_layer:
        out_shape.append(jax.ShapeDtypeStruct((t, d), BF16))
    return pl.pallas_call(
        functools.partial(_ffn2_body, last_layer=last_layer),
        grid=(t // tm,),
        in_specs=[
            row,
            pl.BlockSpec((tm, kdim), lambda i: (i, 0)),
            pl.BlockSpec((None, kdim, d), lambda i: (layer, 0, 0), pipeline_mode=pl.Buffered(1)),
            pl.BlockSpec((1, d), lambda i: (0, 0)),
        ],
        out_specs=[row] * len(out_shape),
        out_shape=out_shape,
        compiler_params=_params(("arbitrary",), 60),
        name=name,
    )(x, h, w, g.reshape(1, d))


def _cast_pad_body(x_ref, o_ref, *, axis):
    if axis == 0:
        o_ref[0:D_FF, :] = x_ref[...].astype(o_ref.dtype)
        o_ref[D_FF:, :] = jnp.zeros((D_FF_PAD - D_FF, o_ref.shape[1]), o_ref.dtype)
    else:
        o_ref[:, 0:D_FF] = x_ref[...].astype(o_ref.dtype)
        o_ref[:, D_FF:] = jnp.zeros((o_ref.shape[0], D_FF_PAD - D_FF), o_ref.dtype)


def _cast_pad(w, axis, tile, name):
    depth, a, b = w.shape
    if axis == 1:
        in_block, out_block, out_dims = (None, D_FF, tile), (None, D_FF_PAD, tile), (depth, D_FF_PAD, b)
        index, grid = (lambda l, i: (l, 0, i)), (depth, b // tile)
    else:
        in_block, out_block, out_dims = (None, tile, D_FF), (None, tile, D_FF_PAD), (depth, a, D_FF_PAD)
        index, grid = (lambda l, i: (l, i, 0)), (depth, a // tile)
    return pl.pallas_call(
        functools.partial(_cast_pad_body, axis=axis - 1),
        grid=grid,
        in_specs=[pl.BlockSpec(in_block, index)],
        out_specs=pl.BlockSpec(out_block, index),
        out_shape=jax.ShapeDtypeStruct(out_dims, BF16),
        compiler_params=_params(("arbitrary", "arbitrary"), 32),
        name=name,
    )(w)


def _rope_tables(s):
    inv_freq = ROPE_BASE ** (-jnp.arange(0, RET_KDIM, 2, dtype=F32) / RET_KDIM)
    ang = jnp.arange(s, dtype=F32)[:, None] * inv_freq[None, :]
    cos, sin = jnp.cos(ang), jnp.sin(ang)
    cos_t = jnp.tile(jnp.concatenate([cos, cos], axis=1), (1, LANES // RET_KDIM))
    sin_t = jnp.tile(jnp.concatenate([-sin, sin], axis=1), (1, LANES // RET_KDIM))
    return cos_t, sin_t


def kernel(x, norm_mix, w_in, hg_lower_bound, w_out, norm_ffn, w_gate, conv_w, conv_b,
           w_up, w_down, rel_bias, norm_final):
    bsz, s, d = x.shape
    t = bsz * s
    pad = D_FF_PAD - D_FF
    cos_t, sin_t = _rope_tables(s)
    x2 = x.reshape(t, d)
    w_in_bf = w_in.astype(BF16)
    w_out_bf = w_out.astype(BF16)
    wg = _cast_pad(w_gate, 2, CAST_TILE, "cast_w_gate")
    wu = _cast_pad(w_up, 2, CAST_TILE, "cast_w_up")
    wd = _cast_pad(w_down, 1, CAST_TILE, "cast_w_down")
    cw = jnp.pad(conv_w, ((0, 0), (0, 0), (0, pad)))
    cb = jnp.pad(conv_b, ((0, 0), (0, pad))).reshape(DEPTH, 1, D_FF_PAD)
    h = None
    for l in range(DEPTH):
        last = l == DEPTH - 1
        if l == 0:
            hg, h = _norm_matmul(x2, norm_mix[0], w_in_bf, l, 0, HG_COLS, F32, TM_NORM_IN, f"in_hgrn_l{l}")
        else:
            hg = _matmul(h, w_in_bf, l, 0, HG_COLS, F32, *TILE_IN_HGRN, f"in_hgrn_l{l}")
        rt = _matmul(h, w_in_bf, l, HG_COLS, RET_COLS, BF16, *TILE_IN_RET, f"in_ret_l{l}")
        qkv = _in_attn(h, w_in_bf, l, HG_COLS + RET_COLS, TM_IN_ATTN, f"in_attn_l{l}")

        a = _hgrn(hg.reshape(bsz, s, HG_COLS), hg_lower_bound, l, *HGRN_BLOCKING)
        b = _retention(rt.reshape(bsz, s, RET_COLS), cos_t, sin_t, *RET_BLOCKING, name=f"retention_l{l}")
        outs, stats = zip(*[_attn_branch(*qkv[di], rel_bias, bsz, dil, ATTN_SUBTILES, f"attn_d{dil}_l{l}")
                            for di, (_, dil) in enumerate(DILATED_CONFIGS)])
        x2, u = _outproj(x2, a.reshape(t, HG_W), b.reshape(t, RET_W), outs, stats, w_out_bf, l,
                         norm_ffn[l], TM_OUT_PROJ, f"out_proj_l{l}")
        hmid = _ffn1(u, wg, wu, cw, cb, l, s, *TILE_FFN_UP, f"ffn_gate_up_l{l}")
        res = _ffn2(x2, hmid, wd, l, norm_final if last else norm_mix[l + 1], last, TM_FFN_DOWN, f"ffn_down_l{l}")
        if last:
            x2 = res[0]
        else:
            x2, h = res
    return x2.reshape(bsz, s, d)
```

```python
import functools
import math

import numpy as np
import jax
import jax.numpy as jnp
from jax import lax
from jax.experimental import pallas as pl
from jax.experimental.pallas import tpu as pltpu

F32 = jnp.float32
BF16 = jnp.bfloat16

D_MODEL = 2048
DEPTH = 2
HG_HEADS = 4
HG_KDIM = 128
HG_VDIM = 128
RET_HEADS = 4
RET_KDIM = 64
RET_VDIM = 128
ROPE_BASE = 10000.0
ATT_HEADS = 8
ATT_HDIM = 128
DILATED_CONFIGS = ((128, 1), (512, 4), (2048, 16))
N_BUCKETS = 32
MAX_DISTANCE = 2048
D_FF = 5504
EPS = 1e-6

HG_QK = HG_HEADS * HG_KDIM
HG_W = HG_HEADS * HG_VDIM
RET_QK = RET_HEADS * RET_KDIM
RET_W = RET_HEADS * RET_VDIM
ATT_W = ATT_HEADS * ATT_HDIM
HG_COLS = 2 * HG_QK + 2 * HG_W
RET_COLS = 2 * RET_QK + 2 * RET_W
ATT_COLS = 3 * ATT_W

LANES = 128
SUBLANES = 8
D_FF_PAD = 5632
ATT_BLK = 128
IN_ATTN_PIECE = 512
MERGE_STRIDE = 4

TILE_IN_HGRN = (1024, 1024)
TM_NORM_IN = 512
TM_IN_ATTN = 512
HGRN_BLOCKING = (1024, 32, 4)
RET_BLOCKING = (1024, 128)
ATTN_SUBTILES = 8
TM_OUT_PROJ = 512
OUT_PROJ_ROWS = 256
TILE_FFN_UP = (2048, 512)
FFN_UP_ROWS = 1024
TM_FFN_DOWN = 512
CAST_TILE = 256
NEG = -1e30
LOG2E = math.log2(math.e)
MIB = 1024 * 1024


def _dot(a, b):
    return jnp.dot(a, b, preferred_element_type=F32)


def _dot_nt(a, b):
    return lax.dot_general(a, b, (((1,), (1,)), ((), ())), preferred_element_type=F32)


def _dot_tn(a, b):
    return lax.dot_general(a, b, (((0,), (0,)), ((), ())), preferred_element_type=F32)


def _params(semantics, vmem_mib):
    return pltpu.CompilerParams(dimension_semantics=semantics, vmem_limit_bytes=vmem_mib * MIB)


def _rms(x):
    return x * lax.rsqrt(jnp.mean(x * x, axis=-1, keepdims=True) + EPS)


def _silu(x):
    return x * jax.nn.sigmoid(x)


def _norm_matmul_body(x_ref, g_ref, w_ref, o_ref, h_ref):
    half = x_ref.shape[0] // 2
    for r in range(2):
        rows = slice(r * half, (r + 1) * half)
        h = (_rms(x_ref[rows, :]) * g_ref[...]).astype(h_ref.dtype)
        h_ref[rows, :] = h
        o_ref[rows, :] = _dot(h, w_ref[...]).astype(o_ref.dtype)


def _norm_matmul(x, g, w, layer, col0, n, out_dtype, tm, name):
    t, d = x.shape
    row = lambda width: pl.BlockSpec((tm, width), lambda i: (i, 0))
    return pl.pallas_call(
        _norm_matmul_body,
        grid=(t // tm,),
        in_specs=[row(d), pl.BlockSpec((1, d), lambda i: (0, 0)),
                  pl.BlockSpec((None, d, n), lambda i: (layer, 0, col0 // n), pipeline_mode=pl.Buffered(1))],
        out_specs=[row(n), row(d)],
        out_shape=[jax.ShapeDtypeStruct((t, n), out_dtype), jax.ShapeDtypeStruct((t, d), BF16)],
        compiler_params=_params(("arbitrary",), 48),
        name=name,
    )(x, g.reshape(1, d), w)


def _matmul_body(h_ref, w_ref, o_ref):
    o_ref[...] = _dot(h_ref[...], w_ref[...]).astype(o_ref.dtype)


def _matmul(h, w, layer, col0, n, out_dtype, tm, tn, name):
    t, d = h.shape
    j0 = col0 // tn
    return pl.pallas_call(
        _matmul_body,
        grid=(t // tm, n // tn),
        in_specs=[pl.BlockSpec((tm, d), lambda i, j: (i, 0)),
                  pl.BlockSpec((None, d, tn), lambda i, j: (layer, 0, j + j0))],
        out_specs=pl.BlockSpec((tm, tn), lambda i, j: (i, j)),
        out_shape=jax.ShapeDtypeStruct((t, n), out_dtype),
        compiler_params=_params(("arbitrary", "arbitrary"), 48),
        name=name,
    )(h, w)


def _hgrn_body(q_ref, f_ref, i_ref, g_ref, lbp_ref, o_ref, st_ref, b_ref, k_ref, *,
               layer, chunk, n_chunks, chunks_per_iter):
    c = chunk
    n_groups = c // SUBLANES

    @pl.when(pl.program_id(1) == 0)
    def _():
        st_ref[...] = jnp.zeros_like(st_ref)

    p = lbp_ref[...]
    e = jnp.exp(p - jnp.max(p, axis=0, keepdims=True))
    sm = e / jnp.sum(e, axis=0, keepdims=True)
    lb_all = jnp.zeros((1, HG_QK), F32)
    for m in range(1, layer + 1):
        lb_all = lb_all + sm[m:m + 1, :]

    tri = lax.broadcasted_iota(jnp.int32, (c, c), 0) >= lax.broadcasted_iota(jnp.int32, (c, c), 1)
    tri_bf = jnp.where(tri, 1.0, 0.0).astype(BF16)
    col = lax.broadcasted_iota(jnp.int32, (SUBLANES, c), 1)

    def stage_gates(rows, h):
        hs = slice(h * HG_KDIM, (h + 1) * HG_KDIM)
        lb = lb_all[:, hs]
        oml = 1.0 - lb
        xf = f_ref[rows, hs]
        sig = jax.nn.sigmoid(xf)
        log2f = jnp.log2(lb + oml * sig)
        kk = oml * (1.0 - sig)
        hi = log2f.astype(BF16)
        r1 = log2f - hi.astype(F32)
        mid = r1.astype(BF16)
        lo = (r1 - mid.astype(F32)).astype(BF16)
        cs = _dot(tri_bf, jnp.concatenate([hi, mid, lo], axis=1))
        b = cs[:, 0:LANES] + cs[:, LANES:2 * LANES] + cs[:, 2 * LANES:3 * LANES]
        return b, kk, _silu(q_ref[rows, hs]), i_ref[rows, hs].astype(BF16)

    def stage_cross(slot, b, kk, qs):
        out = []
        for g in range(1, n_groups):
            r0 = g * SUBLANES
            ref = b_ref[slot, pl.ds(r0 - 1, 1), :]
            qt = (qs[r0:r0 + SUBLANES, :] * jnp.exp2(b[r0:r0 + SUBLANES, :] - ref)).astype(BF16)
            kt = jnp.concatenate([kk[0:r0, :] * jnp.exp2(ref - b[0:r0, :]), jnp.zeros((c - r0, LANES), F32)], axis=0)
            out.append(_dot_nt(qt, kt.astype(BF16)))
        return out

    def stage_scores(slot, b, qs, cross):
        groups = []
        for g in range(n_groups):
            r0 = g * SUBLANES
            qg = qs[r0:r0 + SUBLANES, :]
            bg = b[r0:r0 + SUBLANES, :]
            ag = cross[g - 1] if g else jnp.zeros((SUBLANES, c), F32)
            for s in range(r0, r0 + SUBLANES):
                bs = b_ref[slot, pl.ds(s, 1), :]
                ks = k_ref[slot, pl.ds(s, 1), :]
                rs = jnp.sum(qg * jnp.exp2(bg - bs) * ks, axis=-1, keepdims=True)
                ag = jnp.where(col == s, rs, ag)
            groups.append(ag)
        return jnp.where(tri, jnp.concatenate(groups, axis=0), 0.0).astype(BF16)

    def one_iter(it, carry):
        streams = [(u, h) for u in range(chunks_per_iter) for h in range(HG_HEADS)]
        rows = [pl.ds(pl.multiple_of((it * chunks_per_iter + u) * c, c), c) for u in range(chunks_per_iter)]
        gates = [stage_gates(rows[u], h) for u, h in streams]
        for slot, (b, kk, _, _) in enumerate(gates):
            b_ref[slot] = b
            k_ref[slot] = kk
        cross = [stage_cross(slot, b, kk, qs) for slot, (b, kk, qs, _) in enumerate(gates)]
        scores = [stage_scores(slot, b, qs, cross[slot]) for slot, (b, _, qs, _) in enumerate(gates)]
        st = [st_ref[h] for h in range(HG_HEADS)]
        for slot, (u, h) in enumerate(streams):
            b, kk, qs, v = gates[slot]
            hs = slice(h * HG_KDIM, (h + 1) * HG_KDIM)
            b_last = b_ref[slot, pl.ds(c - 1, 1), :]
            o = _dot(scores[slot], v) + _dot_nt((qs * jnp.exp2(b)).astype(BF16), st[h].astype(BF16))
            kt = (kk * jnp.exp2(b_last - b)).astype(BF16)
            st[h] = st[h] * jnp.exp2(b_last) + _dot_tn(v, kt)
            o_ref[rows[u], hs] = (_rms(o) * _silu(g_ref[rows[u], hs])).astype(o_ref.dtype)
        for h in range(HG_HEADS):
            st_ref[h] = st[h]
        return carry

    lax.fori_loop(0, n_chunks // chunks_per_iter, one_iter, 0)


def _hgrn(hg, lb_param, layer, block, chunk, chunks_per_iter=1):
    bsz, s, _ = hg.shape
    slots = HG_HEADS * chunks_per_iter
    spec = lambda seg: pl.BlockSpec((None, block, HG_QK), lambda b, i, seg=seg: (b, i, seg))
    return pl.pallas_call(
        functools.partial(_hgrn_body, layer=layer, chunk=chunk, n_chunks=block // chunk,
                          chunks_per_iter=chunks_per_iter),
        grid=(bsz, s // block),
        in_specs=[spec(0), spec(1), spec(2), spec(3),
                  pl.BlockSpec((DEPTH, HG_QK), lambda b, i: (0, 0))],
        out_specs=pl.BlockSpec((None, block, HG_W), lambda b, i: (b, i, 0)),
        out_shape=jax.ShapeDtypeStruct((bsz, s, HG_W), BF16),
        scratch_shapes=[pltpu.VMEM((HG_HEADS, HG_VDIM, HG_KDIM), F32),
                        pltpu.VMEM((slots, chunk, LANES), F32),
                        pltpu.VMEM((slots, chunk, LANES), F32)],
        compiler_params=_params(("arbitrary", "arbitrary"), 32),
        name=f"hgrn2_l{layer}",
    )(hg, hg, hg, hg, lb_param)


def _ret_log_gamma(h):
    return math.log(1.0 - 2.0 ** (-5.0 - h))


def _ret_body(q_ref, k_ref, v_ref, g_ref, cos_ref, sin_ref, o_ref,
              st_ref, dm_ref, fs_ref, te_ref, bd_ref, *, chunk, n_chunks, chunks_per_iter):
    c = chunk
    qk = RET_QK
    lane_head = lax.broadcasted_iota(jnp.int32, (c, qk), 1) // RET_KDIM

    def lane_log_gamma(head_idx):
        lg = jnp.full(head_idx.shape, _ret_log_gamma(0), F32)
        for h in range(1, RET_HEADS):
            lg = jnp.where(head_idx == h, _ret_log_gamma(h), lg)
        return lg

    @pl.when((pl.program_id(0) == 0) & (pl.program_id(1) == 0))
    def _():
        t = lax.broadcasted_iota(jnp.int32, (c, qk), 0).astype(F32)
        lg = lane_log_gamma(lane_head)
        fs_ref[...] = jnp.exp(lg * (t + 1.0))
        te_ref[...] = jnp.exp(lg * (c - 1.0 - t))
        dist = (lax.broadcasted_iota(jnp.int32, (c, c), 0) - lax.broadcasted_iota(jnp.int32, (c, c), 1))
        for h in range(RET_HEADS):
            dm_ref[h] = jnp.where(dist >= 0, jnp.exp(_ret_log_gamma(h) * jnp.maximum(dist, 0).astype(F32)), 0.0)
        rh = lax.broadcasted_iota(jnp.int32, (qk, RET_W), 0) // RET_KDIM
        ch = lax.broadcasted_iota(jnp.int32, (qk, RET_W), 1) // RET_VDIM
        bd_ref[...] = jnp.where(rh == ch, jnp.exp(lane_log_gamma(rh) * float(c)), 0.0)

    @pl.when(pl.program_id(1) == 0)
    def _():
        st_ref[...] = jnp.zeros_like(st_ref)

    in_first_half = (lax.broadcasted_iota(jnp.int32, (c, LANES), 1) % RET_KDIM) < (RET_KDIM // 2)

    def rope(t, cos, sin):
        parts = []
        for p in range(qk // LANES):
            th = t[:, p * LANES:(p + 1) * LANES]
            back = pltpu.roll(th, RET_KDIM // 2, axis=1)
            fwd = pltpu.roll(th, LANES - RET_KDIM // 2, axis=1)
            parts.append(th * cos + jnp.where(in_first_half, fwd, back) * sin)
        return jnp.concatenate(parts, axis=1)

    heads = range(RET_HEADS)
    hs = [slice(h * RET_VDIM, (h + 1) * RET_VDIM) for h in heads]

    def one_iter(it, carry):
        units = range(chunks_per_iter)
        rows = [pl.ds(pl.multiple_of((it * chunks_per_iter + u) * c, c), c) for u in units]
        qr, kr, v = [], [], []
        for u in units:
            cos = cos_ref[rows[u], :]
            sin = sin_ref[rows[u], :]
            qr.append(rope(q_ref[rows[u], :].astype(F32), cos, sin))
            kr.append(rope(k_ref[rows[u], :].astype(F32), cos, sin) * (RET_KDIM ** -0.5))
            v.append(v_ref[rows[u], :])
        kr_bf = [k.astype(BF16) for k in kr]
        scores = [[_dot_nt(jnp.where(lane_head == h, qr[u], 0.0).astype(BF16), kr_bf[u]) for h in heads]
                  for u in units]
        upd = [_dot_tn((kr[u] * te_ref[...]).astype(BF16), v[u]) for u in units]
        bd = bd_ref[...]
        st = st_ref[...]
        inter = []
        for u in units:
            inter.append(_dot((qr[u] * fs_ref[...]).astype(BF16), st.astype(BF16)))
            st = st * bd + jnp.where(bd > 0.0, upd[u], 0.0)
        st_ref[...] = st
        for u in units:
            g = g_ref[rows[u], :].astype(F32)
            for h in heads:
                p = (scores[u][h] * dm_ref[h]).astype(BF16)
                oh = _dot(p, v[u][:, hs[h]]) + inter[u][:, hs[h]]
                o_ref[rows[u], hs[h]] = (_rms(oh) * _silu(g[:, hs[h]])).astype(o_ref.dtype)
        return carry

    lax.fori_loop(0, n_chunks // chunks_per_iter, one_iter, 0)


def _retention(rt, cos_t, sin_t, block, chunk, name, chunks_per_iter=2):
    bsz, s, _ = rt.shape
    return pl.pallas_call(
        functools.partial(_ret_body, chunk=chunk, n_chunks=block // chunk, chunks_per_iter=chunks_per_iter),
        grid=(bsz, s // block),
        in_specs=[
            pl.BlockSpec((None, block, RET_QK), lambda b, i: (b, i, 0)),
            pl.BlockSpec((None, block, RET_QK), lambda b, i: (b, i, 1)),
            pl.BlockSpec((None, block, RET_W), lambda b, i: (b, i, 1)),
            pl.BlockSpec((None, block, RET_W), lambda b, i: (b, i, 2)),
            pl.BlockSpec((block, LANES), lambda b, i: (i, 0)),
            pl.BlockSpec((block, LANES), lambda b, i: (i, 0)),
        ],
        out_specs=pl.BlockSpec((None, block, RET_W), lambda b, i: (b, i, 0)),
        out_shape=jax.ShapeDtypeStruct((bsz, s, RET_W), BF16),
        scratch_shapes=[
            pltpu.VMEM((RET_QK, RET_W), F32),
            pltpu.VMEM((RET_HEADS, chunk, chunk), F32),
            pltpu.VMEM((chunk, RET_QK), F32),
            pltpu.VMEM((chunk, RET_QK), F32),
            pltpu.VMEM((RET_QK, RET_W), F32),
        ],
        compiler_params=_params(("arbitrary", "arbitrary"), 32),
        name=name,
    )(rt, rt, rt, rt, cos_t, sin_t)


def _t5_bucket_np(distance):
    max_exact = N_BUCKETS // 2
    n = np.maximum(distance, 1).astype(np.float32)
    large = max_exact + (np.log(n / max_exact) / math.log(MAX_DISTANCE / max_exact)
                         * (N_BUCKETS - max_exact)).astype(np.int32)
    large = np.minimum(large, N_BUCKETS - 1)
    return np.where(distance < max_exact, distance, large)


def _bucket_table(dilation):
    a = np.arange(ATT_BLK)[:, None]
    kk = np.arange(2 * ATT_BLK)[None, :]
    j = a + ATT_BLK - kk
    valid = (j >= 0) & (j <= ATT_BLK)
    bucket = _t5_bucket_np(np.clip(j, 0, ATT_BLK) * dilation)
    return np.where(valid, bucket, -1).astype(np.int32)


def _in_attn_body(h_ref, *refs):
    n_pieces = ATT_COLS // IN_ATTN_PIECE
    n_ret = RET_COLS // IN_ATTN_PIECE
    w_refs, wr_refs = refs[:n_pieces], refs[n_pieces:n_pieces + n_ret]
    outs, rt_ref, (res_ref, mid_ref) = refs[n_pieces + n_ret:-3], refs[-3], refs[-2:]
    (_, d1), (_, d4), (_, d16) = DILATED_CONFIGS
    step = d16 // d4
    assert d1 == 1 and step == d4
    tm = h_ref.shape[0]
    piece = IN_ATTN_PIECE
    blocks = piece // LANES
    h = h_ref[...]
    for p in range(ATT_COLS // piece):
        if p % 2 == 1:
            q = p // 2
            rt_ref[:, q * piece:(q + 1) * piece] = _dot(h, wr_refs[q][...]).astype(BF16)
        seg, off = divmod(p * piece, ATT_W)
        o1_ref, o4_ref, o16_ref = outs[seg], outs[3 + seg], outs[6 + seg]
        val = _dot(h, w_refs[p][...])
        if seg == 0:
            val = val * (ATT_HDIM ** -0.5 * LOG2E)
        o1_ref[:, off:off + piece] = val.astype(BF16)
        for cb in range(blocks):
            slab = p * blocks + cb
            res_ref[slab] = val[:, cb * LANES:(cb + 1) * LANES]
            for r4 in range(d4):
                part = res_ref.at[slab][pl.ds(r4, tm // d4, stride=d4), :]
                lo = r4 * ATT_W + off + cb * LANES
                o4_ref[:, lo:lo + LANES] = part.astype(BF16)
                mid_ref[slab * d4 + r4] = part
                for m in range(step):
                    lo = (r4 + d4 * m) * ATT_W + off + cb * LANES
                    rows = pl.ds(m, tm // d16, stride=step)
                    o16_ref[:, lo:lo + LANES] = mid_ref.at[slab * d4 + r4][rows, :].astype(BF16)


def _in_attn(h, w, layer, col0, ret_col0, tm, name):
    t, dm = h.shape
    dilations = [d for _, d in DILATED_CONFIGS]
    piece = IN_ATTN_PIECE
    n_pieces = ATT_COLS // piece
    n_ret = RET_COLS // piece
    w_specs = [pl.BlockSpec((None, dm, piece), lambda i, c=c: (layer, 0, c), pipeline_mode=pl.Buffered(1))
               for c in ([col0 // piece + p for p in range(n_pieces)] + [ret_col0 // piece + p for p in range(n_ret)])]
    out_shape, out_specs = [], []
    for d in dilations:
        for _ in range(3):
            out_shape.append(jax.ShapeDtypeStruct((t // d, d * ATT_W), BF16))
            out_specs.append(pl.BlockSpec((tm // d, d * ATT_W), lambda i: (i, 0)))
    out_shape.append(jax.ShapeDtypeStruct((t, RET_COLS), BF16))
    out_specs.append(pl.BlockSpec((tm, RET_COLS), lambda i: (i, 0)))
    outs = pl.pallas_call(
        _in_attn_body,
        grid=(t // tm,),
        in_specs=[pl.BlockSpec((tm, dm), lambda i: (i, 0))] + w_specs,
        out_specs=out_specs,
        out_shape=out_shape,
        scratch_shapes=[pltpu.VMEM((ATT_COLS // LANES, tm, LANES), F32),
                        pltpu.VMEM((ATT_COLS // LANES * dilations[1], tm // dilations[1], LANES), F32)],
        compiler_params=_params(("arbitrary",), 60),
        name=name,
    )(h, *([w] * (n_pieces + n_ret)))
    return [tuple(outs[3 * di:3 * di + 3]) for di in range(len(dilations))], outs[-1]


def _attn_body(rb_ref, idx_ref, q_ref, kc_ref, kp_ref, vc_ref, vp_ref, o_ref, st_ref, bias_ref, *, n_sub):
    blk = ATT_BLK
    heads = range(ATT_HEADS)

    @pl.when((pl.program_id(0) == 0) & (pl.program_id(1) == 0) & (pl.program_id(2) == 0))
    def _():
        idx = idx_ref[...]
        in_prev = lax.broadcasted_iota(jnp.int32, idx.shape, 1) < blk
        for h in heads:
            acc = jnp.full(idx.shape, NEG, F32)
            for n in range(N_BUCKETS):
                acc = jnp.where(idx == n, rb_ref[n, h] * LOG2E, acc)
            bias_ref[0, h] = acc
            bias_ref[1, h] = jnp.where(in_prev, NEG, acc)

    sel0 = jnp.where(pl.program_id(2) == 0, 1, 0)
    hs = [slice(h * ATT_HDIM, (h + 1) * ATT_HDIM) for h in heads]
    lane = lax.broadcasted_iota(jnp.int32, (blk, LANES), 1)

    def qk(t):
        rows = slice(t * blk, (t + 1) * blk)
        out = []
        for h in heads:
            q = q_ref[rows, hs[h]]
            if t == 0:
                out.append((_dot_nt(q, kp_ref[:, hs[h]]) + bias_ref[sel0, h, :, 0:blk],
                            _dot_nt(q, kc_ref[rows, hs[h]]) + bias_ref[sel0, h, :, blk:2 * blk]))
            else:
                l2 = _dot_nt(q, kc_ref[(t - 1) * blk:(t + 1) * blk, hs[h]]) + bias_ref[0, h]
                out.append((l2[:, 0:blk], l2[:, blk:2 * blk]))
        return out

    ones = jnp.ones((blk, ATT_HDIM), BF16)

    def softmax(t, logits):
        probs = []
        for h in heads:
            lp, lc = logits[h]
            m = jnp.max(jnp.maximum(lp, lc), axis=-1, keepdims=True)
            probs.append((jnp.exp2(lp - m).astype(BF16), jnp.exp2(lc - m).astype(BF16), m))
        return probs

    def pv(t, probs):
        rows = slice(t * blk, (t + 1) * blk)
        stats = jnp.zeros((blk, LANES), F32)
        for h in heads:
            pp, pc, m = probs[h]
            vprev = vp_ref[:, hs[h]] if t == 0 else vc_ref[(t - 1) * blk:t * blk, hs[h]]
            od = (_dot(pp, jnp.concatenate([vprev, ones], axis=1))
                  + _dot(pc, jnp.concatenate([vc_ref[rows, hs[h]], ones], axis=1)))
            den = od[:, ATT_HDIM:]
            o_ref[rows, hs[h]] = (od[:, 0:ATT_HDIM] / den).astype(o_ref.dtype)
            stats = jnp.where(lane == h, m, stats)
            stats = jnp.where(lane == ATT_HEADS + h, den, stats)
        st_ref[rows, :] = stats

    logits = qk(0)
    for t in range(n_sub):
        probs = softmax(t, logits)
        if t + 1 < n_sub:
            logits = qk(t + 1)
        pv(t, probs)


def _attn_branch(q, k, v, rel_bias, bsz, dilation, n_sub, name):
    rows = q.shape[0]
    l = rows // bsz
    blk = ATT_BLK
    lb = n_sub * blk
    nb = l // lb
    q, k, v = (a.reshape(bsz, l, dilation * ATT_W) for a in (q, k, v))
    idx = jnp.asarray(_bucket_table(dilation))
    cur = pl.BlockSpec((None, lb, ATT_W), lambda b, r, i: (b, i, r))
    prev = pl.BlockSpec((None, blk, ATT_W), lambda b, r, i: (b, jnp.maximum(i * n_sub - 1, 0), r))
    o, st = pl.pallas_call(
        functools.partial(_attn_body, n_sub=n_sub),
        grid=(bsz, dilation, nb),
        in_specs=[
            pl.BlockSpec(memory_space=pltpu.SMEM),
            pl.BlockSpec((blk, 2 * blk), lambda b, r, i: (0, 0)),
            cur, cur, prev, cur, prev,
        ],
        out_specs=[
            pl.BlockSpec((None, lb, ATT_W), lambda b, r, i: (b, i, r)),
            pl.BlockSpec((None, lb, LANES), lambda b, r, i: (b, i, r)),
        ],
        out_shape=[
            jax.ShapeDtypeStruct((bsz, l, dilation * ATT_W), BF16),
            jax.ShapeDtypeStruct((bsz, l, dilation * LANES), F32),
        ],
        scratch_shapes=[pltpu.VMEM((2, ATT_HEADS, blk, 2 * blk), F32)],
        compiler_params=_params(("arbitrary", "arbitrary", "arbitrary"), 32),
        name=name,
    )(rel_bias, idx, q, k, k, v, v)
    return o.reshape(rows, dilation * ATT_W), st.reshape(rows, dilation * LANES)


def _merge_heads(o_refs, s_refs, scratch, tm):
    scratch = list(scratch)
    heads, stats = [], []
    for (_, d), o_ref, s_ref in zip(DILATED_CONFIGS, o_refs, s_refs):
        if d == 1:
            heads.append(lambda h, rows, o_ref=o_ref: o_ref[rows, h * ATT_HDIM:(h + 1) * ATT_HDIM].astype(F32))
            stats.append(s_ref[...])
            continue
        po_ref, ps_ref = scratch.pop(0), scratch.pop(0)
        hop = MERGE_STRIDE
        if d > hop:
            mo_ref, ms_ref = scratch.pop(0), scratch.pop(0)
        for r in range(d):
            blocks = [o_ref[:, r * ATT_W + h * ATT_HDIM:r * ATT_W + (h + 1) * ATT_HDIM].astype(F32)
                      for h in range(ATT_HEADS)] + [s_ref[:, r * LANES:(r + 1) * LANES]]
            if d <= hop:
                rows = pl.ds(r, tm // d, stride=d)
                for h in range(ATT_HEADS):
                    po_ref.at[h][rows, :] = blocks[h]
                ps_ref[rows, :] = blocks[-1]
            else:
                r_lo, m = r % hop, r // hop
                rows = pl.ds(m, tm // d, stride=d // hop)
                for h in range(ATT_HEADS):
                    mo_ref.at[r_lo * ATT_HEADS + h][rows, :] = blocks[h]
                ms_ref.at[r_lo][rows, :] = blocks[-1]
        if d > hop:
            for r_lo in range(hop):
                rows = pl.ds(r_lo, tm // hop, stride=hop)
                for h in range(ATT_HEADS):
                    po_ref.at[h][rows, :] = mo_ref[r_lo * ATT_HEADS + h]
                ps_ref[rows, :] = ms_ref[r_lo]
        heads.append(lambda h, rows, po_ref=po_ref: po_ref[h, rows, :])
        stats.append(ps_ref[...])
    mx = functools.reduce(jnp.maximum, stats)
    ws = [pltpu.roll(s, LANES - ATT_HEADS, axis=1) * jnp.exp2(s - mx) for s in stats]
    inv = 1.0 / functools.reduce(lambda a, b: a + b, ws)
    ws = [w * inv for w in ws]

    def head(h, rows):
        acc = ws[0][rows, h:h + 1] * heads[0](h, rows)
        for w, get in zip(ws[1:], heads[1:]):
            acc = acc + w[rows, h:h + 1] * get(h, rows)
        return acc

    return head


def _outproj_body(x_ref, a_ref, b_ref, *refs):
    n = len(DILATED_CONFIGS)
    o_refs, s_refs = refs[:n], refs[n:2 * n]
    w_ref, g_ref, o_ref, u_ref = refs[2 * n:2 * n + 4]
    tm = x_ref.shape[0]
    head = _merge_heads(o_refs, s_refs, refs[2 * n + 4:], tm)
    for p in range(tm // OUT_PROJ_ROWS):
        rows = slice(p * OUT_PROJ_ROWS, (p + 1) * OUT_PROJ_ROWS)
        acc = x_ref[rows, :]
        acc = acc + _dot(a_ref[rows, :], w_ref[0:HG_W, :])
        acc = acc + _dot(b_ref[rows, :], w_ref[HG_W:HG_W + RET_W, :])
        for h in range(0, ATT_HEADS, 2):
            c2 = jnp.concatenate([head(h, rows), head(h + 1, rows)], axis=1).astype(BF16)
            r0 = HG_W + RET_W + h * ATT_HDIM
            acc = acc + _dot(c2, w_ref[r0:r0 + 2 * ATT_HDIM, :])
        o_ref[rows, :] = acc
        u_ref[rows, :] = (_rms(acc) * g_ref[...]).astype(u_ref.dtype)


def _outproj(x, a, b, outs, stats, w, layer, g, tm, name):
    t, d = x.shape
    row = lambda width: pl.BlockSpec((tm, width), lambda i: (i, 0))
    in_specs, scratch = [row(d), row(HG_W), row(RET_W)], []
    for width in (ATT_W, LANES):
        for _, dil in DILATED_CONFIGS:
            in_specs.append(pl.BlockSpec((tm // dil, dil * width), lambda i: (i, 0)))
    for _, dil in DILATED_CONFIGS:
        if dil > 1:
            scratch += [pltpu.VMEM((ATT_HEADS, tm, ATT_HDIM), F32), pltpu.VMEM((tm, LANES), F32)]
        if dil > MERGE_STRIDE:
            scratch += [pltpu.VMEM((MERGE_STRIDE * ATT_HEADS, tm // MERGE_STRIDE, ATT_HDIM), F32),
                        pltpu.VMEM((MERGE_STRIDE, tm // MERGE_STRIDE, LANES), F32)]
    in_specs += [pl.BlockSpec((None,) + w.shape[1:], lambda i: (layer, 0, 0), pipeline_mode=pl.Buffered(1)),
                 pl.BlockSpec((1, d), lambda i: (0, 0))]
    return pl.pallas_call(
        _outproj_body,
        grid=(t // tm,),
        in_specs=in_specs,
        out_specs=[row(d), row(d)],
        out_shape=[jax.ShapeDtypeStruct((t, d), F32), jax.ShapeDtypeStruct((t, d), BF16)],
        scratch_shapes=scratch,
        compiler_params=_params(("arbitrary",), 56),
        name=name,
    )(x, a, b, *outs, *stats, w, g.reshape(1, d))


def _ffn1_body(u_ref, wg_ref, wu_ref, cw_ref, cb_ref, o_ref, carry_ref, *, tiles_per_seq):
    i = pl.program_id(0)
    j = pl.program_id(1)
    tm = u_ref.shape[0]
    w0 = cw_ref[0:1, :]
    w1 = cw_ref[1:2, :]
    w2 = cw_ref[2:3, :]
    cb = cb_ref[...]
    prev = jnp.where(i % tiles_per_seq == 0, 0.0, carry_ref[j])
    for p in range(tm // FFN_UP_ROWS):
        rows = slice(p * FFN_UP_ROWS, (p + 1) * FFN_UP_ROWS)
        u = u_ref[rows, :]
        gp = _dot(u, wg_ref[...])
        up = _dot(u, wu_ref[...])
        gate = w2 * gp + w1 * pltpu.roll(gp, 1, axis=0) + w0 * pltpu.roll(gp, 2, axis=0) + cb
        o_ref[rows, :] = (_silu(gate) * up).astype(o_ref.dtype)
        top = gp[0:SUBLANES, :]
        r = lax.broadcasted_iota(jnp.int32, top.shape, 0)
        p1 = prev[SUBLANES - 1:SUBLANES, :]
        p2 = prev[SUBLANES - 2:SUBLANES - 1, :]
        t1 = jnp.where(r == 0, p1, pltpu.roll(top, 1, axis=0))
        t2 = jnp.where(r == 0, p2, jnp.where(r == 1, p1, pltpu.roll(top, 2, axis=0)))
        gate_top = w2 * top + w1 * t1 + w0 * t2 + cb
        o_ref[p * FFN_UP_ROWS:p * FFN_UP_ROWS + SUBLANES, :] = (
            _silu(gate_top) * up[0:SUBLANES, :]).astype(o_ref.dtype)
        prev = gp[FFN_UP_ROWS - SUBLANES:, :]
    carry_ref[j] = prev


def _ffn1(u, wg, wu, cw, cb, layer, seq_len, tm, tn, name):
    t, d = u.shape
    n = wg.shape[2]
    nj = n // tn
    wspec = lambda rows: pl.BlockSpec((None, rows, tn), lambda i, j: (layer, 0, j))
    return pl.pallas_call(
        functools.partial(_ffn1_body, tiles_per_seq=seq_len // tm),
        grid=(t // tm, nj),
        in_specs=[pl.BlockSpec((tm, d), lambda i, j: (i, 0)), wspec(d), wspec(d), wspec(cw.shape[1]), wspec(1)],
        out_specs=pl.BlockSpec((tm, tn), lambda i, j: (i, j)),
        out_shape=jax.ShapeDtypeStruct((t, n), BF16),
        scratch_shapes=[pltpu.VMEM((nj, SUBLANES, tn), F32)],
        compiler_params=_params(("arbitrary", "arbitrary"), 48),
        name=name,
    )(u, wg, wu, cw, cb)


def _ffn2_body(x_ref, h_ref, w_ref, g_ref, *out_refs, last_layer):
    y = x_ref[...] + _dot(h_ref[...], w_ref[...])
    normed = _rms(y) * g_ref[...]
    if last_layer:
        out_refs[0][...] = normed
    else:
        out_refs[0][...] = y
        out_refs[1][...] = normed.astype(out_refs[1].dtype)


def _ffn2(x, h, w, layer, g, last_layer, tm, name):
    t, d = x.shape
    kdim = h.shape[1]
    row = pl.BlockSpec((tm, d), lambda i: (i, 0))
    out_shape = [jax.ShapeDtypeStruct((t, d), F32)]
    if not last_layer:
        out_shape.append(jax.ShapeDtypeStruct((t, d), BF16))
    return pl.pallas_call(
        functools.partial(_ffn2_body, last_layer=last_layer),
        grid=(t // tm,),
        in_specs=[
            row,
            pl.BlockSpec((tm, kdim), lambda i: (i, 0)),
            pl.BlockSpec((None, kdim, d), lambda i: (layer, 0, 0), pipeline_mode=pl.Buffered(1)),
            pl.BlockSpec((1, d), lambda i: (0, 0)),
        ],
        out_specs=[row] * len(out_shape),
        out_shape=out_shape,
        compiler_params=_params(("arbitrary",), 60),
        name=name,
    )(x, h, w, g.reshape(1, d))


def _cast_pad_body(x_ref, o_ref, *, axis):
    if axis == 0:
        o_ref[0:D_FF, :] = x_ref[...].astype(o_ref.dtype)
        o_ref[D_FF:, :] = jnp.zeros((D_FF_PAD - D_FF, o_ref.shape[1]), o_ref.dtype)
    else:
        o_ref[:, 0:D_FF] = x_ref[...].astype(o_ref.dtype)
        o_ref[:, D_FF:] = jnp.zeros((o_ref.shape[0], D_FF_PAD - D_FF), o_ref.dtype)


def _cast_pad(w, axis, tile, name):
    depth, a, b = w.shape
    if axis == 1:
        in_block, out_block, out_dims = (None, D_FF, tile), (None, D_FF_PAD, tile), (depth, D_FF_PAD, b)
        index, grid = (lambda l, i: (l, 0, i)), (depth, b // tile)
    else:
        in_block, out_block, out_dims = (None, tile, D_FF), (None, tile, D_FF_PAD), (depth, a, D_FF_PAD)
        index, grid = (lambda l, i: (l, i, 0)), (depth, a // tile)
    return pl.pallas_call(
        functools.partial(_cast_pad_body, axis=axis - 1),
        grid=grid,
        in_specs=[pl.BlockSpec(in_block, index)],
        out_specs=pl.BlockSpec(out_block, index),
        out_shape=jax.ShapeDtypeStruct(out_dims, BF16),
        compiler_params=_params(("arbitrary", "arbitrary"), 32),
        name=name,
    )(w)


def _rope_tables(s):
    inv_freq = ROPE_BASE ** (-jnp.arange(0, RET_KDIM, 2, dtype=F32) / RET_KDIM)
    ang = jnp.arange(s, dtype=F32)[:, None] * inv_freq[None, :]
    cos, sin = jnp.cos(ang), jnp.sin(ang)
    cos_t = jnp.tile(jnp.concatenate([cos, cos], axis=1), (1, LANES // RET_KDIM))
    sin_t = jnp.tile(jnp.concatenate([-sin, sin], axis=1), (1, LANES // RET_KDIM))
    return cos_t, sin_t


def kernel(x, norm_mix, w_in, hg_lower_bound, w_out, norm_ffn, w_gate, conv_w, conv_b,
           w_up, w_down, rel_bias, norm_final):
    bsz, s, d = x.shape
    t = bsz * s
    pad = D_FF_PAD - D_FF
    cos_t, sin_t = _rope_tables(s)
    x2 = x.reshape(t, d)
    w_in_bf = w_in.astype(BF16)
    w_out_bf = w_out.astype(BF16)
    wg = _cast_pad(w_gate, 2, CAST_TILE, "cast_w_gate")
    wu = _cast_pad(w_up, 2, CAST_TILE, "cast_w_up")
    wd = _cast_pad(w_down, 1, CAST_TILE, "cast_w_down")
    cw = jnp.pad(conv_w, ((0, 0), (0, 0), (0, pad)))
    cb = jnp.pad(conv_b, ((0, 0), (0, pad))).reshape(DEPTH, 1, D_FF_PAD)
    h = None
    for l in range(DEPTH):
        last = l == DEPTH - 1
        if l == 0:
            hg, h = _norm_matmul(x2, norm_mix[0], w_in_bf, l, 0, HG_COLS, F32, TM_NORM_IN, f"in_hgrn_l{l}")
        else:
            hg = _matmul(h, w_in_bf, l, 0, HG_COLS, F32, *TILE_IN_HGRN, f"in_hgrn_l{l}")
        qkv, rt = _in_attn(h, w_in_bf, l, HG_COLS + RET_COLS, HG_COLS, TM_IN_ATTN, f"in_attn_l{l}")

        a = _hgrn(hg.reshape(bsz, s, HG_COLS), hg_lower_bound, l, *HGRN_BLOCKING)
        b = _retention(rt.reshape(bsz, s, RET_COLS), cos_t, sin_t, *RET_BLOCKING, name=f"retention_l{l}")
        outs, stats = zip(*[_attn_branch(*qkv[di], rel_bias, bsz, dil, ATTN_SUBTILES, f"attn_d{dil}_l{l}")
                            for di, (_, dil) in enumerate(DILATED_CONFIGS)])
        x2, u = _outproj(x2, a.reshape(t, HG_W), b.reshape(t, RET_W), outs, stats, w_out_bf, l,
                         norm_ffn[l], TM_OUT_PROJ, f"out_proj_l{l}")
        hmid = _ffn1(u, wg, wu, cw, cb, l, s, *TILE_FFN_UP, f"ffn_gate_up_l{l}")
        res = _ffn2(x2, hmid, wd, l, norm_final if last else norm_mix[l + 1], last, TM_FFN_DOWN, f"ffn_down_l{l}")
        if last:
            x2 = res[0]
        else:
            x2, h = res
    return x2.reshape(bsz, s, d)
```
